```python
import math
import jax, jax.numpy as jnp
from jax import lax
import numpy as np

D_MODEL = 2048
BATCH = 8
SEQ = 8192
DEPTH = 4

CHUNK = 64
N_MEM = 256
N_BRANCH = 3
D_CONV = D_MODEL // 2
CONV_WIDTH = 31
ATTN_HEAD_DIM = 64
D_ATTN = D_MODEL // 2
N_ATTN_HEADS = D_ATTN // ATTN_HEAD_DIM
LEFT_CHUNKS = 8
BAND = LEFT_CHUNKS + 1
MAX_REL = 256
N_MEM_HEADS = 4
D_MEM = D_MODEL // 2
MEM_HEAD_DIM = D_MEM // N_MEM_HEADS
D_FF = ((8 * D_MODEL // 3 + 255) // 256) * 256
FFN_CONV_WIDTH = 3
D_IN = 2 * D_CONV + 3 * D_ATTN + D_MEM + N_BRANCH * D_MODEL
SPLITS = (2 * D_CONV, 2 * D_CONV + 3 * D_ATTN, 2 * D_CONV + 3 * D_ATTN + D_MEM)
EPS = 1e-6
NEG_INF = -1e30

kernel_name = 'hybrid_gated_conv_chunkattn_mem_encoder'


def rmsnorm(x, g):
    xf = x.astype(jnp.float32)
    y = xf * lax.rsqrt(jnp.mean(xf * xf, axis=-1, keepdims=True) + EPS)
    return (y * g.astype(jnp.float32)).astype(x.dtype)


def layernorm(x, g, b):
    xf = x.astype(jnp.float32)
    mu = jnp.mean(xf, axis=-1, keepdims=True)
    var = jnp.mean(jnp.square(xf - mu), axis=-1, keepdims=True)
    y = (xf - mu) * lax.rsqrt(var + EPS)
    return (y * g.astype(jnp.float32) + b.astype(jnp.float32)).astype(x.dtype)


def causal_dwconv(x, w, b):
    width = w.shape[0]
    ch = x.shape[-1]
    y = lax.conv_general_dilated(
        x, w[:, None, :].astype(x.dtype), window_strides=(1,),
        padding=[(width - 1, 0)], dimension_numbers=('NWC', 'WIO', 'NWC'),
        feature_group_count=ch)
    return y + b.astype(x.dtype)


def conv_module(u, conv_w, conv_b, ln_g, ln_b):
    a, gt = jnp.split(u, 2, axis=-1)
    h = a * jax.nn.sigmoid(gt)
    h = causal_dwconv(h, conv_w, conv_b)
    return jax.nn.silu(layernorm(h, ln_g, ln_b))


def chunk_attention(q, k, v, rel_bias):
    b, s, h, hd = q.shape
    nc = s // CHUNK
    qc = q.reshape(b, nc, CHUNK, h, hd)
    pad = ((0, 0), (LEFT_CHUNKS, 0), (0, 0), (0, 0), (0, 0))
    kp = jnp.pad(k.reshape(b, nc, CHUNK, h, hd), pad)
    vp = jnp.pad(v.reshape(b, nc, CHUNK, h, hd), pad)
    band_idx = jnp.arange(nc)[:, None] + jnp.arange(BAND)[None, :]
    kb = kp[:, band_idx].reshape(b, nc, BAND * CHUNK, h, hd)
    vb = vp[:, band_idx].reshape(b, nc, BAND * CHUNK, h, hd)
    scores = jnp.einsum('bcqhd,bckhd->bhcqk', qc, kb).astype(jnp.float32) * (hd ** -0.5)
    qi = jnp.arange(CHUNK)[:, None]
    km = jnp.arange(BAND * CHUNK)[None, :]
    dist = LEFT_CHUNKS * CHUNK + qi - km
    rel_idx = jnp.clip(dist, -MAX_REL, MAX_REL) + MAX_REL
    bias = rel_bias.astype(jnp.float32)[:, rel_idx]
    valid = jnp.repeat(band_idx >= LEFT_CHUNKS, CHUNK, axis=1)
    scores = scores + bias[None, :, None, :, :]
    scores = jnp.where(valid[None, None, :, None, :], scores, NEG_INF)
    p = jax.nn.softmax(scores, axis=-1).astype(v.dtype)
    out = jnp.einsum('bhcqk,bckhd->bcqhd', p, vb)
    return out.reshape(b, s, h * hd)


def memory_attention(q, km, vm):
    b, s, h, hd = q.shape
    scores = jnp.einsum('bshd,bmhd->bhsm', q, km).astype(jnp.float32) * (hd ** -0.5)
    p = jax.nn.softmax(scores, axis=-1).astype(vm.dtype)
    out = jnp.einsum('bhsm,bmhd->bshd', p, vm)
    return out.reshape(b, s, h * hd)


def _fwd_setup_inputs(seed: int = 0) -> dict:
    key = jax.random.key(seed)
    ks = jax.random.split(key, 24)

    def nrm(k, shape, scale):
        return jax.random.normal(k, shape, jnp.float32) * scale

    def gain(k, shape):
        return 1.0 + 0.05 * jax.random.normal(k, shape, jnp.float32)

    L = DEPTH
    return {
        'x': nrm(ks[0], (BATCH, SEQ, D_MODEL), 1.0),
        'mem': nrm(ks[1], (BATCH, N_MEM, D_MODEL), 1.0),
        'mix_norm_g': gain(ks[2], (L, D_MODEL)),
        'mem_norm_g': gain(ks[3], (L, D_MODEL)),
        'w_in': nrm(ks[4], (L, D_MODEL, D_IN), D_MODEL ** -0.5),
        'gate_b': nrm(ks[5], (L, N_BRANCH * D_MODEL), 0.1),
        'conv_w': nrm(ks[6], (L, CONV_WIDTH, D_CONV), CONV_WIDTH ** -0.5),
        'conv_b': nrm(ks[7], (L, D_CONV), 0.02),
        'conv_ln_g': gain(ks[8], (L, D_CONV)),
        'conv_ln_b': nrm(ks[9], (L, D_CONV), 0.02),
        'w_conv_out': nrm(ks[10], (L, D_CONV, D_MODEL), D_CONV ** -0.5),
        'rel_bias': nrm(ks[11], (L, N_ATTN_HEADS, 2 * MAX_REL + 1), 0.1),
        'w_attn_out': nrm(ks[12], (L, D_ATTN, D_MODEL), D_ATTN ** -0.5),
        'w_mem_kv': nrm(ks[13], (L, D_MODEL, 2 * D_MEM), D_MODEL ** -0.5),
        'w_mem_out': nrm(ks[14], (L, D_MEM, D_MODEL), D_MEM ** -0.5),
        'w_o': nrm(ks[15], (L, D_MODEL, D_MODEL), D_MODEL ** -0.5),
        'ffn_norm_g': gain(ks[16], (L, D_MODEL)),
        'w_up': nrm(ks[17], (L, D_MODEL, 2 * D_FF), D_MODEL ** -0.5),
        'ffn_conv_w': nrm(ks[18], (L, FFN_CONV_WIDTH, 2 * D_FF), FFN_CONV_WIDTH ** -0.5),
        'ffn_conv_b': nrm(ks[19], (L, 2 * D_FF), 0.02),
        'w_down': nrm(ks[20], (L, D_FF, D_MODEL), D_FF ** -0.5),
        'final_norm_g': gain(ks[21], (D_MODEL,)),
    }


def _fwd_reference(x, mem, mix_norm_g, mem_norm_g, w_in, gate_b, conv_w, conv_b, conv_ln_g, conv_ln_b,
              w_conv_out, rel_bias, w_attn_out, w_mem_kv, w_mem_out, w_o, ffn_norm_g, w_up,
              ffn_conv_w, ffn_conv_b, w_down, final_norm_g):
    b, s, d = x.shape
    h = x
    for l in range(DEPTH):
        xn = rmsnorm(h, mix_norm_g[l])
        proj = xn @ w_in[l]
        u_conv, qkv, q_mem, gates = jnp.split(proj, SPLITS, axis=-1)
        y_conv = conv_module(u_conv, conv_w[l], conv_b[l], conv_ln_g[l], conv_ln_b[l]) @ w_conv_out[l]
        q, k, v = jnp.split(qkv.reshape(b, s, 3, N_ATTN_HEADS, ATTN_HEAD_DIM), 3, axis=2)
        y_attn = chunk_attention(q[:, :, 0], k[:, :, 0], v[:, :, 0], rel_bias[l]) @ w_attn_out[l]
        mn = rmsnorm(mem, mem_norm_g[l])
        kv = (mn @ w_mem_kv[l]).reshape(b, mem.shape[1], 2, N_MEM_HEADS, MEM_HEAD_DIM)
        qm = q_mem.reshape(b, s, N_MEM_HEADS, MEM_HEAD_DIM)
        y_mem = memory_attention(qm, kv[:, :, 0], kv[:, :, 1]) @ w_mem_out[l]
        g = jax.nn.sigmoid(gates + gate_b[l]).reshape(b, s, N_BRANCH, d)
        merged = g[:, :, 0] * y_conv + g[:, :, 1] * y_attn + g[:, :, 2] * y_mem
        h = h + merged @ w_o[l]
        hn = rmsnorm(h, ffn_norm_g[l])
        up = causal_dwconv(hn @ w_up[l], ffn_conv_w[l], ffn_conv_b[l])
        val, gt = jnp.split(up, 2, axis=-1)
        h = h + (jax.nn.silu(gt) * val) @ w_down[l]
    return rmsnorm(h, final_norm_g)


import jax as _jax
import jax.numpy as _jnp

TWIN_FORMAT = 'train_step'
FWD_PARAMS = ['x', 'mem', 'mix_norm_g', 'mem_norm_g', 'w_in', 'gate_b', 'conv_w', 'conv_b', 'conv_ln_g', 'conv_ln_b', 'w_conv_out', 'rel_bias', 'w_attn_out', 'w_mem_kv', 'w_mem_out', 'w_o', 'ffn_norm_g', 'w_up', 'ffn_conv_w', 'ffn_conv_b', 'w_down', 'final_norm_g']
TWIN_WEIGHTS = ['mix_norm_g', 'mem_norm_g', 'w_in', 'gate_b', 'conv_w', 'conv_b', 'conv_ln_g', 'conv_ln_b', 'w_conv_out', 'rel_bias', 'w_attn_out', 'w_mem_kv', 'w_mem_out', 'w_o', 'ffn_norm_g', 'w_up', 'ffn_conv_w', 'ffn_conv_b', 'w_down', 'final_norm_g']
TWIN_DIFF_INPUT = 'x'
TWIN_INPUTS = ['x', 'mem', 'mix_norm_g', 'mem_norm_g', 'w_in', 'gate_b', 'conv_w', 'conv_b', 'conv_ln_g', 'conv_ln_b', 'w_conv_out', 'rel_bias', 'w_attn_out', 'w_mem_kv', 'w_mem_out', 'w_o', 'ffn_norm_g', 'w_up', 'ffn_conv_w', 'ffn_conv_b', 'w_down', 'final_norm_g', 'loss_target', 'm_mix_norm_g', 'm_mem_norm_g', 'm_w_in', 'm_gate_b', 'm_conv_w', 'm_conv_b', 'm_conv_ln_g', 'm_conv_ln_b', 'm_w_conv_out', 'm_rel_bias', 'm_w_attn_out', 'm_w_mem_kv', 'm_w_mem_out', 'm_w_o', 'm_ffn_norm_g', 'm_w_up', 'm_ffn_conv_w', 'm_ffn_conv_b', 'm_w_down', 'm_final_norm_g', 'v_mix_norm_g', 'v_mem_norm_g', 'v_w_in', 'v_gate_b', 'v_conv_w', 'v_conv_b', 'v_conv_ln_g', 'v_conv_ln_b', 'v_w_conv_out', 'v_rel_bias', 'v_w_attn_out', 'v_w_mem_kv', 'v_w_mem_out', 'v_w_o', 'v_ffn_norm_g', 'v_w_up', 'v_ffn_conv_w', 'v_ffn_conv_b', 'v_w_down', 'v_final_norm_g']
TWIN_OUTPUTS = ['loss', 'grad_x', 'grad_mix_norm_g', 'grad_mem_norm_g', 'grad_w_in', 'grad_gate_b', 'grad_conv_w', 'grad_conv_b', 'grad_conv_ln_g', 'grad_conv_ln_b', 'grad_w_conv_out', 'grad_rel_bias', 'grad_w_attn_out', 'grad_w_mem_kv', 'grad_w_mem_out', 'grad_w_o', 'grad_ffn_norm_g', 'grad_w_up', 'grad_ffn_conv_w', 'grad_ffn_conv_b', 'grad_w_down', 'grad_final_norm_g', 'delta_mix_norm_g', 'delta_mem_norm_g', 'delta_w_in', 'delta_gate_b', 'delta_conv_w', 'delta_conv_b', 'delta_conv_ln_g', 'delta_conv_ln_b', 'delta_w_conv_out', 'delta_rel_bias', 'delta_w_attn_out', 'delta_w_mem_kv', 'delta_w_mem_out', 'delta_w_o', 'delta_ffn_norm_g', 'delta_w_up', 'delta_ffn_conv_w', 'delta_ffn_conv_b', 'delta_w_down', 'delta_final_norm_g', 'new_m_mix_norm_g', 'new_m_mem_norm_g', 'new_m_w_in', 'new_m_gate_b', 'new_m_conv_w', 'new_m_conv_b', 'new_m_conv_ln_g', 'new_m_conv_ln_b', 'new_m_w_conv_out', 'new_m_rel_bias', 'new_m_w_attn_out', 'new_m_w_mem_kv', 'new_m_w_mem_out', 'new_m_w_o', 'new_m_ffn_norm_g', 'new_m_w_up', 'new_m_ffn_conv_w', 'new_m_ffn_conv_b', 'new_m_w_down', 'new_m_final_norm_g', 'new_v_mix_norm_g', 'new_v_mem_norm_g', 'new_v_w_in', 'new_v_gate_b', 'new_v_conv_w', 'new_v_conv_b', 'new_v_conv_ln_g', 'new_v_conv_ln_b', 'new_v_w_conv_out', 'new_v_rel_bias', 'new_v_w_attn_out', 'new_v_w_mem_kv', 'new_v_w_mem_out', 'new_v_w_o', 'new_v_ffn_norm_g', 'new_v_w_up', 'new_v_ffn_conv_w', 'new_v_ffn_conv_b', 'new_v_w_down', 'new_v_final_norm_g']
TWIN_LEAF_KINDS = {'loss': 'loss', 'grad_x': 'grad_x', 'grad_mix_norm_g': 'grad_w', 'grad_mem_norm_g': 'grad_w', 'grad_w_in': 'grad_w', 'grad_gate_b': 'grad_w', 'grad_conv_w': 'grad_w', 'grad_conv_b': 'grad_w', 'grad_conv_ln_g': 'grad_w', 'grad_conv_ln_b': 'grad_w', 'grad_w_conv_out': 'grad_w', 'grad_rel_bias': 'grad_w', 'grad_w_attn_out': 'grad_w', 'grad_w_mem_kv': 'grad_w', 'grad_w_mem_out': 'grad_w', 'grad_w_o': 'grad_w', 'grad_ffn_norm_g': 'grad_w', 'grad_w_up': 'grad_w', 'grad_ffn_conv_w': 'grad_w', 'grad_ffn_conv_b': 'grad_w', 'grad_w_down': 'grad_w', 'grad_final_norm_g': 'grad_w', 'delta_mix_norm_g': 'delta_w', 'delta_mem_norm_g': 'delta_w', 'delta_w_in': 'delta_w', 'delta_gate_b': 'delta_w', 'delta_conv_w': 'delta_w', 'delta_conv_b': 'delta_w', 'delta_conv_ln_g': 'delta_w', 'delta_conv_ln_b': 'delta_w', 'delta_w_conv_out': 'delta_w', 'delta_rel_bias': 'delta_w', 'delta_w_attn_out': 'delta_w', 'delta_w_mem_kv': 'delta_w', 'delta_w_mem_out': 'delta_w', 'delta_w_o': 'delta_w', 'delta_ffn_norm_g': 'delta_w', 'delta_w_up': 'delta_w', 'delta_ffn_conv_w': 'delta_w', 'delta_ffn_conv_b': 'delta_w', 'delta_w_down': 'delta_w', 'delta_final_norm_g': 'delta_w', 'new_m_mix_norm_g': 'new_m', 'new_m_mem_norm_g': 'new_m', 'new_m_w_in': 'new_m', 'new_m_gate_b': 'new_m', 'new_m_conv_w': 'new_m', 'new_m_conv_b': 'new_m', 'new_m_conv_ln_g': 'new_m', 'new_m_conv_ln_b': 'new_m', 'new_m_w_conv_out': 'new_m', 'new_m_rel_bias': 'new_m', 'new_m_w_attn_out': 'new_m', 'new_m_w_mem_kv': 'new_m', 'new_m_w_mem_out': 'new_m', 'new_m_w_o': 'new_m', 'new_m_ffn_norm_g': 'new_m', 'new_m_w_up': 'new_m', 'new_m_ffn_conv_w': 'new_m', 'new_m_ffn_conv_b': 'new_m', 'new_m_w_down': 'new_m', 'new_m_final_norm_g': 'new_m', 'new_v_mix_norm_g': 'new_v', 'new_v_mem_norm_g': 'new_v', 'new_v_w_in': 'new_v', 'new_v_gate_b': 'new_v', 'new_v_conv_w': 'new_v', 'new_v_conv_b': 'new_v', 'new_v_conv_ln_g': 'new_v', 'new_v_conv_ln_b': 'new_v', 'new_v_w_conv_out': 'new_v', 'new_v_rel_bias': 'new_v', 'new_v_w_attn_out': 'new_v', 'new_v_w_mem_kv': 'new_v', 'new_v_w_mem_out': 'new_v', 'new_v_w_o': 'new_v', 'new_v_ffn_norm_g': 'new_v', 'new_v_w_up': 'new_v', 'new_v_ffn_conv_w': 'new_v', 'new_v_ffn_conv_b': 'new_v', 'new_v_w_down': 'new_v', 'new_v_final_norm_g': 'new_v'}


def _forward(args):
    return _fwd_reference(*[args[k] for k in FWD_PARAMS])


def _output_shape():
    def fwd():
        inp = _fwd_setup_inputs(0)
        return _fwd_reference(*[inp[k] for k in FWD_PARAMS])
    out = _jax.eval_shape(fwd)
    return out.shape, out.dtype

N_MICROBATCH = 1
ADAM_LR = 0.001
ADAM_B1 = 0.9
ADAM_B2 = 0.999
ADAM_EPS = 1e-08
ADAM_WD = 0.01
ADAM_STEP = 10
PER_EXAMPLE_BATCH_AXIS = {'x': 0, 'mem': 0, 'loss_target': 0}
SHARED_INPUTS = []
_WEIGHT_DTYPES = {'mix_norm_g': _jnp.float32, 'mem_norm_g': _jnp.float32, 'w_in': _jnp.float32, 'gate_b': _jnp.float32, 'conv_w': _jnp.float32, 'conv_b': _jnp.float32, 'conv_ln_g': _jnp.float32, 'conv_ln_b': _jnp.float32, 'w_conv_out': _jnp.float32, 'rel_bias': _jnp.float32, 'w_attn_out': _jnp.float32, 'w_mem_kv': _jnp.float32, 'w_mem_out': _jnp.float32, 'w_o': _jnp.float32, 'ffn_norm_g': _jnp.float32, 'w_up': _jnp.float32, 'ffn_conv_w': _jnp.float32, 'ffn_conv_b': _jnp.float32, 'w_down': _jnp.float32, 'final_norm_g': _jnp.float32}
MOMENT_SCALE = {'mix_norm_g': 5.206491e-02, 'mem_norm_g': 1.047475e-02, 'w_in': 2.156048e-02, 'gate_b': 1.102722e-02, 'conv_w': 6.082552e-02, 'conv_b': 1.504758e-01, 'conv_ln_g': 8.732314e-02, 'conv_ln_b': 9.347952e-02, 'w_conv_out': 4.646712e-02, 'rel_bias': 4.714078e-03, 'w_attn_out': 1.264461e-02, 'w_mem_kv': 1.035624e-02, 'w_mem_out': 7.607914e-03, 'w_o': 4.789330e-02, 'ffn_norm_g': 9.154076e-02, 'w_up': 3.841640e-02, 'ffn_conv_w': 3.883635e-02, 'ffn_conv_b': 4.372215e-02, 'w_down': 6.305464e-02, 'final_norm_g': 3.205645e+01}


def _to_microbatches(a, axis):
    t = _jnp.moveaxis(a, axis, 0)
    t = t.reshape((N_MICROBATCH, t.shape[0] // N_MICROBATCH) + t.shape[1:])
    return _jnp.moveaxis(t, 1, axis + 1)


def setup_inputs(seed: int = 0) -> dict:
    inp = _fwd_setup_inputs(seed)
    key = _jax.random.fold_in(_jax.random.key(seed), 7919)
    shape, _ = _output_shape()
    out = dict(inp)
    out["loss_target"] = _jax.random.normal(_jax.random.fold_in(key, 0), shape, _jnp.float32)
    for i, name in enumerate(TWIN_WEIGHTS):
        w = inp[name].astype(_jnp.float32)
        if MOMENT_SCALE is None:
            s = _jnp.sqrt(_jnp.mean(_jnp.square(w)) + 1e-30)
        else:
            s = MOMENT_SCALE[name]
        km, kv = _jax.random.split(_jax.random.fold_in(key, i + 1))
        out[name] = w
        out["m_" + name] = s * _jax.random.normal(km, w.shape, _jnp.float32)
        out["v_" + name] = (s * s) * _jax.random.uniform(kv, w.shape, _jnp.float32, 0.5, 1.5)
    if N_MICROBATCH > 1:
        for name, axis in PER_EXAMPLE_BATCH_AXIS.items():
            out[name] = _to_microbatches(out[name], axis)
    return {'x': out['x'], 'mem': out['mem'], 'mix_norm_g': out['mix_norm_g'], 'mem_norm_g': out['mem_norm_g'], 'w_in': out['w_in'], 'gate_b': out['gate_b'], 'conv_w': out['conv_w'], 'conv_b': out['conv_b'], 'conv_ln_g': out['conv_ln_g'], 'conv_ln_b': out['conv_ln_b'], 'w_conv_out': out['w_conv_out'], 'rel_bias': out['rel_bias'], 'w_attn_out': out['w_attn_out'], 'w_mem_kv': out['w_mem_kv'], 'w_mem_out': out['w_mem_out'], 'w_o': out['w_o'], 'ffn_norm_g': out['ffn_norm_g'], 'w_up': out['w_up'], 'ffn_conv_w': out['ffn_conv_w'], 'ffn_conv_b': out['ffn_conv_b'], 'w_down': out['w_down'], 'final_norm_g': out['final_norm_g'], 'loss_target': out['loss_target'], 'm_mix_norm_g': out['m_mix_norm_g'], 'm_mem_norm_g': out['m_mem_norm_g'], 'm_w_in': out['m_w_in'], 'm_gate_b': out['m_gate_b'], 'm_conv_w': out['m_conv_w'], 'm_conv_b': out['m_conv_b'], 'm_conv_ln_g': out['m_conv_ln_g'], 'm_conv_ln_b': out['m_conv_ln_b'], 'm_w_conv_out': out['m_w_conv_out'], 'm_rel_bias': out['m_rel_bias'], 'm_w_attn_out': out['m_w_attn_out'], 'm_w_mem_kv': out['m_w_mem_kv'], 'm_w_mem_out': out['m_w_mem_out'], 'm_w_o': out['m_w_o'], 'm_ffn_norm_g': out['m_ffn_norm_g'], 'm_w_up': out['m_w_up'], 'm_ffn_conv_w': out['m_ffn_conv_w'], 'm_ffn_conv_b': out['m_ffn_conv_b'], 'm_w_down': out['m_w_down'], 'm_final_norm_g': out['m_final_norm_g'], 'v_mix_norm_g': out['v_mix_norm_g'], 'v_mem_norm_g': out['v_mem_norm_g'], 'v_w_in': out['v_w_in'], 'v_gate_b': out['v_gate_b'], 'v_conv_w': out['v_conv_w'], 'v_conv_b': out['v_conv_b'], 'v_conv_ln_g': out['v_conv_ln_g'], 'v_conv_ln_b': out['v_conv_ln_b'], 'v_w_conv_out': out['v_w_conv_out'], 'v_rel_bias': out['v_rel_bias'], 'v_w_attn_out': out['v_w_attn_out'], 'v_w_mem_kv': out['v_w_mem_kv'], 'v_w_mem_out': out['v_w_mem_out'], 'v_w_o': out['v_w_o'], 'v_ffn_norm_g': out['v_ffn_norm_g'], 'v_w_up': out['v_w_up'], 'v_ffn_conv_w': out['v_ffn_conv_w'], 'v_ffn_conv_b': out['v_ffn_conv_b'], 'v_w_down': out['v_w_down'], 'v_final_norm_g': out['v_final_norm_g']}


def _loss(weights, diff, rest, loss_target):
    with _jax.named_scope("forward"):
        args = {**rest, TWIN_DIFF_INPUT: diff, **{k: w.astype(_WEIGHT_DTYPES[k]) for k, w in weights.items()}}
        y = _forward(args)
    with _jax.named_scope("loss_head"):
        err = _jnp.square(y.astype(_jnp.float32) - loss_target)
        return 0.5 * _jnp.sum(_jnp.mean(err, axis=-1)) if err.ndim else 0.5 * err


def _adamw(w, g, m, v):
    m = ADAM_B1 * m + (1.0 - ADAM_B1) * g
    v = ADAM_B2 * v + (1.0 - ADAM_B2) * _jnp.square(g)
    m_hat = m / (1.0 - ADAM_B1 ** ADAM_STEP)
    v_hat = v / (1.0 - ADAM_B2 ** ADAM_STEP)
    delta = -ADAM_LR * (m_hat / (_jnp.sqrt(v_hat) + ADAM_EPS) + ADAM_WD * w)
    return delta, m, v


def reference(x, mem, mix_norm_g, mem_norm_g, w_in, gate_b, conv_w, conv_b, conv_ln_g, conv_ln_b, w_conv_out, rel_bias, w_attn_out, w_mem_kv, w_mem_out, w_o, ffn_norm_g, w_up, ffn_conv_w, ffn_conv_b, w_down, final_norm_g, loss_target, m_mix_norm_g, m_mem_norm_g, m_w_in, m_gate_b, m_conv_w, m_conv_b, m_conv_ln_g, m_conv_ln_b, m_w_conv_out, m_rel_bias, m_w_attn_out, m_w_mem_kv, m_w_mem_out, m_w_o, m_ffn_norm_g, m_w_up, m_ffn_conv_w, m_ffn_conv_b, m_w_down, m_final_norm_g, v_mix_norm_g, v_mem_norm_g, v_w_in, v_gate_b, v_conv_w, v_conv_b, v_conv_ln_g, v_conv_ln_b, v_w_conv_out, v_rel_bias, v_w_attn_out, v_w_mem_kv, v_w_mem_out, v_w_o, v_ffn_norm_g, v_w_up, v_ffn_conv_w, v_ffn_conv_b, v_w_down, v_final_norm_g):
    given = dict(x=x, mem=mem, mix_norm_g=mix_norm_g, mem_norm_g=mem_norm_g, w_in=w_in, gate_b=gate_b, conv_w=conv_w, conv_b=conv_b, conv_ln_g=conv_ln_g, conv_ln_b=conv_ln_b, w_conv_out=w_conv_out, rel_bias=rel_bias, w_attn_out=w_attn_out, w_mem_kv=w_mem_kv, w_mem_out=w_mem_out, w_o=w_o, ffn_norm_g=ffn_norm_g, w_up=w_up, ffn_conv_w=ffn_conv_w, ffn_conv_b=ffn_conv_b, w_down=w_down, final_norm_g=final_norm_g, loss_target=loss_target, m_mix_norm_g=m_mix_norm_g, m_mem_norm_g=m_mem_norm_g, m_w_in=m_w_in, m_gate_b=m_gate_b, m_conv_w=m_conv_w, m_conv_b=m_conv_b, m_conv_ln_g=m_conv_ln_g, m_conv_ln_b=m_conv_ln_b, m_w_conv_out=m_w_conv_out, m_rel_bias=m_rel_bias, m_w_attn_out=m_w_attn_out, m_w_mem_kv=m_w_mem_kv, m_w_mem_out=m_w_mem_out, m_w_o=m_w_o, m_ffn_norm_g=m_ffn_norm_g, m_w_up=m_w_up, m_ffn_conv_w=m_ffn_conv_w, m_ffn_conv_b=m_ffn_conv_b, m_w_down=m_w_down, m_final_norm_g=m_final_norm_g, v_mix_norm_g=v_mix_norm_g, v_mem_norm_g=v_mem_norm_g, v_w_in=v_w_in, v_gate_b=v_gate_b, v_conv_w=v_conv_w, v_conv_b=v_conv_b, v_conv_ln_g=v_conv_ln_g, v_conv_ln_b=v_conv_ln_b, v_w_conv_out=v_w_conv_out, v_rel_bias=v_rel_bias, v_w_attn_out=v_w_attn_out, v_w_mem_kv=v_w_mem_kv, v_w_mem_out=v_w_mem_out, v_w_o=v_w_o, v_ffn_norm_g=v_ffn_norm_g, v_w_up=v_w_up, v_ffn_conv_w=v_ffn_conv_w, v_ffn_conv_b=v_ffn_conv_b, v_w_down=v_w_down, v_final_norm_g=v_final_norm_g)
    weights = {n: given[n] for n in TWIN_WEIGHTS}
    shared = {n: given[n] for n in SHARED_INPUTS}
    per_example = {n: given[n] for n in ['x', 'mem']}
    grad_fn = _jax.value_and_grad(_loss, argnums=(0, 1))

    def one_microbatch(ex, loss_target):
        ex = dict(ex)
        diff = ex.pop(TWIN_DIFF_INPUT)
        return grad_fn(weights, diff, {**shared, **ex}, loss_target)

    if N_MICROBATCH == 1:
        loss, (grad_w, grad_x) = one_microbatch(per_example, given["loss_target"])
    else:
        def body(carry, xs):
            loss_sum, grad_sum = carry
            l_k, (gw_k, gx_k) = one_microbatch(xs[0], xs[1])
            with _jax.named_scope("update"):
                return (loss_sum + l_k, _jax.tree.map(_jnp.add, grad_sum, gw_k)), gx_k

        init = (_jnp.zeros((), _jnp.float32), _jax.tree.map(_jnp.zeros_like, weights))
        (loss, grad_w), grad_x = _jax.lax.scan(body, init, (per_example, given["loss_target"]))
    with _jax.named_scope("update"):
        delta_w, new_m, new_v = {}, {}, {}
        for n in TWIN_WEIGHTS:
            delta_w[n], new_m[n], new_v[n] = _adamw(weights[n], grad_w[n], given["m_" + n], given["v_" + n])
    return (loss, grad_x, *[grad_w[n] for n in TWIN_WEIGHTS], *[delta_w[n] for n in TWIN_WEIGHTS],
            *[new_m[n] for n in TWIN_WEIGHTS], *[new_v[n] for n in TWIN_WEIGHTS])
```

```python
import functools

import numpy as np
import jax
import jax.numpy as jnp
from jax import lax
from jax.experimental import pallas as pl
from jax.experimental.pallas import tpu as pltpu

F32 = jnp.float32
BF16 = jnp.bfloat16

CHUNK = 64
LEFT_CHUNKS = 8
MAX_REL = 256
N_MEM_HEADS = 4
ATTN_HEAD_DIM = 64
CONV_WIDTH = 31
FFN_CONV_WIDTH = 3
EPS = 1e-6
NEG_INF = -1e30
ADAM_LR = 0.001
ADAM_B1 = 0.9
ADAM_B2 = 0.999
ADAM_EPS = 1e-08
ADAM_WD = 0.01
ADAM_STEP = 10
N_DEV = 8

VMEM_LIMIT_BYTES = 56 * 1024 * 1024
LANES = 128


def _cparams(sem=None):
    return pltpu.CompilerParams(dimension_semantics=sem, vmem_limit_bytes=VMEM_LIMIT_BYTES)


def _tile(n, want):
    if n <= want:
        return n
    t = (want // LANES) * LANES
    while t >= LANES:
        if n % t == 0:
            return t if 4 * t >= want or n > 2 * want else n
        t -= LANES
    return n


def mm_nn(a, w, *, out_dtype, residual=None, tm=1024, tn=1024, tk=512, name):
    M, K = a.shape
    S, K2, Ns = w.shape
    assert K == K2
    tm, tn, tk = _tile(M, tm), _tile(Ns, tn), _tile(K, tk)
    nb = Ns // tn
    nk = K // tk
    grid = (M // tm, S * nb, nk)

    def body(*refs):
        if residual is None:
            a_ref, w_ref, o_ref, acc = refs
        else:
            a_ref, w_ref, r_ref, o_ref, acc = refs
        k = pl.program_id(2)

        @pl.when(k == 0)
        def _():
            acc[...] = jnp.zeros_like(acc)

        acc[...] += jnp.dot(a_ref[...], w_ref[...], preferred_element_type=F32)

        @pl.when(k == nk - 1)
        def _():
            r = acc[...]
            if residual is not None:
                r = r + r_ref[...]
            o_ref[...] = r.astype(o_ref.dtype)

    in_specs = [pl.BlockSpec((tm, tk), lambda i, j, k: (i, k)),
                pl.BlockSpec((None, tk, tn), lambda i, j, k: (j // nb, k, j % nb))]
    args = [a, w]
    if residual is not None:
        in_specs.append(pl.BlockSpec((tm, tn), lambda i, j, k: (i, j)))
        args.append(residual)
    return pl.pallas_call(
        body, name=name, grid=grid, in_specs=in_specs,
        out_specs=pl.BlockSpec((tm, tn), lambda i, j, k: (i, j)),
        out_shape=jax.ShapeDtypeStruct((M, S * Ns), out_dtype),
        scratch_shapes=[pltpu.VMEM((tm, tn), F32)],
        compiler_params=_cparams(("parallel", "parallel", "arbitrary")),
    )(*args)


def _lead_spec(arr, lead, block, index):
    if lead is None:
        assert arr.ndim == 2
        return pl.BlockSpec(block, index)
    assert arr.ndim == 3
    return pl.BlockSpec((None, *block), lambda i, j, k: (lead, *index(i, j, k)))


def mm_nt(a, w, *, out_dtype, lead=None, tm=1024, tn=1024, tk=1024, name):
    M, N = a.shape[-2:]
    S, K, Ns = w.shape
    assert N == S * Ns
    tm, tn, tk = _tile(M, tm), _tile(K, tn), _tile(Ns, tk)
    nb = Ns // tk
    nk = S * nb
    grid = (M // tm, K // tn, nk)

    def body(a_ref, w_ref, o_ref, acc):
        k = pl.program_id(2)

        @pl.when(k == 0)
        def _():
            acc[...] = jnp.zeros_like(acc)

        acc[...] += lax.dot_general(a_ref[...], w_ref[...], (((1,), (1,)), ((), ())),
                                    preferred_element_type=F32)

        @pl.when(k == nk - 1)
        def _():
            o_ref[...] = acc[...].astype(o_ref.dtype)

    return pl.pallas_call(
        body, name=name, grid=grid,
        in_specs=[_lead_spec(a, lead, (tm, tk), lambda i, j, k: (i, k)),
                  pl.BlockSpec((None, tn, tk), lambda i, j, k: (k // nb, j, k % nb))],
        out_specs=pl.BlockSpec((tm, tn), lambda i, j, k: (i, j)),
        out_shape=jax.ShapeDtypeStruct((M, K), out_dtype),
        scratch_shapes=[pltpu.VMEM((tm, tn), F32)],
        compiler_params=_cparams(("parallel", "parallel", "arbitrary")),
    )(a, w)


def mm_tn(a, b, *, slots, out_dtype, lead=None, tm=1024, tn=1024, tk=512, name):
    T, K = a.shape
    T2, N = b.shape[-2:]
    assert T == T2 and N % slots == 0
    Ns = N // slots
    tm, tn, tk = _tile(K, tm), _tile(Ns, tn), _tile(T, tk)
    nb = Ns // tn
    nk = T // tk
    grid = (K // tm, slots * nb, nk)

    def body(a_ref, b_ref, o_ref, acc):
        k = pl.program_id(2)

        @pl.when(k == 0)
        def _():
            acc[...] = jnp.zeros_like(acc)

        acc[...] += lax.dot_general(a_ref[...], b_ref[...], (((0,), (0,)), ((), ())),
                                    preferred_element_type=F32)

        @pl.when(k == nk - 1)
        def _():
            o_ref[...] = acc[...].astype(o_ref.dtype)

    return pl.pallas_call(
        body, name=name, grid=grid,
        in_specs=[pl.BlockSpec((tk, tm), lambda i, j, k: (k, i)),
                  _lead_spec(b, lead, (tk, tn), lambda i, j, k: (k, j))],
        out_specs=pl.BlockSpec((None, tm, tn), lambda i, j, k: (j // nb, i, j % nb)),
        out_shape=jax.ShapeDtypeStruct((slots, K, Ns), out_dtype),
        scratch_shapes=[pltpu.VMEM((tm, tn), F32)],
        compiler_params=_cparams(("parallel", "parallel", "arbitrary")),
    )(a, b)


def rms_fwd(h, g, *, tr=256, name):
    T, D = h.shape
    tr = min(tr, T)

    def body(h_ref, g_ref, o_ref):
        x = h_ref[...]
        r = lax.rsqrt(jnp.mean(x * x, axis=-1, keepdims=True) + EPS)
        o_ref[...] = (x * r * g_ref[...]).astype(o_ref.dtype)

    return pl.pallas_call(
        body, name=name, grid=(T // tr,),
        in_specs=[pl.BlockSpec((tr, D), lambda i: (i, 0)), pl.BlockSpec((1, D), lambda i: (0, 0))],
        out_specs=pl.BlockSpec((tr, D), lambda i: (i, 0)),
        out_shape=jax.ShapeDtypeStruct((T, D), BF16),
        compiler_params=_cparams(("parallel",)),
    )(h, g)


def rms_bwd(h, g, dxn, dres, *, tr=256, name):
    T, D = h.shape
    tr = min(tr, T)

    def body(h_ref, g_ref, d_ref, r_ref, dh_ref, dhb_ref, dg_ref):
        i = pl.program_id(0)
        x = h_ref[...]
        r = lax.rsqrt(jnp.mean(x * x, axis=-1, keepdims=True) + EPS)
        xh = x * r
        d = d_ref[...].astype(F32)

        @pl.when(i == 0)
        def _():
            dg_ref[...] = jnp.zeros_like(dg_ref)

        dg_ref[...] += jnp.sum(d * xh, axis=0, keepdims=True)
        dxh = d * g_ref[...]
        dh = r * (dxh - xh * jnp.mean(dxh * xh, axis=-1, keepdims=True)) + r_ref[...]
        dh_ref[...] = dh
        dhb_ref[...] = dh.astype(BF16)

    row = pl.BlockSpec((tr, D), lambda i: (i, 0))
    vec = pl.BlockSpec((1, D), lambda i: (0, 0))
    return pl.pallas_call(
        body, name=name, grid=(T // tr,),
        in_specs=[row, vec, row, row],
        out_specs=[row, row, vec],
        out_shape=[jax.ShapeDtypeStruct((T, D), F32), jax.ShapeDtypeStruct((T, D), BF16),
                   jax.ShapeDtypeStruct((1, D), F32)],
        compiler_params=_cparams(("arbitrary",)),
    )(h, g, dxn, dres)


def loss_head(h, g, target, *, tr=256, name):
    T, D = h.shape
    tr = min(tr, T)

    def body(h_ref, g_ref, t_ref, loss_ref, dh_ref, dhb_ref, dg_ref):
        i = pl.program_id(0)
        x = h_ref[...]
        r = lax.rsqrt(jnp.mean(x * x, axis=-1, keepdims=True) + EPS)
        xh = x * r
        gg = g_ref[...]
        err = xh * gg - t_ref[...]

        @pl.when(i == 0)
        def _():
            dg_ref[...] = jnp.zeros_like(dg_ref)
            loss_ref[...] = jnp.zeros_like(loss_ref)

        loss_ref[...] += 0.5 * jnp.sum(jnp.mean(err * err, axis=-1, keepdims=True))
        dy = err * (1.0 / D)
        dg_ref[...] += jnp.sum(dy * xh, axis=0, keepdims=True)
        dxh = dy * gg
        dh = r * (dxh - xh * jnp.mean(dxh * xh, axis=-1, keepdims=True))
        dh_ref[...] = dh
        dhb_ref[...] = dh.astype(BF16)

    row = pl.BlockSpec((tr, D), lambda i: (i, 0))
    vec = pl.BlockSpec((1, D), lambda i: (0, 0))
    return pl.pallas_call(
        body, name=name, grid=(T // tr,),
        in_specs=[row, vec, row],
        out_specs=[pl.BlockSpec((8, LANES), lambda i: (0, 0)), row, row, vec],
        out_shape=[jax.ShapeDtypeStruct((8, LANES), F32), jax.ShapeDtypeStruct((T, D), F32),
                   jax.ShapeDtypeStruct((T, D), BF16), jax.ShapeDtypeStruct((1, D), F32)],
        compiler_params=_cparams(("arbitrary",)),
    )(h, g, target)


CONV_HALO = 32


def _glu_ext(prev_ref, main_ref, hs_ref, i, dc, tt):
    up = prev_ref[...].astype(F32)
    hp = up[:, :dc] * jax.nn.sigmoid(up[:, dc:])
    hs_ref[pl.ds(0, CONV_HALO), :] = jnp.where(i > 0, hp, 0.0)
    um = main_ref[...].astype(F32)
    sig = jax.nn.sigmoid(um[:, dc:])
    hs_ref[pl.ds(CONV_HALO, tt), :] = um[:, :dc] * sig
    return um[:, :dc], sig


def conv_fwd(proj, conv_w, conv_b, ln_g, ln_b, *, tt=256, name):
    T = proj.shape[0]
    W, DC = conv_w.shape
    tt = min(tt, T)
    hb = tt // CONV_HALO

    def body(prev_ref, main_ref, w_ref, b_ref, g_ref, bb_ref, s_ref, c_ref, hs_ref):
        i = pl.program_id(0)
        _glu_ext(prev_ref, main_ref, hs_ref, i, DC, tt)
        c = jnp.zeros((tt, DC), F32) + b_ref[...]
        for k in range(W):
            c = c + w_ref[pl.ds(k, 1), :] * hs_ref[pl.ds(CONV_HALO - (W - 1) + k, tt), :]
        c_ref[...] = c
        mu = jnp.mean(c, axis=-1, keepdims=True)
        xc = c - mu
        var = jnp.mean(xc * xc, axis=-1, keepdims=True)
        y = xc * lax.rsqrt(var + EPS) * g_ref[...] + bb_ref[...]
        s_ref[...] = (y * jax.nn.sigmoid(y)).astype(s_ref.dtype)

    vec = pl.BlockSpec((1, DC), lambda i: (0, 0))
    return pl.pallas_call(
        body, name=name, grid=(T // tt,),
        in_specs=[pl.BlockSpec((CONV_HALO, 2 * DC), lambda i: (jnp.maximum(i * hb - 1, 0), 0)),
                  pl.BlockSpec((tt, 2 * DC), lambda i: (i, 0)),
                  pl.BlockSpec((W, DC), lambda i: (0, 0)), vec, vec, vec],
        out_specs=[pl.BlockSpec((tt, DC), lambda i: (i, 0)), pl.BlockSpec((tt, DC), lambda i: (i, 0))],
        out_shape=[jax.ShapeDtypeStruct((T, DC), BF16), jax.ShapeDtypeStruct((T, DC), F32)],
        scratch_shapes=[pltpu.VMEM((tt + CONV_HALO, DC), F32)],
        compiler_params=_cparams(("parallel",)),
    )(proj, proj, conv_w, conv_b, ln_g, ln_b)


def conv_bwd_ln(ds, c, ln_g, ln_b, *, tt=256, name):
    T, DC = c.shape
    tt = min(tt, T)

    def body(ds_ref, c_ref, g_ref, bb_ref, dc_ref, dg_ref, db_ref):
        i = pl.program_id(0)
        c = c_ref[...]
        mu = jnp.mean(c, axis=-1, keepdims=True)
        xc = c - mu
        var = jnp.mean(xc * xc, axis=-1, keepdims=True)
        rstd = lax.rsqrt(var + EPS)
        xh = xc * rstd
        y = xh * g_ref[...] + bb_ref[...]
        sg = jax.nn.sigmoid(y)
        dy = ds_ref[...].astype(F32) * (sg * (1.0 + y * (1.0 - sg)))

        @pl.when(i == 0)
        def _():
            dg_ref[...] = jnp.zeros_like(dg_ref)
            db_ref[...] = jnp.zeros_like(db_ref)

        db_ref[...] += jnp.sum(dy, axis=0, keepdims=True)
        dg_ref[...] += jnp.sum(dy * xh, axis=0, keepdims=True)
        dxh = dy * g_ref[...]
        dc_ref[...] = rstd * (dxh - jnp.mean(dxh, axis=-1, keepdims=True)
                              - xh * jnp.mean(dxh * xh, axis=-1, keepdims=True))

    row = pl.BlockSpec((tt, DC), lambda i: (i, 0))
    vec = pl.BlockSpec((1, DC), lambda i: (0, 0))
    return pl.pallas_call(
        body, name=name, grid=(T // tt,),
        in_specs=[row, row, vec, vec], out_specs=[row, vec, vec],
        out_shape=[jax.ShapeDtypeStruct((T, DC), F32), jax.ShapeDtypeStruct((1, DC), F32),
                   jax.ShapeDtypeStruct((1, DC), F32)],
        compiler_params=_cparams(("arbitrary",)),
    )(ds, c, ln_g, ln_b)


def conv_bwd_taps(dc, proj, conv_w, dproj, *, tt=256, name):
    T, DC = dc.shape
    W = conv_w.shape[0]
    tt = min(tt, T)
    hb = tt // CONV_HALO
    n_t = T // tt
    last_halo = T // CONV_HALO - 1

    def body(dc_ref, dcn_ref, prev_ref, main_ref, w_ref, _, dp_ref, dw_ref, db_ref, hs_ref, ds_ref):
        i = pl.program_id(0)
        a, sig = _glu_ext(prev_ref, main_ref, hs_ref, i, DC, tt)
        d = dc_ref[...]
        ds_ref[pl.ds(0, tt), :] = d
        ds_ref[pl.ds(tt, CONV_HALO), :] = jnp.where(i < n_t - 1, dcn_ref[...], 0.0)

        @pl.when(i == 0)
        def _():
            dw_ref[...] = jnp.zeros_like(dw_ref)
            db_ref[...] = jnp.zeros_like(db_ref)

        db_ref[...] += jnp.sum(d, axis=0, keepdims=True)
        dh = jnp.zeros((tt, DC), F32)
        for k in range(W):
            dh = dh + w_ref[pl.ds(k, 1), :] * ds_ref[pl.ds(W - 1 - k, tt), :]
            dw_ref[pl.ds(k, 1), :] += jnp.sum(d * hs_ref[pl.ds(CONV_HALO - (W - 1) + k, tt), :],
                                              axis=0, keepdims=True)
        da = dh * sig
        dgt = dh * a * sig * (1.0 - sig)
        dp_ref[...] = jnp.concatenate([da, dgt], axis=1).astype(dp_ref.dtype)

    return pl.pallas_call(
        body, name=name, grid=(n_t,),
        in_specs=[pl.BlockSpec((tt, DC), lambda i: (i, 0)),
                  pl.BlockSpec((CONV_HALO, DC), lambda i: (jnp.minimum((i + 1) * hb, last_halo), 0)),
                  pl.BlockSpec((CONV_HALO, 2 * DC), lambda i: (jnp.maximum(i * hb - 1, 0), 0)),
                  pl.BlockSpec((tt, 2 * DC), lambda i: (i, 0)),
                  pl.BlockSpec((W, DC), lambda i: (0, 0)),
                  pl.BlockSpec(memory_space=pl.ANY)],
        out_specs=[pl.BlockSpec((tt, 2 * DC), lambda i: (i, 0)),
                   pl.BlockSpec((W, DC), lambda i: (0, 0)), pl.BlockSpec((1, DC), lambda i: (0, 0))],
        out_shape=[jax.ShapeDtypeStruct(dproj.shape, dproj.dtype), jax.ShapeDtypeStruct((W, DC), F32),
                   jax.ShapeDtypeStruct((1, DC), F32)],
        scratch_shapes=[pltpu.VMEM((tt + CONV_HALO, DC), F32), pltpu.VMEM((tt + CONV_HALO, DC), F32)],
        input_output_aliases={5: 0},
        compiler_params=_cparams(("arbitrary",)),
    )(dc, dc, proj, proj, conv_w, dproj)


ATT_TQ = 256
ATT_NB = 1 + (LEFT_CHUNKS * CHUNK) // ATT_TQ
ATT_WIN = ATT_NB * ATT_TQ
ATT_PERIOD = 1024
REL_PAD = 640


def _rel_onehot():
    m = lax.broadcasted_iota(jnp.int32, (REL_PAD, ATT_PERIOD), 1)
    r = lax.broadcasted_iota(jnp.int32, (REL_PAD, ATT_PERIOD), 0)
    qk = jnp.where(m < ATT_PERIOD - ATT_TQ, -m, ATT_PERIOD - m)
    idx = jnp.clip(LEFT_CHUNKS * CHUNK + qk, -MAX_REL, MAX_REL) + MAX_REL
    return (idx == r).astype(F32)


def _band_valid():
    q = lax.broadcasted_iota(jnp.int32, (ATT_TQ, ATT_WIN), 0)
    k = lax.broadcasted_iota(jnp.int32, (ATT_TQ, ATT_WIN), 1)
    j = k // CHUNK - q // CHUNK
    return (j >= 0) & (j <= LEFT_CHUNKS)


def attn_bias(rel_pad, *, name):
    H = rel_pad.shape[0]

    def body(rb_ref, o_ref):
        g = jnp.dot(rb_ref[...], _rel_onehot(), preferred_element_type=F32, precision=lax.Precision.HIGHEST)
        valid = _band_valid()
        for h in range(H):
            row = jnp.broadcast_to(g[h:h + 1, :], (ATT_TQ, ATT_PERIOD))
            t = pltpu.roll(row, 0, 1, stride=1, stride_axis=0)
            o_ref[h] = jnp.where(valid, t[:, :ATT_WIN], NEG_INF)

    return pl.pallas_call(
        body, name=name,
        in_specs=[pl.BlockSpec(memory_space=pltpu.VMEM)], out_specs=pl.BlockSpec(memory_space=pltpu.VMEM),
        out_shape=jax.ShapeDtypeStruct((H, ATT_TQ, ATT_WIN), F32),
        compiler_params=_cparams(),
    )(rel_pad)


def attn_bias_bwd(dbias, *, name):
    H = dbias.shape[0]
    band = (LEFT_CHUNKS + 1) * CHUNK

    def body(d_ref, o_ref, acc_ref):
        r = lax.broadcasted_iota(jnp.int32, (CHUNK, CHUNK), 0)
        c = lax.broadcasted_iota(jnp.int32, (CHUNK, CHUNK), 1)
        rev = (r + c == CHUNK - 1).astype(F32)
        pad = jnp.zeros((CHUNK, ATT_PERIOD - ATT_WIN), F32)
        for h in range(H):
            tab = d_ref[h, pl.ds(0, CHUNK), :]
            for cq in range(1, ATT_TQ // CHUNK):
                tab = tab + pltpu.roll(d_ref[h, pl.ds(cq * CHUNK, CHUNK), :], ATT_WIN - cq * CHUNK, 1)
            t = jnp.dot(rev, tab, preferred_element_type=F32, precision=lax.Precision.HIGHEST)
            u = pltpu.roll(jnp.concatenate([t, pad], axis=1), 0, 1, stride=1, stride_axis=0)
            acc_ref[pl.ds(h, 1), :] = jnp.sum(u, axis=0, keepdims=True)
        j = lax.broadcasted_iota(jnp.int32, (REL_PAD, ATT_PERIOD), 1)
        rr = lax.broadcasted_iota(jnp.int32, (REL_PAD, ATT_PERIOD), 0)
        idx = jnp.clip(LEFT_CHUNKS * CHUNK + CHUNK - 1 - j, -MAX_REL, MAX_REL) + MAX_REL
        onehot = ((idx == rr) & (j < band + CHUNK - 1)).astype(F32)
        o_ref[...] = lax.dot_general(acc_ref[...], onehot, (((1,), (1,)), ((), ())),
                                     preferred_element_type=F32, precision=lax.Precision.HIGHEST)

    return pl.pallas_call(
        body, name=name,
        in_specs=[pl.BlockSpec(memory_space=pltpu.VMEM)], out_specs=pl.BlockSpec(memory_space=pltpu.VMEM),
        out_shape=jax.ShapeDtypeStruct((H, REL_PAD), F32),
        scratch_shapes=[pltpu.VMEM((H, ATT_PERIOD), F32)],
        compiler_params=_cparams(),
    )(dbias)


def _attn_specs(T, qcol, dattn):
    dn = dattn // LANES

    def kv(col0, back):
        return pl.BlockSpec((ATT_TQ, LANES), lambda hp, i: (jnp.maximum(i - back, 0), col0 // LANES + hp))

    q = pl.BlockSpec((ATT_TQ, LANES), lambda hp, i: (i, qcol // LANES + hp))
    ks = [kv(qcol + dattn, ATT_NB - 1 - b) for b in range(ATT_NB)]
    vs = [kv(qcol + 2 * dattn, ATT_NB - 1 - b) for b in range(ATT_NB)]
    return q, ks, vs


def _attn_scores(q, kw, bias, i):
    s = lax.dot_general(q, kw, (((1,), (1,)), ((), ())), preferred_element_type=F32)
    s = s * (ATTN_HEAD_DIM ** -0.5) + bias
    col = lax.broadcasted_iota(jnp.int32, s.shape, 1)
    s = jnp.where(col // ATT_TQ + i >= ATT_NB - 1, s, NEG_INF)
    s = s - jnp.max(s, axis=-1, keepdims=True)
    p = jnp.exp(s)
    return p / jnp.sum(p, axis=-1, keepdims=True)


def attn_fwd(proj, bias, *, qcol, dattn, name):
    T = proj.shape[0]
    HP = dattn // LANES
    q_spec, k_specs, v_specs = _attn_specs(T, qcol, dattn)

    def body(*refs):
        q_ref = refs[0]
        k_refs = refs[1:1 + ATT_NB]
        v_refs = refs[1 + ATT_NB:1 + 2 * ATT_NB]
        b_ref, o_ref = refs[1 + 2 * ATT_NB:]
        i = pl.program_id(1)
        lane = lax.broadcasted_iota(jnp.int32, (ATT_TQ, LANES), 1)
        kw = jnp.concatenate([r[...] for r in k_refs], axis=0)
        vw = jnp.concatenate([r[...] for r in v_refs], axis=0)
        q = q_ref[...]
        out = jnp.zeros((ATT_TQ, LANES), F32)
        for hh in range(2):
            mine = (lane // ATTN_HEAD_DIM) == hh
            p = _attn_scores(jnp.where(mine, q, jnp.zeros_like(q)), kw, b_ref[hh], i)
            o = jnp.dot(p.astype(BF16), vw, preferred_element_type=F32)
            out = jnp.where(mine, o, out)
        o_ref[...] = out.astype(o_ref.dtype)

    return pl.pallas_call(
        body, name=name, grid=(HP, T // ATT_TQ),
        in_specs=[q_spec, *k_specs, *v_specs,
                  pl.BlockSpec((2, ATT_TQ, ATT_WIN), lambda hp, i: (hp, 0, 0))],
        out_specs=pl.BlockSpec((ATT_TQ, LANES), lambda hp, i: (i, hp)),
        out_shape=jax.ShapeDtypeStruct((T, dattn), BF16),
        compiler_params=_cparams(("parallel", "parallel")),
    )(*([proj] * (1 + 2 * ATT_NB)), bias)


def attn_bwd(proj, bias, dout, dproj, *, qcol, dattn, name):
    T = proj.shape[0]
    HP = dattn // LANES
    nq = T // ATT_TQ
    q_spec, k_specs, v_specs = _attn_specs(T, qcol, dattn)

    def body(*refs):
        q_ref = refs[0]
        k_refs = refs[1:1 + ATT_NB]
        v_refs = refs[1 + ATT_NB:1 + 2 * ATT_NB]
        b_ref, do_ref, _, dq_ref, dk_ref, dv_ref, db_ref = refs[1 + 2 * ATT_NB:]
        i = pl.program_id(1)
        lane = lax.broadcasted_iota(jnp.int32, (ATT_TQ, LANES), 1)
        kw = jnp.concatenate([r[...] for r in k_refs], axis=0)
        vw = jnp.concatenate([r[...] for r in v_refs], axis=0)
        q = q_ref[...]
        do = do_ref[...]
        dq = jnp.zeros((ATT_TQ, LANES), F32)
        dk = jnp.zeros((ATT_WIN, LANES), F32)
        dv = jnp.zeros((ATT_WIN, LANES), F32)

        @pl.when(i == 0)
        def _():
            db_ref[...] = jnp.zeros_like(db_ref)

        for hh in range(2):
            mine = (lane // ATTN_HEAD_DIM) == hh
            qh = jnp.where(mine, q, jnp.zeros_like(q))
            doh = jnp.where(mine, do, jnp.zeros_like(do))
            p = _attn_scores(qh, kw, b_ref[hh], i)
            dp = lax.dot_general(doh, vw, (((1,), (1,)), ((), ())), preferred_element_type=F32)
            dv = dv + lax.dot_general(p.astype(BF16), doh, (((0,), (0,)), ((), ())), preferred_element_type=F32)
            ds = p * (dp - jnp.sum(dp * p, axis=-1, keepdims=True))
            db_ref[hh] += ds
            dsb = (ds * (ATTN_HEAD_DIM ** -0.5)).astype(BF16)
            dq = jnp.where(mine, jnp.dot(dsb, kw, preferred_element_type=F32), dq)
            dk = dk + lax.dot_general(dsb, qh, (((0,), (0,)), ((), ())), preferred_element_type=F32)
        dq_ref[...] = dq.astype(dq_ref.dtype)
        dk_ref[...] = dk
        dv_ref[...] = dv

    win = pl.BlockSpec((None, ATT_WIN, LANES), lambda hp, i: (i, 0, hp))
    return pl.pallas_call(
        body, name=name, grid=(HP, nq),
        in_specs=[q_spec, *k_specs, *v_specs,
                  pl.BlockSpec((2, ATT_TQ, ATT_WIN), lambda hp, i: (hp, 0, 0)),
                  pl.BlockSpec((ATT_TQ, LANES), lambda hp, i: (i, hp)),
                  pl.BlockSpec(memory_space=pl.ANY)],
        out_specs=[pl.BlockSpec((ATT_TQ, LANES), lambda hp, i: (i, qcol // LANES + hp)), win, win,
                   pl.BlockSpec((2, ATT_TQ, ATT_WIN), lambda hp, i: (hp, 0, 0))],
        out_shape=[jax.ShapeDtypeStruct(dproj.shape, dproj.dtype),
                   jax.ShapeDtypeStruct((nq, ATT_WIN, dattn), F32), jax.ShapeDtypeStruct((nq, ATT_WIN, dattn), F32),
                   jax.ShapeDtypeStruct((2 * HP, ATT_TQ, ATT_WIN), F32)],
        input_output_aliases={3 + 2 * ATT_NB: 0},
        compiler_params=_cparams(("arbitrary", "arbitrary")),
    )(*([proj] * (1 + 2 * ATT_NB)), bias, dout, dproj)


def window_sum(win, dproj, *, col, name):
    nq, _, C = win.shape
    assert col % C == 0

    def body(*refs):
        w_refs = refs[:ATT_NB]
        o_ref = refs[ATT_NB + 1]
        j = pl.program_id(0)
        acc = w_refs[0][...]
        for b in range(1, ATT_NB):
            acc = acc + jnp.where(j + b < nq, w_refs[b][...], 0.0)
        o_ref[...] = acc.astype(o_ref.dtype)

    def part(b):
        return pl.BlockSpec((None, ATT_TQ, C), lambda j: (jnp.minimum(j + b, nq - 1), ATT_NB - 1 - b, 0))

    return pl.pallas_call(
        body, name=name, grid=(nq,),
        in_specs=[part(b) for b in range(ATT_NB)] + [pl.BlockSpec(memory_space=pl.ANY)],
        out_specs=pl.BlockSpec((ATT_TQ, C), lambda j: (j, col // C)),
        out_shape=jax.ShapeDtypeStruct(dproj.shape, dproj.dtype),
        input_output_aliases={ATT_NB: 0},
        compiler_params=_cparams(("arbitrary",)),
    )(*([win] * ATT_NB), dproj)


def _mem_probs(q, km, scale):
    s = lax.dot_general(q, km, (((1,), (1,)), ((), ())), preferred_element_type=F32) * scale
    s = s - jnp.max(s, axis=-1, keepdims=True)
    p = jnp.exp(s)
    return p / jnp.sum(p, axis=-1, keepdims=True)


def memattn_fwd(proj, kv, *, qcol, tq=512, name):
    T = proj.shape[0]
    M, dm2 = kv.shape
    DM = dm2 // 2
    hd = DM // N_MEM_HEADS
    tq = min(tq, T)

    def body(q_ref, kv_ref, o_ref):
        for h in range(N_MEM_HEADS):
            sl = pl.ds(h * hd, hd)
            p = _mem_probs(q_ref[:, sl], kv_ref[:, sl], hd ** -0.5)
            o = jnp.dot(p.astype(BF16), kv_ref[:, pl.ds(DM + h * hd, hd)], preferred_element_type=F32)
            o_ref[:, sl] = o.astype(o_ref.dtype)

    return pl.pallas_call(
        body, name=name, grid=(T // tq,),
        in_specs=[pl.BlockSpec((tq, DM), lambda i: (i, qcol // DM)),
                  pl.BlockSpec((M, 2 * DM), lambda i: (0, 0))],
        out_specs=pl.BlockSpec((tq, DM), lambda i: (i, 0)),
        out_shape=jax.ShapeDtypeStruct((T, DM), BF16),
        compiler_params=_cparams(("parallel",)),
    )(proj, kv)


def memattn_bwd(proj, kv, dout, dproj, *, qcol, tq=512, name):
    T = proj.shape[0]
    M, dm2 = kv.shape
    DM = dm2 // 2
    hd = DM // N_MEM_HEADS
    tq = min(tq, T)

    def body(q_ref, kv_ref, do_ref, _, dq_ref, dkv_ref):
        i = pl.program_id(0)

        @pl.when(i == 0)
        def _():
            dkv_ref[...] = jnp.zeros_like(dkv_ref)

        for h in range(N_MEM_HEADS):
            sl = pl.ds(h * hd, hd)
            vsl = pl.ds(DM + h * hd, hd)
            q = q_ref[:, sl]
            do = do_ref[:, sl]
            p = _mem_probs(q, kv_ref[:, sl], hd ** -0.5)
            dp = lax.dot_general(do, kv_ref[:, vsl], (((1,), (1,)), ((), ())), preferred_element_type=F32)
            dkv_ref[:, vsl] += lax.dot_general(p.astype(BF16), do, (((0,), (0,)), ((), ())),
                                               preferred_element_type=F32)
            ds = p * (dp - jnp.sum(dp * p, axis=-1, keepdims=True))
            dsb = (ds * (hd ** -0.5)).astype(BF16)
            dq_ref[:, sl] = jnp.dot(dsb, kv_ref[:, sl], preferred_element_type=F32).astype(dq_ref.dtype)
            dkv_ref[:, sl] += lax.dot_general(dsb, q, (((0,), (0,)), ((), ())), preferred_element_type=F32)

    return pl.pallas_call(
        body, name=name, grid=(T // tq,),
        in_specs=[pl.BlockSpec((tq, DM), lambda i: (i, qcol // DM)),
                  pl.BlockSpec((M, 2 * DM), lambda i: (0, 0)),
                  pl.BlockSpec((tq, DM), lambda i: (i, 0)),
                  pl.BlockSpec(memory_space=pl.ANY)],
        out_specs=[pl.BlockSpec((tq, DM), lambda i: (i, qcol // DM)),
                   pl.BlockSpec((M, 2 * DM), lambda i: (0, 0))],
        out_shape=[jax.ShapeDtypeStruct(dproj.shape, dproj.dtype), jax.ShapeDtypeStruct((M, 2 * DM), F32)],
        input_output_aliases={3: 0},
        compiler_params=_cparams(("arbitrary",)),
    )(proj, kv, dout, dproj)


def merge_fwd(proj, gate_b, ys, *, gcol, tr=512, tc=512, name):
    T = proj.shape[0]
    D = ys[0].shape[1]
    tr, tc = min(tr, T), _tile(D, tc)
    nd = D // tc

    def body(g0, g1, g2, b0, b1, b2, y0, y1, y2, o_ref):
        acc = jnp.zeros((tr, tc), F32)
        for g, b, y in ((g0, b0, y0), (g1, b1, y1), (g2, b2, y2)):
            acc = acc + jax.nn.sigmoid(g[...].astype(F32) + b[...]) * y[...].astype(F32)
        o_ref[...] = acc.astype(o_ref.dtype)

    def gate(b):
        return pl.BlockSpec((tr, tc), lambda i, j: (i, gcol // tc + b * nd + j))

    def bias(b):
        return pl.BlockSpec((1, tc), lambda i, j: (0, b * nd + j))

    blk = pl.BlockSpec((tr, tc), lambda i, j: (i, j))
    return pl.pallas_call(
        body, name=name, grid=(T // tr, nd),
        in_specs=[gate(0), gate(1), gate(2), bias(0), bias(1), bias(2), blk, blk, blk],
        out_specs=blk,
        out_shape=jax.ShapeDtypeStruct((T, D), BF16),
        compiler_params=_cparams(("parallel", "parallel")),
    )(proj, proj, proj, gate_b, gate_b, gate_b, *ys)


def merge_bwd(dmerged, proj, gate_b, ys, *, gcol, tr=512, tc=512, name):
    T, D_IN = proj.shape
    D = ys[0].shape[1]
    tr, tc = min(tr, T), _tile(D, tc)
    nd = D // tc

    def body(dm_ref, g_ref, b_ref, y0, y1, y2, dp_ref, dy_ref, db_ref):
        br = pl.program_id(0)
        i = pl.program_id(2)
        y = jnp.where(br == 0, y0[...], jnp.where(br == 1, y1[...], y2[...])).astype(F32)
        dm = dm_ref[...].astype(F32)
        sg = jax.nn.sigmoid(g_ref[...].astype(F32) + b_ref[...])
        dy_ref[...] = (dm * sg).astype(dy_ref.dtype)
        dg = dm * y * sg * (1.0 - sg)
        dp_ref[...] = dg.astype(dp_ref.dtype)

        @pl.when(i == 0)
        def _():
            db_ref[...] = jnp.zeros_like(db_ref)

        db_ref[...] += jnp.sum(dg, axis=0, keepdims=True)

    def ysp(b):
        return pl.BlockSpec((tr, tc), lambda br, j, i: (jnp.where(br == b, i, 0), jnp.where(br == b, j, 0)))

    return pl.pallas_call(
        body, name=name, grid=(3, nd, T // tr),
        in_specs=[pl.BlockSpec((tr, tc), lambda br, j, i: (i, j)),
                  pl.BlockSpec((tr, tc), lambda br, j, i: (i, gcol // tc + br * nd + j)),
                  pl.BlockSpec((1, tc), lambda br, j, i: (0, br * nd + j)),
                  ysp(0), ysp(1), ysp(2)],
        out_specs=[pl.BlockSpec((tr, tc), lambda br, j, i: (i, gcol // tc + br * nd + j)),
                   pl.BlockSpec((None, tr, tc), lambda br, j, i: (br, i, j)),
                   pl.BlockSpec((1, tc), lambda br, j, i: (0, br * nd + j))],
        out_shape=[jax.ShapeDtypeStruct((T, D_IN), BF16), jax.ShapeDtypeStruct((3, T, D), BF16),
                   jax.ShapeDtypeStruct((1, 3 * D), F32)],
        compiler_params=_cparams(("arbitrary", "arbitrary", "arbitrary")),
    )(dmerged, proj, gate_b, *ys)


FFN_HALO = 16


def _ffn_conv(prev_ref, main_ref, w_ref, b_ref, xs_ref, i, tt):
    xs_ref[pl.ds(0, FFN_HALO), :] = jnp.where(i > 0, prev_ref[...].astype(F32), 0.0)
    x = main_ref[...].astype(F32)
    xs_ref[pl.ds(FFN_HALO, tt), :] = x
    return (w_ref[pl.ds(0, 1), :] * xs_ref[pl.ds(FFN_HALO - 2, tt), :]
            + w_ref[pl.ds(1, 1), :] * xs_ref[pl.ds(FFN_HALO - 1, tt), :]
            + w_ref[pl.ds(2, 1), :] * x + b_ref[...])


def _ffn_specs(tt, tc, nv, order):
    hb = tt // FFN_HALO

    def mk(shape, f):
        return pl.BlockSpec(shape, (lambda i, j: f(i, j)) if order == "ij" else (lambda j, i: f(i, j)))

    out = []
    for off in (0, nv):
        out += [mk((FFN_HALO, tc), lambda i, j, off=off: (jnp.maximum(i * hb - 1, 0), off + j)),
                mk((tt, tc), lambda i, j, off=off: (i, off + j))]
    out += [mk((FFN_CONV_WIDTH, tc), lambda i, j, off=off: (0, off + j)) for off in (0, nv)]
    out += [mk((1, tc), lambda i, j, off=off: (0, off + j)) for off in (0, nv)]
    return out


def ffn_act_fwd(up0, w, b, *, tt=512, tc=512, name):
    T, dff2 = up0.shape
    DFF = dff2 // 2
    tt, tc = min(tt, T), _tile(DFF, tc)
    nv = DFF // tc

    def body(vp, vm, gp, gm, wv, wg, bv, bg, o_ref, xs_ref):
        i = pl.program_id(0)
        val = _ffn_conv(vp, vm, wv, bv, xs_ref, i, tt)
        gt = _ffn_conv(gp, gm, wg, bg, xs_ref, i, tt)
        o_ref[...] = (gt * jax.nn.sigmoid(gt) * val).astype(o_ref.dtype)

    return pl.pallas_call(
        body, name=name, grid=(T // tt, nv),
        in_specs=_ffn_specs(tt, tc, nv, "ij"),
        out_specs=pl.BlockSpec((tt, tc), lambda i, j: (i, j)),
        out_shape=jax.ShapeDtypeStruct((T, DFF), BF16),
        scratch_shapes=[pltpu.VMEM((FFN_HALO + tt, tc), F32)],
        compiler_params=_cparams(("parallel", "parallel")),
    )(up0, up0, up0, up0, w, w, b, b)


def ffn_act_bwd(dact, up0, w, b, *, tt=512, tc=512, name):
    T, dff2 = up0.shape
    DFF = dff2 // 2
    tt, tc = min(tt, T), _tile(DFF, tc)
    nv = DFF // tc

    def body(da_ref, vp, vm, gp, gm, wv, wg, bv, bg, o_ref, xs_ref):
        i = pl.program_id(0)
        val = _ffn_conv(vp, vm, wv, bv, xs_ref, i, tt)
        gt = _ffn_conv(gp, gm, wg, bg, xs_ref, i, tt)
        da = da_ref[...].astype(F32)
        sg = jax.nn.sigmoid(gt)
        o_ref[0] = (da * gt * sg).astype(o_ref.dtype)
        o_ref[1] = (da * val * (sg * (1.0 + gt * (1.0 - sg)))).astype(o_ref.dtype)

    return pl.pallas_call(
        body, name=name, grid=(T // tt, nv),
        in_specs=[pl.BlockSpec((tt, tc), lambda i, j: (i, j))] + _ffn_specs(tt, tc, nv, "ij"),
        out_specs=pl.BlockSpec((2, tt, tc), lambda i, j: (0, i, j)),
        out_shape=jax.ShapeDtypeStruct((2, T, DFF), BF16),
        scratch_shapes=[pltpu.VMEM((FFN_HALO + tt, tc), F32)],
        compiler_params=_cparams(("parallel", "parallel")),
    )(dact, up0, up0, up0, up0, w, w, b, b)


def ffn_conv_bwd(dup, up0, w, *, tt=512, tc=512, name):
    T, dff2 = up0.shape
    DFF = dff2 // 2
    tt, tc = min(tt, T), _tile(DFF, tc)
    nv = DFF // tc
    hb = tt // FFN_HALO
    n_t = T // tt
    last_halo = T // FFN_HALO - 1

    def body(d_ref, dn_ref, xp_ref, xm_ref, w_ref, o_ref, dw_ref, db_ref, ds_ref, xs_ref):
        i = pl.program_id(1)
        d = d_ref[...].astype(F32)
        ds_ref[pl.ds(0, tt), :] = d
        ds_ref[pl.ds(tt, FFN_HALO), :] = jnp.where(i < n_t - 1, dn_ref[...].astype(F32), 0.0)
        xs_ref[pl.ds(0, FFN_HALO), :] = jnp.where(i > 0, xp_ref[...].astype(F32), 0.0)
        x = xm_ref[...].astype(F32)
        xs_ref[pl.ds(FFN_HALO, tt), :] = x
        o_ref[...] = (w_ref[pl.ds(2, 1), :] * d + w_ref[pl.ds(1, 1), :] * ds_ref[pl.ds(1, tt), :]
                      + w_ref[pl.ds(0, 1), :] * ds_ref[pl.ds(2, tt), :]).astype(o_ref.dtype)

        @pl.when(i == 0)
        def _():
            dw_ref[...] = jnp.zeros_like(dw_ref)
            db_ref[...] = jnp.zeros_like(db_ref)

        db_ref[...] += jnp.sum(d, axis=0, keepdims=True)
        dw_ref[pl.ds(0, 1), :] += jnp.sum(d * xs_ref[pl.ds(FFN_HALO - 2, tt), :], axis=0, keepdims=True)
        dw_ref[pl.ds(1, 1), :] += jnp.sum(d * xs_ref[pl.ds(FFN_HALO - 1, tt), :], axis=0, keepdims=True)
        dw_ref[pl.ds(2, 1), :] += jnp.sum(d * x, axis=0, keepdims=True)

    return pl.pallas_call(
        body, name=name, grid=(2 * nv, n_t),
        in_specs=[pl.BlockSpec((None, tt, tc), lambda j, i: (j // nv, i, j % nv)),
                  pl.BlockSpec((None, FFN_HALO, tc),
                               lambda j, i: (j // nv, jnp.minimum((i + 1) * hb, last_halo), j % nv)),
                  pl.BlockSpec((FFN_HALO, tc), lambda j, i: (jnp.maximum(i * hb - 1, 0), j)),
                  pl.BlockSpec((tt, tc), lambda j, i: (i, j)),
                  pl.BlockSpec((FFN_CONV_WIDTH, tc), lambda j, i: (0, j))],
        out_specs=[pl.BlockSpec((tt, tc), lambda j, i: (i, j)),
                   pl.BlockSpec((FFN_CONV_WIDTH, tc), lambda j, i: (0, j)),
                   pl.BlockSpec((1, tc), lambda j, i: (0, j))],
        out_shape=[jax.ShapeDtypeStruct((T, dff2), BF16), jax.ShapeDtypeStruct((FFN_CONV_WIDTH, dff2), F32),
                   jax.ShapeDtypeStruct((1, dff2), F32)],
        scratch_shapes=[pltpu.VMEM((tt + FFN_HALO, tc), F32), pltpu.VMEM((FFN_HALO + tt, tc), F32)],
        compiler_params=_cparams(("arbitrary", "arbitrary")),
    )(dup, dup, up0, up0, w)


def adamw(contrib, w, m, v, *, name):
    L, R, C = w.shape
    assert contrib.shape == (L, N_DEV, R, C)
    tr = R
    if R * C > 256 * 1024 and R % 8 == 0:
        tr = 8
        while R % (tr * 2) == 0 and tr * 2 * C <= 256 * 1024:
            tr *= 2
    c1 = 1.0 - ADAM_B1 ** ADAM_STEP
    c2 = 1.0 - ADAM_B2 ** ADAM_STEP

    def body(c_ref, w_ref, m_ref, v_ref, g_ref, d_ref, nm_ref, nv_ref):
        g = c_ref[0].astype(F32)
        for s in range(1, N_DEV):
            g = g + c_ref[s].astype(F32)
        nm = ADAM_B1 * m_ref[...] + (1.0 - ADAM_B1) * g
        nv = ADAM_B2 * v_ref[...] + (1.0 - ADAM_B2) * (g * g)
        g_ref[...] = g
        nm_ref[...] = nm
        nv_ref[...] = nv
        d_ref[...] = -ADAM_LR * ((nm / c1) / (jnp.sqrt(nv / c2) + ADAM_EPS) + ADAM_WD * w_ref[...])

    blk = pl.BlockSpec((None, tr, C), lambda l, i: (l, i, 0))
    out = jax.ShapeDtypeStruct((L, R, C), F32)
    return pl.pallas_call(
        body, name=name, grid=(L, R // tr),
        in_specs=[pl.BlockSpec((None, N_DEV, tr, C), lambda l, i: (l, 0, i, 0)), blk, blk, blk],
        out_specs=[blk, blk, blk, blk], out_shape=[out, out, out, out],
        compiler_params=_cparams(("parallel", "parallel")),
    )(contrib, w, m, v)


def _my_position():
    x, y, c = lax.axis_index("x"), lax.axis_index("y"), lax.axis_index("c")
    return x, y, c, 4 * x + 2 * y + c


def _peers(x, y, c):
    out = []
    for r in range(1, N_DEV):
        px = 1 - x if r & 4 else x
        py = 1 - y if r & 2 else y
        pc = 1 - c if r & 1 else c
        out.append(((px, py, pc), 4 * px + 2 * py + pc))
    return out


def _run_exchange(plan, n, send_sems, recv_sems, local_sems):
    x, y, c, me = _my_position()
    copies = []
    for t in range(n):
        src, dst = plan(t, None, me)
        own = pltpu.make_async_copy(src, dst, local_sems.at[t])
        own.start()
        copies.append(own)
    remote = []
    for r, (peer, peer_index) in enumerate(_peers(x, y, c)):
        for t in range(n):
            src, dst = plan(t, peer_index, me)
            cp = pltpu.make_async_remote_copy(src_ref=src, dst_ref=dst, send_sem=send_sems.at[t, r],
                                              recv_sem=recv_sems.at[t, r], device_id=peer,
                                              device_id_type=pl.DeviceIdType.MESH)
            cp.start()
            remote.append(cp)
    for cp in remote:
        cp.wait_send()
    for cp in remote:
        cp.wait_recv()
    for cp in copies:
        cp.wait()


def all_gather(blocks, *, name):
    n = len(blocks)

    def body(*refs):
        srcs, outs = refs[:n], refs[n:2 * n]
        send_sems, recv_sems, local_sems = refs[2 * n:]
        _run_exchange(lambda t, peer_index, me: (srcs[t], outs[t].at[me]), n, send_sems, recv_sems, local_sems)

    hbm = pl.BlockSpec(memory_space=pl.ANY)
    return pl.pallas_call(
        body, name=name, in_specs=[hbm] * n, out_specs=[hbm] * n,
        out_shape=[jax.ShapeDtypeStruct((N_DEV, *b.shape), b.dtype) for b in blocks],
        scratch_shapes=[pltpu.SemaphoreType.DMA((n, N_DEV - 1)), pltpu.SemaphoreType.DMA((n, N_DEV - 1)),
                        pltpu.SemaphoreType.DMA((n,))],
    )(*blocks)


def scatter_to_owners(grads, landing, *, layer, name):
    n = len(grads)

    def body(*refs):
        srcs, outs = refs[:n], refs[2 * n:3 * n]
        send_sems, recv_sems, local_sems = refs[3 * n:]

        def plan(t, peer_index, me):
            return srcs[t].at[me if peer_index is None else peer_index], outs[t].at[layer, me]

        _run_exchange(plan, n, send_sems, recv_sems, local_sems)

    hbm = pl.BlockSpec(memory_space=pl.ANY)
    return pl.pallas_call(
        body, name=name, in_specs=[hbm] * (2 * n), out_specs=[hbm] * n,
        out_shape=[jax.ShapeDtypeStruct(a.shape, a.dtype) for a in landing],
        input_output_aliases={n + t: t for t in range(n)},
        scratch_shapes=[pltpu.SemaphoreType.DMA((n, N_DEV - 1)), pltpu.SemaphoreType.DMA((n, N_DEV - 1)),
                        pltpu.SemaphoreType.DMA((n,))],
    )(*grads, *landing)


WEIGHT_NAMES = ('mix_norm_g', 'mem_norm_g', 'w_in', 'gate_b', 'conv_w', 'conv_b', 'conv_ln_g', 'conv_ln_b',
                'w_conv_out', 'rel_bias', 'w_attn_out', 'w_mem_kv', 'w_mem_out', 'w_o', 'ffn_norm_g', 'w_up',
                'ffn_conv_w', 'ffn_conv_b', 'w_down', 'final_norm_g')
BIG = (('w_in', False), ('w_conv_out', False), ('w_attn_out', False), ('w_mem_out', False), ('w_up', False),
       ('w_mem_kv', True), ('w_o', True), ('w_down', True))


def _as_matrix(gathered, by_rows):
    return gathered.reshape(1, -1, gathered.shape[-1]) if by_rows else gathered


def _forward_layer(h, mem2, p, l, dims):
    W = p['gathered'][l]
    row = lambda name: p[name][l:l + 1]
    sv = {'h': h}
    sv['xn'] = rms_fwd(h, row('mix_norm_g'), name="mix_norm")
    sv['proj'] = mm_nn(sv['xn'], W['w_in'], out_dtype=BF16, name="w_in")
    sv['s'], sv['c'] = conv_fwd(sv['proj'], p['conv_w_full'][l], row('conv_b'), row('conv_ln_g'),
                                row('conv_ln_b'), name="conv_module")
    sv['bias'] = attn_bias(p['rel_pad'][l], name="attn_bias")
    sv['att'] = attn_fwd(sv['proj'], sv['bias'], qcol=dims['qcol'], dattn=dims['DA'], name="chunk_attn")
    sv['mn'] = rms_fwd(mem2, row('mem_norm_g'), name="mem_norm")
    sv['kv'] = mm_nn(sv['mn'], W['w_mem_kv'], out_dtype=BF16, name="w_mem_kv")
    sv['mo'] = memattn_fwd(sv['proj'], sv['kv'], qcol=dims['mcol'], name="mem_attn")
    sv['ys'] = (mm_nn(sv['s'], W['w_conv_out'], out_dtype=BF16, name="w_conv_out"),
                mm_nn(sv['att'], W['w_attn_out'], out_dtype=BF16, name="w_attn_out"),
                mm_nn(sv['mo'], W['w_mem_out'], out_dtype=BF16, name="w_mem_out"))
    sv['merged'] = merge_fwd(sv['proj'], row('gate_b'), sv['ys'], gcol=dims['gcol'], name="merge")
    sv['h1'] = mm_nn(sv['merged'], W['w_o'], out_dtype=F32, residual=h, name="w_o")
    sv['hn'] = rms_fwd(sv['h1'], row('ffn_norm_g'), name="ffn_norm")
    sv['up0'] = mm_nn(sv['hn'], W['w_up'], out_dtype=BF16, name="w_up")
    sv['act'] = ffn_act_fwd(sv['up0'], p['ffn_conv_w_full'][l], row('ffn_conv_b'), name="ffn_act")
    h2 = mm_nn(sv['act'], W['w_down'], out_dtype=F32, residual=sv['h1'], name="w_down")
    return h2, sv


def _backward_layer(dh, dhb, mem2, sv, p, l, dims):
    W = p['gathered'][l]
    row = lambda name: p[name][l:l + 1]
    g, small = {}, {}
    dact = mm_nt(dhb, W['w_down'], out_dtype=BF16, name="d_act")
    g['w_down'] = mm_tn(sv['act'], dhb, slots=1, out_dtype=BF16, name="g_w_down")
    dup = ffn_act_bwd(dact, sv['up0'], p['ffn_conv_w_full'][l], row('ffn_conv_b'), name="ffn_act_bwd")
    dup0, small['ffn_conv_w'], small['ffn_conv_b'] = ffn_conv_bwd(dup, sv['up0'], p['ffn_conv_w_full'][l],
                                                                  name="ffn_conv_bwd")
    dhn = mm_nt(dup0, W['w_up'], out_dtype=BF16, name="d_hn")
    g['w_up'] = mm_tn(sv['hn'], dup0, slots=N_DEV, out_dtype=BF16, name="g_w_up")
    dh1, dh1b, small['ffn_norm_g'] = rms_bwd(sv['h1'], row('ffn_norm_g'), dhn, dh, name="ffn_norm_bwd")
    dmerged = mm_nt(dh1b, W['w_o'], out_dtype=BF16, name="d_merged")
    g['w_o'] = mm_tn(sv['merged'], dh1b, slots=1, out_dtype=BF16, name="g_w_o")
    dproj, dy, small['gate_b'] = merge_bwd(dmerged, sv['proj'], row('gate_b'), sv['ys'], gcol=dims['gcol'],
                                           name="merge_bwd")
    ds = mm_nt(dy, W['w_conv_out'], lead=0, out_dtype=BF16, name="d_conv_out")
    g['w_conv_out'] = mm_tn(sv['s'], dy, lead=0, slots=N_DEV, out_dtype=BF16, name="g_w_conv_out")
    dc, small['conv_ln_g'], small['conv_ln_b'] = conv_bwd_ln(ds, sv['c'], row('conv_ln_g'), row('conv_ln_b'),
                                                             name="conv_ln_bwd")
    dproj, small['conv_w'], small['conv_b'] = conv_bwd_taps(dc, sv['proj'], p['conv_w_full'][l], dproj,
                                                            name="conv_taps_bwd")
    datt = mm_nt(dy, W['w_attn_out'], lead=1, out_dtype=BF16, name="d_attn_out")
    g['w_attn_out'] = mm_tn(sv['att'], dy, lead=1, slots=N_DEV, out_dtype=BF16, name="g_w_attn_out")
    dproj, dkw, dvw, dbias = attn_bwd(sv['proj'], sv['bias'], datt, dproj, qcol=dims['qcol'], dattn=dims['DA'],
                                      name="chunk_attn_bwd")
    dproj = window_sum(dkw, dproj, col=dims['qcol'] + dims['DA'], name="dk_windows")
    dproj = window_sum(dvw, dproj, col=dims['qcol'] + 2 * dims['DA'], name="dv_windows")
    small['rel_bias'] = attn_bias_bwd(dbias, name="attn_bias_bwd")[:, :2 * MAX_REL + 1]
    dmo = mm_nt(dy, W['w_mem_out'], lead=2, out_dtype=BF16, name="d_mem_out")
    g['w_mem_out'] = mm_tn(sv['mo'], dy, lead=2, slots=N_DEV, out_dtype=BF16, name="g_w_mem_out")
    dproj, dkv = memattn_bwd(sv['proj'], sv['kv'], dmo, dproj, qcol=dims['mcol'], name="mem_attn_bwd")
    dkvb = dkv.astype(BF16)
    g['w_mem_kv'] = mm_tn(sv['mn'], dkvb, slots=1, out_dtype=BF16, name="g_w_mem_kv")
    dmn = mm_nt(dkvb, W['w_mem_kv'], out_dtype=BF16, name="d_mem_norm")
    _, _, small['mem_norm_g'] = rms_bwd(mem2, row('mem_norm_g'), dmn, jnp.zeros(mem2.shape, F32),
                                        name="mem_norm_bwd")
    dxn = mm_nt(dproj, W['w_in'], out_dtype=BF16, name="d_xn")
    g['w_in'] = mm_tn(sv['xn'], dproj, slots=N_DEV, out_dtype=BF16, name="g_w_in")
    dh0, dh0b, small['mix_norm_g'] = rms_bwd(sv['h'], row('mix_norm_g'), dxn, dh1, name="mix_norm_bwd")
    return dh0, dh0b, g, small


def kernel(x, mem, mix_norm_g, mem_norm_g, w_in, gate_b, conv_w, conv_b, conv_ln_g, conv_ln_b, w_conv_out, rel_bias, w_attn_out, w_mem_kv, w_mem_out, w_o, ffn_norm_g, w_up, ffn_conv_w, ffn_conv_b, w_down, final_norm_g, loss_target, m_mix_norm_g, m_mem_norm_g, m_w_in, m_gate_b, m_conv_w, m_conv_b, m_conv_ln_g, m_conv_ln_b, m_w_conv_out, m_rel_bias, m_w_attn_out, m_w_mem_kv, m_w_mem_out, m_w_o, m_ffn_norm_g, m_w_up, m_ffn_conv_w, m_ffn_conv_b, m_w_down, m_final_norm_g, v_mix_norm_g, v_mem_norm_g, v_w_in, v_gate_b, v_conv_w, v_conv_b, v_conv_ln_g, v_conv_ln_b, v_w_conv_out, v_rel_bias, v_w_attn_out, v_w_mem_kv, v_w_mem_out, v_w_o, v_ffn_norm_g, v_w_up, v_ffn_conv_w, v_ffn_conv_b, v_w_down, v_final_norm_g):
    env = locals()
    w = {n: env[n] for n in WEIGHT_NAMES}
    mom = {n: env['m_' + n] for n in WEIGHT_NAMES}
    var = {n: env['v_' + n] for n in WEIGHT_NAMES}
    T, D = x.shape[-2:]
    L = w_in.shape[0]
    DC = conv_b.shape[-1]
    DA = N_DEV * w_attn_out.shape[-1] // 2
    dims = {'DA': DA, 'qcol': 2 * DC, 'mcol': 2 * DC + 3 * DA, 'gcol': 2 * DC + 3 * DA + D // 2}
    x2, mem2, target = x.reshape(T, D), mem.reshape(-1, D), loss_target.reshape(T, D)
    me = 4 * lax.axis_index("x") + 2 * lax.axis_index("y") + lax.axis_index("c")

    p = dict(w)
    p['gathered'] = []
    for l in range(L):
        got = all_gather([w[n][l].astype(BF16) for n, _ in BIG], name="gather_weights")
        p['gathered'].append({n: _as_matrix(a, by_rows) for (n, by_rows), a in zip(BIG, got)})
    taps, ffn_taps = all_gather([conv_w, ffn_conv_w], name="gather_taps")
    p['conv_w_full'] = jnp.moveaxis(taps, 0, 2).reshape(L, conv_w.shape[1], -1)
    p['ffn_conv_w_full'] = jnp.moveaxis(ffn_taps, 0, 2).reshape(L, ffn_conv_w.shape[1], -1)
    p['rel_pad'] = jnp.pad(rel_bias, ((0, 0), (0, 0), (0, REL_PAD - rel_bias.shape[-1])))

    h = x2
    saved = []
    for l in range(L):
        h, sv = _forward_layer(h, mem2, p, l, dims)
        saved.append(sv)
    loss_part, dh, dhb, d_final = loss_head(h, final_norm_g.reshape(1, D), target, name="loss_head")
    loss = lax.psum(loss_part[0, 0], ("x", "y", "c"))

    landing = [lax.empty((L, N_DEV, *w[n].shape[1:]), BF16) for n, _ in BIG]
    small = [None] * L
    for l in reversed(range(L)):
        dh, dhb, g, small[l] = _backward_layer(dh, dhb, mem2, saved[l], p, l, dims)
        grads = [g[n].reshape(N_DEV, *w[n].shape[1:]) for n, _ in BIG]
        landing = scatter_to_owners(grads, landing, layer=l, name="scatter_grads_%d" % l)
    contrib = dict(zip([n for n, _ in BIG], landing))

    small_names = [n for n in WEIGHT_NAMES if n not in contrib and n != 'final_norm_g']
    stacked = [jnp.stack([small[l][n] for l in range(L)]) for n in small_names] + [d_final]
    got = dict(zip(small_names + ['final_norm_g'], all_gather(stacked, name="gather_small_grads")))
    for n in ('conv_w', 'ffn_conv_w'):
        cs = w[n].shape[-1]
        got[n] = lax.dynamic_slice_in_dim(got[n], me * cs, cs, axis=3)
    for n in small_names + ['final_norm_g']:
        contrib[n] = got[n].reshape(1, N_DEV, -1, w[n].shape[-1])

    outs = {}
    for n in WEIGHT_NAMES:
        c = contrib[n]
        shp = c.shape[:1] + c.shape[2:]
        res = adamw(c, w[n].reshape(shp), mom[n].reshape(shp), var[n].reshape(shp), name="adamw_" + n)
        outs[n] = [r.reshape(w[n].shape) for r in res]
    return (loss, dh.reshape(x.shape),
            *[outs[n][0] for n in WEIGHT_NAMES], *[outs[n][1] for n in WEIGHT_NAMES],
            *[outs[n][2] for n in WEIGHT_NAMES], *[outs[n][3] for n in WEIGHT_NAMES])
```

```python
import functools

import numpy as np
import jax
import jax.numpy as jnp
from jax import lax
from jax.experimental import pallas as pl
from jax.experimental.pallas import tpu as pltpu

F32 = jnp.float32
BF16 = jnp.bfloat16

CHUNK = 64
LEFT_CHUNKS = 8
MAX_REL = 256
N_MEM_HEADS = 4
ATTN_HEAD_DIM = 64
CONV_WIDTH = 31
FFN_CONV_WIDTH = 3
EPS = 1e-6
NEG_INF = -1e30
ADAM_LR = 0.001
ADAM_B1 = 0.9
ADAM_B2 = 0.999
ADAM_EPS = 1e-08
ADAM_WD = 0.01
ADAM_STEP = 10
N_DEV = 8

VMEM_LIMIT_BYTES = 56 * 1024 * 1024
LANES = 128


def _cparams(sem=None):
    return pltpu.CompilerParams(dimension_semantics=sem, vmem_limit_bytes=VMEM_LIMIT_BYTES)


def _tile(n, want):
    if n <= want:
        return n
    t = (want // LANES) * LANES
    while t >= LANES:
        if n % t == 0:
            return t if 4 * t >= want or n > 2 * want else n
        t -= LANES
    return n


MM_TM, MM_TN, MM_TK = 1024, 1536, 2048


def _matmul(name, a, b, *, dims, grid, nk, a_spec, b_spec, out_spec, out_shape, acc_shape, residual=None,
            after=None):
    n_in = 2 + (residual is not None) + (after is not None)

    def body(*refs):
        a_ref, b_ref = refs[:2]
        o_ref = refs[n_in]

        def finish(r):
            if residual is not None:
                r = r + refs[2][...]
            o_ref[...] = r.astype(o_ref.dtype)

        part = lax.dot_general(a_ref[...], b_ref[...], (dims, ((), ())), preferred_element_type=F32)
        if nk == 1:
            finish(part)
            return
        acc = refs[-1]
        k = pl.program_id(2)

        @pl.when(k == 0)
        def _():
            acc[...] = part

        @pl.when((k > 0) & (k < nk - 1))
        def _():
            acc[...] += part

        @pl.when(k == nk - 1)
        def _():
            finish(acc[...] + part)

    in_specs, args = [a_spec, b_spec], [a, b]
    if residual is not None:
        in_specs.append(out_spec)
        args.append(residual)
    if after is not None:
        in_specs.append(pl.BlockSpec(memory_space=pl.ANY))
        args.append(after)
    return pl.pallas_call(
        body, name=name, grid=grid, in_specs=in_specs, out_specs=out_spec, out_shape=out_shape,
        scratch_shapes=[pltpu.VMEM(acc_shape, F32)] if nk > 1 else [],
        compiler_params=_cparams(("parallel", "parallel", "arbitrary")),
    )(*args)


def _lead_spec(arr, lead, block, index):
    if lead is None:
        assert arr.ndim == 2
        return pl.BlockSpec(block, index)
    assert arr.ndim == 3
    return pl.BlockSpec((None, *block), lambda i, j, k: (lead, *index(i, j, k)))


def mm_nn(a, w, *, out_dtype, residual=None, after=None, tm=MM_TM, tn=MM_TN, tk=MM_TK, name):
    M, K = a.shape
    S, K2, Ns = w.shape
    assert K == K2
    tm, tn, tk = _tile(M, tm), _tile(Ns, tn), _tile(K, tk)
    nb = Ns // tn
    return _matmul(
        name, a, w, dims=((1,), (0,)), grid=(M // tm, S * nb, K // tk), nk=K // tk,
        a_spec=pl.BlockSpec((tm, tk), lambda i, j, k: (i, k)),
        b_spec=pl.BlockSpec((None, tk, tn), lambda i, j, k: (j // nb, k, j % nb)),
        out_spec=pl.BlockSpec((tm, tn), lambda i, j, k: (i, j)),
        out_shape=jax.ShapeDtypeStruct((M, S * Ns), out_dtype), acc_shape=(tm, tn), residual=residual,
        after=after)


def mm_nt(a, w, *, out_dtype, lead=None, after=None, tm=MM_TM, tn=MM_TN, tk=MM_TK, name):
    M, N = a.shape[-2:]
    S, K, Ns = w.shape
    assert N == S * Ns
    tm, tn, tk = _tile(M, tm), _tile(K, tn), _tile(Ns, tk)
    nb = Ns // tk
    return _matmul(
        name, a, w, dims=((1,), (1,)), grid=(M // tm, K // tn, S * nb), nk=S * nb,
        a_spec=_lead_spec(a, lead, (tm, tk), lambda i, j, k: (i, k)),
        b_spec=pl.BlockSpec((None, tn, tk), lambda i, j, k: (k // nb, j, k % nb)),
        out_spec=pl.BlockSpec((tm, tn), lambda i, j, k: (i, j)),
        out_shape=jax.ShapeDtypeStruct((M, K), out_dtype), acc_shape=(tm, tn), after=after)


def mm_tn(a, b, *, slots, out_dtype, lead=None, tm=MM_TM, tn=MM_TN, tk=MM_TK, name):
    T, K = a.shape
    T2, N = b.shape[-2:]
    assert T == T2 and N % slots == 0
    Ns = N // slots
    tm, tn, tk = _tile(K, tm), _tile(Ns, tn), _tile(T, tk)
    nb = Ns // tn
    return _matmul(
        name, a, b, dims=((0,), (0,)), grid=(K // tm, slots * nb, T // tk), nk=T // tk,
        a_spec=pl.BlockSpec((tk, tm), lambda i, j, k: (k, i)),
        b_spec=_lead_spec(b, lead, (tk, tn), lambda i, j, k: (k, j)),
        out_spec=pl.BlockSpec((None, tm, tn), lambda i, j, k: (j // nb, i, j % nb)),
        out_shape=jax.ShapeDtypeStruct((slots, K, Ns), out_dtype), acc_shape=(tm, tn))


def rms_fwd(h, g, *, tr=256, name):
    T, D = h.shape
    tr = min(tr, T)

    def body(h_ref, g_ref, o_ref):
        x = h_ref[...]
        r = lax.rsqrt(jnp.mean(x * x, axis=-1, keepdims=True) + EPS)
        o_ref[...] = (x * r * g_ref[...]).astype(o_ref.dtype)

    return pl.pallas_call(
        body, name=name, grid=(T // tr,),
        in_specs=[pl.BlockSpec((tr, D), lambda i: (i, 0)), pl.BlockSpec((1, D), lambda i: (0, 0))],
        out_specs=pl.BlockSpec((tr, D), lambda i: (i, 0)),
        out_shape=jax.ShapeDtypeStruct((T, D), BF16),
        compiler_params=_cparams(("parallel",)),
    )(h, g)


def rms_bwd(h, g, dxn, dres, *, tr=256, name):
    T, D = h.shape
    tr = min(tr, T)

    def body(h_ref, g_ref, d_ref, r_ref, dh_ref, dhb_ref, dg_ref):
        i = pl.program_id(0)
        x = h_ref[...]
        r = lax.rsqrt(jnp.mean(x * x, axis=-1, keepdims=True) + EPS)
        xh = x * r
        d = d_ref[...].astype(F32)

        @pl.when(i == 0)
        def _():
            dg_ref[...] = jnp.zeros_like(dg_ref)

        dg_ref[...] += jnp.sum(d * xh, axis=0, keepdims=True)
        dxh = d * g_ref[...]
        dh = r * (dxh - xh * jnp.mean(dxh * xh, axis=-1, keepdims=True)) + r_ref[...]
        dh_ref[...] = dh
        dhb_ref[...] = dh.astype(BF16)

    row = pl.BlockSpec((tr, D), lambda i: (i, 0))
    vec = pl.BlockSpec((1, D), lambda i: (0, 0))
    return pl.pallas_call(
        body, name=name, grid=(T // tr,),
        in_specs=[row, vec, row, row],
        out_specs=[row, row, vec],
        out_shape=[jax.ShapeDtypeStruct((T, D), F32), jax.ShapeDtypeStruct((T, D), BF16),
                   jax.ShapeDtypeStruct((1, D), F32)],
        compiler_params=_cparams(("arbitrary",)),
    )(h, g, dxn, dres)


def loss_head(h, g, target, *, tr=256, name):
    T, D = h.shape
    tr = min(tr, T)

    def body(h_ref, g_ref, t_ref, loss_ref, dh_ref, dhb_ref, dg_ref):
        i = pl.program_id(0)
        x = h_ref[...]
        r = lax.rsqrt(jnp.mean(x * x, axis=-1, keepdims=True) + EPS)
        xh = x * r
        gg = g_ref[...]
        err = xh * gg - t_ref[...]

        @pl.when(i == 0)
        def _():
            dg_ref[...] = jnp.zeros_like(dg_ref)
            loss_ref[...] = jnp.zeros_like(loss_ref)

        loss_ref[...] += 0.5 * jnp.sum(jnp.mean(err * err, axis=-1, keepdims=True))
        dy = err * (1.0 / D)
        dg_ref[...] += jnp.sum(dy * xh, axis=0, keepdims=True)
        dxh = dy * gg
        dh = r * (dxh - xh * jnp.mean(dxh * xh, axis=-1, keepdims=True))
        dh_ref[...] = dh
        dhb_ref[...] = dh.astype(BF16)

    row = pl.BlockSpec((tr, D), lambda i: (i, 0))
    vec = pl.BlockSpec((1, D), lambda i: (0, 0))
    return pl.pallas_call(
        body, name=name, grid=(T // tr,),
        in_specs=[row, vec, row],
        out_specs=[pl.BlockSpec((8, LANES), lambda i: (0, 0)), row, row, vec],
        out_shape=[jax.ShapeDtypeStruct((8, LANES), F32), jax.ShapeDtypeStruct((T, D), F32),
                   jax.ShapeDtypeStruct((T, D), BF16), jax.ShapeDtypeStruct((1, D), F32)],
        compiler_params=_cparams(("arbitrary",)),
    )(h, g, target)


CONV_HALO = 32


def _glu_ext(prev_ref, main_ref, hs_ref, i, dc, tt):
    up = prev_ref[...].astype(F32)
    hp = up[:, :dc] * jax.nn.sigmoid(up[:, dc:])
    hs_ref[pl.ds(0, CONV_HALO), :] = jnp.where(i > 0, hp, 0.0)
    um = main_ref[...].astype(F32)
    sig = jax.nn.sigmoid(um[:, dc:])
    hs_ref[pl.ds(CONV_HALO, tt), :] = um[:, :dc] * sig
    return um[:, :dc], sig


def conv_fwd(proj, conv_w, conv_b, ln_g, ln_b, *, tt=256, name):
    T = proj.shape[0]
    W, DC = conv_w.shape
    tt = min(tt, T)
    hb = tt // CONV_HALO

    def body(prev_ref, main_ref, w_ref, b_ref, g_ref, bb_ref, s_ref, c_ref, hs_ref):
        i = pl.program_id(0)
        _glu_ext(prev_ref, main_ref, hs_ref, i, DC, tt)
        c = jnp.zeros((tt, DC), F32) + b_ref[...]
        for k in range(W):
            c = c + w_ref[pl.ds(k, 1), :] * hs_ref[pl.ds(CONV_HALO - (W - 1) + k, tt), :]
        c_ref[...] = c
        mu = jnp.mean(c, axis=-1, keepdims=True)
        xc = c - mu
        var = jnp.mean(xc * xc, axis=-1, keepdims=True)
        y = xc * lax.rsqrt(var + EPS) * g_ref[...] + bb_ref[...]
        s_ref[...] = (y * jax.nn.sigmoid(y)).astype(s_ref.dtype)

    vec = pl.BlockSpec((1, DC), lambda i: (0, 0))
    return pl.pallas_call(
        body, name=name, grid=(T // tt,),
        in_specs=[pl.BlockSpec((CONV_HALO, 2 * DC), lambda i: (jnp.maximum(i * hb - 1, 0), 0)),
                  pl.BlockSpec((tt, 2 * DC), lambda i: (i, 0)),
                  pl.BlockSpec((W, DC), lambda i: (0, 0)), vec, vec, vec],
        out_specs=[pl.BlockSpec((tt, DC), lambda i: (i, 0)), pl.BlockSpec((tt, DC), lambda i: (i, 0))],
        out_shape=[jax.ShapeDtypeStruct((T, DC), BF16), jax.ShapeDtypeStruct((T, DC), F32)],
        scratch_shapes=[pltpu.VMEM((tt + CONV_HALO, DC), F32)],
        compiler_params=_cparams(("parallel",)),
    )(proj, proj, conv_w, conv_b, ln_g, ln_b)


def conv_bwd_ln(ds, c, ln_g, ln_b, *, tt=256, name):
    T, DC = c.shape
    tt = min(tt, T)

    def body(ds_ref, c_ref, g_ref, bb_ref, dc_ref, dg_ref, db_ref):
        i = pl.program_id(0)
        c = c_ref[...]
        mu = jnp.mean(c, axis=-1, keepdims=True)
        xc = c - mu
        var = jnp.mean(xc * xc, axis=-1, keepdims=True)
        rstd = lax.rsqrt(var + EPS)
        xh = xc * rstd
        y = xh * g_ref[...] + bb_ref[...]
        sg = jax.nn.sigmoid(y)
        dy = ds_ref[...].astype(F32) * (sg * (1.0 + y * (1.0 - sg)))

        @pl.when(i == 0)
        def _():
            dg_ref[...] = jnp.zeros_like(dg_ref)
            db_ref[...] = jnp.zeros_like(db_ref)

        db_ref[...] += jnp.sum(dy, axis=0, keepdims=True)
        dg_ref[...] += jnp.sum(dy * xh, axis=0, keepdims=True)
        dxh = dy * g_ref[...]
        dc_ref[...] = rstd * (dxh - jnp.mean(dxh, axis=-1, keepdims=True)
                              - xh * jnp.mean(dxh * xh, axis=-1, keepdims=True))

    row = pl.BlockSpec((tt, DC), lambda i: (i, 0))
    vec = pl.BlockSpec((1, DC), lambda i: (0, 0))
    return pl.pallas_call(
        body, name=name, grid=(T // tt,),
        in_specs=[row, row, vec, vec], out_specs=[row, vec, vec],
        out_shape=[jax.ShapeDtypeStruct((T, DC), F32), jax.ShapeDtypeStruct((1, DC), F32),
                   jax.ShapeDtypeStruct((1, DC), F32)],
        compiler_params=_cparams(("arbitrary",)),
    )(ds, c, ln_g, ln_b)


def conv_bwd_taps(dc, proj, conv_w, dproj, *, tt=256, name):
    T, DC = dc.shape
    W = conv_w.shape[0]
    tt = min(tt, T)
    hb = tt // CONV_HALO
    n_t = T // tt
    last_halo = T // CONV_HALO - 1

    def body(dc_ref, dcn_ref, prev_ref, main_ref, w_ref, _, dp_ref, dw_ref, db_ref, hs_ref, ds_ref):
        i = pl.program_id(0)
        a, sig = _glu_ext(prev_ref, main_ref, hs_ref, i, DC, tt)
        d = dc_ref[...]
        ds_ref[pl.ds(0, tt), :] = d
        ds_ref[pl.ds(tt, CONV_HALO), :] = jnp.where(i < n_t - 1, dcn_ref[...], 0.0)

        @pl.when(i == 0)
        def _():
            dw_ref[...] = jnp.zeros_like(dw_ref)
            db_ref[...] = jnp.zeros_like(db_ref)

        db_ref[...] += jnp.sum(d, axis=0, keepdims=True)
        dh = jnp.zeros((tt, DC), F32)
        for k in range(W):
            dh = dh + w_ref[pl.ds(k, 1), :] * ds_ref[pl.ds(W - 1 - k, tt), :]
            dw_ref[pl.ds(k, 1), :] += jnp.sum(d * hs_ref[pl.ds(CONV_HALO - (W - 1) + k, tt), :],
                                              axis=0, keepdims=True)
        da = dh * sig
        dgt = dh * a * sig * (1.0 - sig)
        dp_ref[...] = jnp.concatenate([da, dgt], axis=1).astype(dp_ref.dtype)

    return pl.pallas_call(
        body, name=name, grid=(n_t,),
        in_specs=[pl.BlockSpec((tt, DC), lambda i: (i, 0)),
                  pl.BlockSpec((CONV_HALO, DC), lambda i: (jnp.minimum((i + 1) * hb, last_halo), 0)),
                  pl.BlockSpec((CONV_HALO, 2 * DC), lambda i: (jnp.maximum(i * hb - 1, 0), 0)),
                  pl.BlockSpec((tt, 2 * DC), lambda i: (i, 0)),
                  pl.BlockSpec((W, DC), lambda i: (0, 0)),
                  pl.BlockSpec(memory_space=pl.ANY)],
        out_specs=[pl.BlockSpec((tt, 2 * DC), lambda i: (i, 0)),
                   pl.BlockSpec((W, DC), lambda i: (0, 0)), pl.BlockSpec((1, DC), lambda i: (0, 0))],
        out_shape=[jax.ShapeDtypeStruct(dproj.shape, dproj.dtype), jax.ShapeDtypeStruct((W, DC), F32),
                   jax.ShapeDtypeStruct((1, DC), F32)],
        scratch_shapes=[pltpu.VMEM((tt + CONV_HALO, DC), F32), pltpu.VMEM((tt + CONV_HALO, DC), F32)],
        input_output_aliases={5: 0},
        compiler_params=_cparams(("arbitrary",)),
    )(dc, dc, proj, proj, conv_w, dproj)


ATT_TQ = 256
ATT_NB = 1 + (LEFT_CHUNKS * CHUNK) // ATT_TQ
ATT_WIN = ATT_NB * ATT_TQ
ATT_PERIOD = 1024
REL_PAD = 640


def _rel_onehot():
    m = lax.broadcasted_iota(jnp.int32, (REL_PAD, ATT_PERIOD), 1)
    r = lax.broadcasted_iota(jnp.int32, (REL_PAD, ATT_PERIOD), 0)
    qk = jnp.where(m < ATT_PERIOD - ATT_TQ, -m, ATT_PERIOD - m)
    idx = jnp.clip(LEFT_CHUNKS * CHUNK + qk, -MAX_REL, MAX_REL) + MAX_REL
    return (idx == r).astype(F32)


def _band_valid():
    q = lax.broadcasted_iota(jnp.int32, (ATT_TQ, ATT_WIN), 0)
    k = lax.broadcasted_iota(jnp.int32, (ATT_TQ, ATT_WIN), 1)
    j = k // CHUNK - q // CHUNK
    return (j >= 0) & (j <= LEFT_CHUNKS)


def attn_bias(rel_pad, *, name):
    H = rel_pad.shape[0]

    def body(rb_ref, o_ref):
        g = jnp.dot(rb_ref[...], _rel_onehot(), preferred_element_type=F32, precision=lax.Precision.HIGHEST)
        valid = _band_valid()
        for h in range(H):
            row = jnp.broadcast_to(g[h:h + 1, :], (ATT_TQ, ATT_PERIOD))
            t = pltpu.roll(row, 0, 1, stride=1, stride_axis=0)
            o_ref[h] = jnp.where(valid, t[:, :ATT_WIN], NEG_INF)

    return pl.pallas_call(
        body, name=name,
        in_specs=[pl.BlockSpec(memory_space=pltpu.VMEM)], out_specs=pl.BlockSpec(memory_space=pltpu.VMEM),
        out_shape=jax.ShapeDtypeStruct((H, ATT_TQ, ATT_WIN), F32),
        compiler_params=_cparams(),
    )(rel_pad)


def attn_bias_bwd(dbias, *, name):
    H = dbias.shape[0]
    band = (LEFT_CHUNKS + 1) * CHUNK

    def body(d_ref, o_ref, acc_ref):
        r = lax.broadcasted_iota(jnp.int32, (CHUNK, CHUNK), 0)
        c = lax.broadcasted_iota(jnp.int32, (CHUNK, CHUNK), 1)
        rev = (r + c == CHUNK - 1).astype(F32)
        pad = jnp.zeros((CHUNK, ATT_PERIOD - ATT_WIN), F32)
        for h in range(H):
            tab = d_ref[h, pl.ds(0, CHUNK), :]
            for cq in range(1, ATT_TQ // CHUNK):
                tab = tab + pltpu.roll(d_ref[h, pl.ds(cq * CHUNK, CHUNK), :], ATT_WIN - cq * CHUNK, 1)
            t = jnp.dot(rev, tab, preferred_element_type=F32, precision=lax.Precision.HIGHEST)
            u = pltpu.roll(jnp.concatenate([t, pad], axis=1), 0, 1, stride=1, stride_axis=0)
            acc_ref[pl.ds(h, 1), :] = jnp.sum(u, axis=0, keepdims=True)
        j = lax.broadcasted_iota(jnp.int32, (REL_PAD, ATT_PERIOD), 1)
        rr = lax.broadcasted_iota(jnp.int32, (REL_PAD, ATT_PERIOD), 0)
        idx = jnp.clip(LEFT_CHUNKS * CHUNK + CHUNK - 1 - j, -MAX_REL, MAX_REL) + MAX_REL
        onehot = ((idx == rr) & (j < band + CHUNK - 1)).astype(F32)
        o_ref[...] = lax.dot_general(acc_ref[...], onehot, (((1,), (1,)), ((), ())),
                                     preferred_element_type=F32, precision=lax.Precision.HIGHEST)

    return pl.pallas_call(
        body, name=name,
        in_specs=[pl.BlockSpec(memory_space=pltpu.VMEM)], out_specs=pl.BlockSpec(memory_space=pltpu.VMEM),
        out_shape=jax.ShapeDtypeStruct((H, REL_PAD), F32),
        scratch_shapes=[pltpu.VMEM((H, ATT_PERIOD), F32)],
        compiler_params=_cparams(),
    )(dbias)


def _attn_specs(T, qcol, dattn):
    dn = dattn // LANES

    def kv(col0, back):
        return pl.BlockSpec((ATT_TQ, LANES), lambda hp, i: (jnp.maximum(i - back, 0), col0 // LANES + hp))

    q = pl.BlockSpec((ATT_TQ, LANES), lambda hp, i: (i, qcol // LANES + hp))
    ks = [kv(qcol + dattn, ATT_NB - 1 - b) for b in range(ATT_NB)]
    vs = [kv(qcol + 2 * dattn, ATT_NB - 1 - b) for b in range(ATT_NB)]
    return q, ks, vs


def _attn_scores(q, kw, bias, i):
    s = lax.dot_general(q, kw, (((1,), (1,)), ((), ())), preferred_element_type=F32)
    s = s * (ATTN_HEAD_DIM ** -0.5) + bias
    col = lax.broadcasted_iota(jnp.int32, s.shape, 1)
    s = jnp.where(col // ATT_TQ + i >= ATT_NB - 1, s, NEG_INF)
    s = s - jnp.max(s, axis=-1, keepdims=True)
    p = jnp.exp(s)
    return p / jnp.sum(p, axis=-1, keepdims=True)


def attn_fwd(proj, bias, *, qcol, dattn, name):
    T = proj.shape[0]
    HP = dattn // LANES
    q_spec, k_specs, v_specs = _attn_specs(T, qcol, dattn)

    def body(*refs):
        q_ref = refs[0]
        k_refs = refs[1:1 + ATT_NB]
        v_refs = refs[1 + ATT_NB:1 + 2 * ATT_NB]
        b_ref, o_ref = refs[1 + 2 * ATT_NB:]
        i = pl.program_id(1)
        lane = lax.broadcasted_iota(jnp.int32, (ATT_TQ, LANES), 1)
        kw = jnp.concatenate([r[...] for r in k_refs], axis=0)
        vw = jnp.concatenate([r[...] for r in v_refs], axis=0)
        q = q_ref[...]
        out = jnp.zeros((ATT_TQ, LANES), F32)
        for hh in range(2):
            mine = (lane // ATTN_HEAD_DIM) == hh
            p = _attn_scores(jnp.where(mine, q, jnp.zeros_like(q)), kw, b_ref[hh], i)
            o = jnp.dot(p.astype(BF16), vw, preferred_element_type=F32)
            out = jnp.where(mine, o, out)
        o_ref[...] = out.astype(o_ref.dtype)

    return pl.pallas_call(
        body, name=name, grid=(HP, T // ATT_TQ),
        in_specs=[q_spec, *k_specs, *v_specs,
                  pl.BlockSpec((2, ATT_TQ, ATT_WIN), lambda hp, i: (hp, 0, 0))],
        out_specs=pl.BlockSpec((ATT_TQ, LANES), lambda hp, i: (i, hp)),
        out_shape=jax.ShapeDtypeStruct((T, dattn), BF16),
        compiler_params=_cparams(("parallel", "parallel")),
    )(*([proj] * (1 + 2 * ATT_NB)), bias)


def attn_bwd(proj, bias, dout, dproj, *, qcol, dattn, name):
    T = proj.shape[0]
    HP = dattn // LANES
    nq = T // ATT_TQ
    q_spec, k_specs, v_specs = _attn_specs(T, qcol, dattn)

    def body(*refs):
        q_ref = refs[0]
        k_refs = refs[1:1 + ATT_NB]
        v_refs = refs[1 + ATT_NB:1 + 2 * ATT_NB]
        b_ref, do_ref, _, dq_ref, dk_ref, dv_ref, db_ref = refs[1 + 2 * ATT_NB:]
        i = pl.program_id(1)
        lane = lax.broadcasted_iota(jnp.int32, (ATT_TQ, LANES), 1)
        kw = jnp.concatenate([r[...] for r in k_refs], axis=0)
        vw = jnp.concatenate([r[...] for r in v_refs], axis=0)
        q = q_ref[...]
        do = do_ref[...]
        dq = jnp.zeros((ATT_TQ, LANES), F32)
        dk = jnp.zeros((ATT_WIN, LANES), F32)
        dv = jnp.zeros((ATT_WIN, LANES), F32)

        @pl.when(i == 0)
        def _():
            db_ref[...] = jnp.zeros_like(db_ref)

        for hh in range(2):
            mine = (lane // ATTN_HEAD_DIM) == hh
            qh = jnp.where(mine, q, jnp.zeros_like(q))
            doh = jnp.where(mine, do, jnp.zeros_like(do))
            p = _attn_scores(qh, kw, b_ref[hh], i)
            dp = lax.dot_general(doh, vw, (((1,), (1,)), ((), ())), preferred_element_type=F32)
            dv = dv + lax.dot_general(p.astype(BF16), doh, (((0,), (0,)), ((), ())), preferred_element_type=F32)
            ds = p * (dp - jnp.sum(dp * p, axis=-1, keepdims=True))
            db_ref[hh] += ds
            dsb = (ds * (ATTN_HEAD_DIM ** -0.5)).astype(BF16)
            dq = jnp.where(mine, jnp.dot(dsb, kw, preferred_element_type=F32), dq)
            dk = dk + lax.dot_general(dsb, qh, (((0,), (0,)), ((), ())), preferred_element_type=F32)
        dq_ref[...] = dq.astype(dq_ref.dtype)
        dk_ref[...] = dk
        dv_ref[...] = dv

    win = pl.BlockSpec((None, ATT_WIN, LANES), lambda hp, i: (i, 0, hp))
    return pl.pallas_call(
        body, name=name, grid=(HP, nq),
        in_specs=[q_spec, *k_specs, *v_specs,
                  pl.BlockSpec((2, ATT_TQ, ATT_WIN), lambda hp, i: (hp, 0, 0)),
                  pl.BlockSpec((ATT_TQ, LANES), lambda hp, i: (i, hp)),
                  pl.BlockSpec(memory_space=pl.ANY)],
        out_specs=[pl.BlockSpec((ATT_TQ, LANES), lambda hp, i: (i, qcol // LANES + hp)), win, win,
                   pl.BlockSpec((2, ATT_TQ, ATT_WIN), lambda hp, i: (hp, 0, 0))],
        out_shape=[jax.ShapeDtypeStruct(dproj.shape, dproj.dtype),
                   jax.ShapeDtypeStruct((nq, ATT_WIN, dattn), F32), jax.ShapeDtypeStruct((nq, ATT_WIN, dattn), F32),
                   jax.ShapeDtypeStruct((2 * HP, ATT_TQ, ATT_WIN), F32)],
        input_output_aliases={3 + 2 * ATT_NB: 0},
        compiler_params=_cparams(("arbitrary", "arbitrary")),
    )(*([proj] * (1 + 2 * ATT_NB)), bias, dout, dproj)


def window_sum(win, dproj, *, col, name):
    nq, _, C = win.shape
    assert col % C == 0

    def body(*refs):
        w_refs = refs[:ATT_NB]
        o_ref = refs[ATT_NB + 1]
        j = pl.program_id(0)
        acc = w_refs[0][...]
        for b in range(1, ATT_NB):
            acc = acc + jnp.where(j + b < nq, w_refs[b][...], 0.0)
        o_ref[...] = acc.astype(o_ref.dtype)

    def part(b):
        return pl.BlockSpec((None, ATT_TQ, C), lambda j: (jnp.minimum(j + b, nq - 1), ATT_NB - 1 - b, 0))

    return pl.pallas_call(
        body, name=name, grid=(nq,),
        in_specs=[part(b) for b in range(ATT_NB)] + [pl.BlockSpec(memory_space=pl.ANY)],
        out_specs=pl.BlockSpec((ATT_TQ, C), lambda j: (j, col // C)),
        out_shape=jax.ShapeDtypeStruct(dproj.shape, dproj.dtype),
        input_output_aliases={ATT_NB: 0},
        compiler_params=_cparams(("arbitrary",)),
    )(*([win] * ATT_NB), dproj)


def _mem_probs(q, km, scale):
    s = lax.dot_general(q, km, (((1,), (1,)), ((), ())), preferred_element_type=F32) * scale
    s = s - jnp.max(s, axis=-1, keepdims=True)
    p = jnp.exp(s)
    return p / jnp.sum(p, axis=-1, keepdims=True)


def memattn_fwd(proj, kv, *, qcol, tq=512, name):
    T = proj.shape[0]
    M, dm2 = kv.shape
    DM = dm2 // 2
    hd = DM // N_MEM_HEADS
    tq = min(tq, T)

    def body(q_ref, kv_ref, o_ref):
        for h in range(N_MEM_HEADS):
            sl = pl.ds(h * hd, hd)
            p = _mem_probs(q_ref[:, sl], kv_ref[:, sl], hd ** -0.5)
            o = jnp.dot(p.astype(BF16), kv_ref[:, pl.ds(DM + h * hd, hd)], preferred_element_type=F32)
            o_ref[:, sl] = o.astype(o_ref.dtype)

    return pl.pallas_call(
        body, name=name, grid=(T // tq,),
        in_specs=[pl.BlockSpec((tq, DM), lambda i: (i, qcol // DM)),
                  pl.BlockSpec((M, 2 * DM), lambda i: (0, 0))],
        out_specs=pl.BlockSpec((tq, DM), lambda i: (i, 0)),
        out_shape=jax.ShapeDtypeStruct((T, DM), BF16),
        compiler_params=_cparams(("parallel",)),
    )(proj, kv)


def memattn_bwd(proj, kv, dout, dproj, *, qcol, tq=512, name):
    T = proj.shape[0]
    M, dm2 = kv.shape
    DM = dm2 // 2
    hd = DM // N_MEM_HEADS
    tq = min(tq, T)

    def body(q_ref, kv_ref, do_ref, _, dq_ref, dkv_ref):
        i = pl.program_id(0)

        @pl.when(i == 0)
        def _():
            dkv_ref[...] = jnp.zeros_like(dkv_ref)

        for h in range(N_MEM_HEADS):
            sl = pl.ds(h * hd, hd)
            vsl = pl.ds(DM + h * hd, hd)
            q = q_ref[:, sl]
            do = do_ref[:, sl]
            p = _mem_probs(q, kv_ref[:, sl], hd ** -0.5)
            dp = lax.dot_general(do, kv_ref[:, vsl], (((1,), (1,)), ((), ())), preferred_element_type=F32)
            dkv_ref[:, vsl] += lax.dot_general(p.astype(BF16), do, (((0,), (0,)), ((), ())),
                                               preferred_element_type=F32)
            ds = p * (dp - jnp.sum(dp * p, axis=-1, keepdims=True))
            dsb = (ds * (hd ** -0.5)).astype(BF16)
            dq_ref[:, sl] = jnp.dot(dsb, kv_ref[:, sl], preferred_element_type=F32).astype(dq_ref.dtype)
            dkv_ref[:, sl] += lax.dot_general(dsb, q, (((0,), (0,)), ((), ())), preferred_element_type=F32)

    return pl.pallas_call(
        body, name=name, grid=(T // tq,),
        in_specs=[pl.BlockSpec((tq, DM), lambda i: (i, qcol // DM)),
                  pl.BlockSpec((M, 2 * DM), lambda i: (0, 0)),
                  pl.BlockSpec((tq, DM), lambda i: (i, 0)),
                  pl.BlockSpec(memory_space=pl.ANY)],
        out_specs=[pl.BlockSpec((tq, DM), lambda i: (i, qcol // DM)),
                   pl.BlockSpec((M, 2 * DM), lambda i: (0, 0))],
        out_shape=[jax.ShapeDtypeStruct(dproj.shape, dproj.dtype), jax.ShapeDtypeStruct((M, 2 * DM), F32)],
        input_output_aliases={3: 0},
        compiler_params=_cparams(("arbitrary",)),
    )(proj, kv, dout, dproj)


def merge_fwd(proj, gate_b, ys, *, gcol, tr=512, tc=512, name):
    T = proj.shape[0]
    D = ys[0].shape[1]
    tr, tc = min(tr, T), _tile(D, tc)
    nd = D // tc

    def body(g0, g1, g2, b0, b1, b2, y0, y1, y2, o_ref):
        acc = jnp.zeros((tr, tc), F32)
        for g, b, y in ((g0, b0, y0), (g1, b1, y1), (g2, b2, y2)):
            acc = acc + jax.nn.sigmoid(g[...].astype(F32) + b[...]) * y[...].astype(F32)
        o_ref[...] = acc.astype(o_ref.dtype)

    def gate(b):
        return pl.BlockSpec((tr, tc), lambda i, j: (i, gcol // tc + b * nd + j))

    def bias(b):
        return pl.BlockSpec((1, tc), lambda i, j: (0, b * nd + j))

    blk = pl.BlockSpec((tr, tc), lambda i, j: (i, j))
    return pl.pallas_call(
        body, name=name, grid=(T // tr, nd),
        in_specs=[gate(0), gate(1), gate(2), bias(0), bias(1), bias(2), blk, blk, blk],
        out_specs=blk,
        out_shape=jax.ShapeDtypeStruct((T, D), BF16),
        compiler_params=_cparams(("parallel", "parallel")),
    )(proj, proj, proj, gate_b, gate_b, gate_b, *ys)


def merge_bwd(dmerged, proj, gate_b, ys, *, gcol, tr=512, tc=512, name):
    T, D_IN = proj.shape
    D = ys[0].shape[1]
    tr, tc = min(tr, T), _tile(D, tc)
    nd = D // tc

    def body(dm_ref, g_ref, b_ref, y0, y1, y2, dp_ref, dy_ref, db_ref):
        br = pl.program_id(0)
        i = pl.program_id(2)
        y = jnp.where(br == 0, y0[...], jnp.where(br == 1, y1[...], y2[...])).astype(F32)
        dm = dm_ref[...].astype(F32)
        sg = jax.nn.sigmoid(g_ref[...].astype(F32) + b_ref[...])
        dy_ref[...] = (dm * sg).astype(dy_ref.dtype)
        dg = dm * y * sg * (1.0 - sg)
        dp_ref[...] = dg.astype(dp_ref.dtype)

        @pl.when(i == 0)
        def _():
            db_ref[...] = jnp.zeros_like(db_ref)

        db_ref[...] += jnp.sum(dg, axis=0, keepdims=True)

    def ysp(b):
        return pl.BlockSpec((tr, tc), lambda br, j, i: (jnp.where(br == b, i, 0), jnp.where(br == b, j, 0)))

    return pl.pallas_call(
        body, name=name, grid=(3, nd, T // tr),
        in_specs=[pl.BlockSpec((tr, tc), lambda br, j, i: (i, j)),
                  pl.BlockSpec((tr, tc), lambda br, j, i: (i, gcol // tc + br * nd + j)),
                  pl.BlockSpec((1, tc), lambda br, j, i: (0, br * nd + j)),
                  ysp(0), ysp(1), ysp(2)],
        out_specs=[pl.BlockSpec((tr, tc), lambda br, j, i: (i, gcol // tc + br * nd + j)),
                   pl.BlockSpec((None, tr, tc), lambda br, j, i: (br, i, j)),
                   pl.BlockSpec((1, tc), lambda br, j, i: (0, br * nd + j))],
        out_shape=[jax.ShapeDtypeStruct((T, D_IN), BF16), jax.ShapeDtypeStruct((3, T, D), BF16),
                   jax.ShapeDtypeStruct((1, 3 * D), F32)],
        compiler_params=_cparams(("arbitrary", "arbitrary", "arbitrary")),
    )(dmerged, proj, gate_b, *ys)


FFN_HALO = 16


def _ffn_conv(prev_ref, main_ref, w_ref, b_ref, xs_ref, i, tt):
    xs_ref[pl.ds(0, FFN_HALO), :] = jnp.where(i > 0, prev_ref[...].astype(F32), 0.0)
    x = main_ref[...].astype(F32)
    xs_ref[pl.ds(FFN_HALO, tt), :] = x
    return (w_ref[pl.ds(0, 1), :] * xs_ref[pl.ds(FFN_HALO - 2, tt), :]
            + w_ref[pl.ds(1, 1), :] * xs_ref[pl.ds(FFN_HALO - 1, tt), :]
            + w_ref[pl.ds(2, 1), :] * x + b_ref[...])


def _ffn_specs(tt, tc, nv, order):
    hb = tt // FFN_HALO

    def mk(shape, f):
        return pl.BlockSpec(shape, (lambda i, j: f(i, j)) if order == "ij" else (lambda j, i: f(i, j)))

    out = []
    for off in (0, nv):
        out += [mk((FFN_HALO, tc), lambda i, j, off=off: (jnp.maximum(i * hb - 1, 0), off + j)),
                mk((tt, tc), lambda i, j, off=off: (i, off + j))]
    out += [mk((FFN_CONV_WIDTH, tc), lambda i, j, off=off: (0, off + j)) for off in (0, nv)]
    out += [mk((1, tc), lambda i, j, off=off: (0, off + j)) for off in (0, nv)]
    return out


def ffn_act_fwd(up0, w, b, *, tt=512, tc=512, name):
    T, dff2 = up0.shape
    DFF = dff2 // 2
    tt, tc = min(tt, T), _tile(DFF, tc)
    nv = DFF // tc

    def body(vp, vm, gp, gm, wv, wg, bv, bg, o_ref, xs_ref):
        i = pl.program_id(0)
        val = _ffn_conv(vp, vm, wv, bv, xs_ref, i, tt)
        gt = _ffn_conv(gp, gm, wg, bg, xs_ref, i, tt)
        o_ref[...] = (gt * jax.nn.sigmoid(gt) * val).astype(o_ref.dtype)

    return pl.pallas_call(
        body, name=name, grid=(T // tt, nv),
        in_specs=_ffn_specs(tt, tc, nv, "ij"),
        out_specs=pl.BlockSpec((tt, tc), lambda i, j: (i, j)),
        out_shape=jax.ShapeDtypeStruct((T, DFF), BF16),
        scratch_shapes=[pltpu.VMEM((FFN_HALO + tt, tc), F32)],
        compiler_params=_cparams(("parallel", "parallel")),
    )(up0, up0, up0, up0, w, w, b, b)


def ffn_act_bwd(dact, up0, w, b, *, tt=512, tc=512, name):
    T, dff2 = up0.shape
    DFF = dff2 // 2
    tt, tc = min(tt, T), _tile(DFF, tc)
    nv = DFF // tc

    def body(da_ref, vp, vm, gp, gm, wv, wg, bv, bg, o_ref, xs_ref):
        i = pl.program_id(0)
        val = _ffn_conv(vp, vm, wv, bv, xs_ref, i, tt)
        gt = _ffn_conv(gp, gm, wg, bg, xs_ref, i, tt)
        da = da_ref[...].astype(F32)
        sg = jax.nn.sigmoid(gt)
        o_ref[0] = (da * gt * sg).astype(o_ref.dtype)
        o_ref[1] = (da * val * (sg * (1.0 + gt * (1.0 - sg)))).astype(o_ref.dtype)

    return pl.pallas_call(
        body, name=name, grid=(T // tt, nv),
        in_specs=[pl.BlockSpec((tt, tc), lambda i, j: (i, j))] + _ffn_specs(tt, tc, nv, "ij"),
        out_specs=pl.BlockSpec((2, tt, tc), lambda i, j: (0, i, j)),
        out_shape=jax.ShapeDtypeStruct((2, T, DFF), BF16),
        scratch_shapes=[pltpu.VMEM((FFN_HALO + tt, tc), F32)],
        compiler_params=_cparams(("parallel", "parallel")),
    )(dact, up0, up0, up0, up0, w, w, b, b)


def ffn_conv_bwd(dup, up0, w, *, tt=512, tc=512, name):
    T, dff2 = up0.shape
    DFF = dff2 // 2
    tt, tc = min(tt, T), _tile(DFF, tc)
    nv = DFF // tc
    hb = tt // FFN_HALO
    n_t = T // tt
    last_halo = T // FFN_HALO - 1

    def body(d_ref, dn_ref, xp_ref, xm_ref, w_ref, o_ref, dw_ref, db_ref, ds_ref, xs_ref):
        i = pl.program_id(1)
        d = d_ref[...].astype(F32)
        ds_ref[pl.ds(0, tt), :] = d
        ds_ref[pl.ds(tt, FFN_HALO), :] = jnp.where(i < n_t - 1, dn_ref[...].astype(F32), 0.0)
        xs_ref[pl.ds(0, FFN_HALO), :] = jnp.where(i > 0, xp_ref[...].astype(F32), 0.0)
        x = xm_ref[...].astype(F32)
        xs_ref[pl.ds(FFN_HALO, tt), :] = x
        o_ref[...] = (w_ref[pl.ds(2, 1), :] * d + w_ref[pl.ds(1, 1), :] * ds_ref[pl.ds(1, tt), :]
                      + w_ref[pl.ds(0, 1), :] * ds_ref[pl.ds(2, tt), :]).astype(o_ref.dtype)

        @pl.when(i == 0)
        def _():
            dw_ref[...] = jnp.zeros_like(dw_ref)
            db_ref[...] = jnp.zeros_like(db_ref)

        db_ref[...] += jnp.sum(d, axis=0, keepdims=True)
        dw_ref[pl.ds(0, 1), :] += jnp.sum(d * xs_ref[pl.ds(FFN_HALO - 2, tt), :], axis=0, keepdims=True)
        dw_ref[pl.ds(1, 1), :] += jnp.sum(d * xs_ref[pl.ds(FFN_HALO - 1, tt), :], axis=0, keepdims=True)
        dw_ref[pl.ds(2, 1), :] += jnp.sum(d * x, axis=0, keepdims=True)

    return pl.pallas_call(
        body, name=name, grid=(2 * nv, n_t),
        in_specs=[pl.BlockSpec((None, tt, tc), lambda j, i: (j // nv, i, j % nv)),
                  pl.BlockSpec((None, FFN_HALO, tc),
                               lambda j, i: (j // nv, jnp.minimum((i + 1) * hb, last_halo), j % nv)),
                  pl.BlockSpec((FFN_HALO, tc), lambda j, i: (jnp.maximum(i * hb - 1, 0), j)),
                  pl.BlockSpec((tt, tc), lambda j, i: (i, j)),
                  pl.BlockSpec((FFN_CONV_WIDTH, tc), lambda j, i: (0, j))],
        out_specs=[pl.BlockSpec((tt, tc), lambda j, i: (i, j)),
                   pl.BlockSpec((FFN_CONV_WIDTH, tc), lambda j, i: (0, j)),
                   pl.BlockSpec((1, tc), lambda j, i: (0, j))],
        out_shape=[jax.ShapeDtypeStruct((T, dff2), BF16), jax.ShapeDtypeStruct((FFN_CONV_WIDTH, dff2), F32),
                   jax.ShapeDtypeStruct((1, dff2), F32)],
        scratch_shapes=[pltpu.VMEM((tt + FFN_HALO, tc), F32), pltpu.VMEM((FFN_HALO + tt, tc), F32)],
        compiler_params=_cparams(("arbitrary", "arbitrary")),
    )(dup, dup, up0, up0, w)


def adamw(contribs, w, m, v, *, name):
    L, R, C = w.shape
    assert len(contribs) == L and all(c.shape == (N_DEV, R, C) for c in contribs)
    tr = R
    if R * C > 256 * 1024 and R % 8 == 0:
        tr = 8
        while R % (tr * 2) == 0 and tr * 2 * C <= 256 * 1024:
            tr *= 2
    c1 = 1.0 - ADAM_B1 ** ADAM_STEP
    c2 = 1.0 - ADAM_B2 ** ADAM_STEP

    def body(*refs):
        c_refs = refs[:L]
        w_ref, m_ref, v_ref, g_ref, d_ref, nm_ref, nv_ref = refs[L:]
        layer = pl.program_id(0)
        for lp in range(L):
            @pl.when(layer == lp)
            def _(c_ref=c_refs[lp]):
                g = c_ref[0].astype(F32)
                for s in range(1, N_DEV):
                    g = g + c_ref[s].astype(F32)
                g_ref[...] = g

        g = g_ref[...]
        nm = ADAM_B1 * m_ref[...] + (1.0 - ADAM_B1) * g
        nv = ADAM_B2 * v_ref[...] + (1.0 - ADAM_B2) * (g * g)
        nm_ref[...] = nm
        nv_ref[...] = nv
        d_ref[...] = -ADAM_LR * ((nm / c1) / (jnp.sqrt(nv / c2) + ADAM_EPS) + ADAM_WD * w_ref[...])

    def contrib_spec(lp):
        return pl.BlockSpec((N_DEV, tr, C), lambda l, i: (0, jnp.where(l == lp, i, 0), 0))

    blk = pl.BlockSpec((None, tr, C), lambda l, i: (l, i, 0))
    out = jax.ShapeDtypeStruct((L, R, C), F32)
    return pl.pallas_call(
        body, name=name, grid=(L, R // tr),
        in_specs=[contrib_spec(lp) for lp in range(L)] + [blk, blk, blk],
        out_specs=[blk, blk, blk, blk], out_shape=[out, out, out, out],
        compiler_params=_cparams(("arbitrary", "arbitrary")),
    )(*contribs, w, m, v)


def _my_position():
    x, y, c = lax.axis_index("x"), lax.axis_index("y"), lax.axis_index("c")
    return x, y, c, 4 * x + 2 * y + c


def _peers(x, y, c):
    out = []
    for r in range(1, N_DEV):
        px = 1 - x if r & 4 else x
        py = 1 - y if r & 2 else y
        pc = 1 - c if r & 1 else c
        out.append(((px, py, pc), 4 * px + 2 * py + pc))
    return out


def _run_exchange(plan, n, send_sems, recv_sems, local_sems):
    x, y, c, me = _my_position()
    copies = []
    for t in range(n):
        src, dst = plan(t, None, me)
        own = pltpu.make_async_copy(src, dst, local_sems.at[t])
        own.start()
        copies.append(own)
    remote = []
    for r, (peer, peer_index) in enumerate(_peers(x, y, c)):
        for t in range(n):
            src, dst = plan(t, peer_index, me)
            cp = pltpu.make_async_remote_copy(src_ref=src, dst_ref=dst, send_sem=send_sems.at[t, r],
                                              recv_sem=recv_sems.at[t, r], device_id=peer,
                                              device_id_type=pl.DeviceIdType.MESH)
            cp.start()
            remote.append(cp)
    for cp in remote:
        cp.wait_send()
    for cp in remote:
        cp.wait_recv()
    for cp in copies:
        cp.wait()


def all_gather(blocks, *, name):
    n = len(blocks)

    def body(*refs):
        srcs, outs = refs[:n], refs[n:2 * n]
        send_sems, recv_sems, local_sems = refs[2 * n:]
        _run_exchange(lambda t, peer_index, me: (srcs[t], outs[t].at[me]), n, send_sems, recv_sems, local_sems)

    hbm = pl.BlockSpec(memory_space=pl.ANY)
    return pl.pallas_call(
        body, name=name, in_specs=[hbm] * n, out_specs=[hbm] * n,
        out_shape=[jax.ShapeDtypeStruct((N_DEV, *b.shape), b.dtype) for b in blocks],
        scratch_shapes=[pltpu.SemaphoreType.DMA((n, N_DEV - 1)), pltpu.SemaphoreType.DMA((n, N_DEV - 1)),
                        pltpu.SemaphoreType.DMA((n,))],
    )(*blocks)


_HBM = pl.BlockSpec(memory_space=pltpu.HBM)
_SEM = pl.BlockSpec(memory_space=pltpu.SEMAPHORE)


def _push_copies(kind, src_refs, land_refs, send_sems, recv_sems):
    x, y, c, me = _my_position()
    copies = []
    for r, (peer, peer_index) in enumerate(_peers(x, y, c)):
        for t, (src, land) in enumerate(zip(src_refs, land_refs)):
            copies.append(pltpu.make_async_remote_copy(
                src_ref=src if kind == "gather" else src.at[peer_index], dst_ref=land.at[me],
                send_sem=send_sems.at[t * (N_DEV - 1) + r], recv_sem=recv_sems.at[t * (N_DEV - 1) + r],
                device_id=peer, device_id_type=pl.DeviceIdType.MESH))
    return copies


def push_start(srcs, lands, *, kind, name):
    n = len(srcs)

    def body(*refs):
        send_sems, recv_sems = refs[2 * n:2 * n + 2]
        token = refs[-1]
        for cp in _push_copies(kind, refs[:n], refs[n:2 * n], send_sems, recv_sems):
            cp.start()
        token[...] = jnp.zeros_like(token)

    sems = pltpu.SemaphoreType.DMA((n * (N_DEV - 1),))
    arrays = [*srcs, *lands]
    out = pl.pallas_call(
        body, name=name,
        out_shape=(sems, sems, *[pltpu.HBM(a.shape, a.dtype) for a in arrays],
                   jax.ShapeDtypeStruct((8, LANES), F32)),
        in_specs=[_HBM] * (2 * n),
        out_specs=(_SEM, _SEM, *[_HBM] * (2 * n), pl.BlockSpec(memory_space=pltpu.VMEM)),
        input_output_aliases={i: 2 + i for i in range(2 * n)},
        compiler_params=pltpu.CompilerParams(has_side_effects=pltpu.SideEffectType.DATAFLOW_SIDE_EFFECTING),
    )(*[pltpu.with_memory_space_constraint(a, pltpu.HBM) for a in arrays])
    return out[0], out[1], list(out[2:2 + n]), list(out[2 + n:2 + 2 * n]), out[-1]


def push_wait(send_sems, recv_sems, srcs, lands, after, *, kind, name):
    n = len(srcs)

    def body(*refs):
        for cp in _push_copies(kind, refs[:n], refs[n:2 * n], refs[2 * n], refs[2 * n + 1]):
            cp.wait_send()
            cp.wait_recv()

    arrays = [*srcs, *lands]
    out = pl.pallas_call(
        body, name=name,
        out_shape=tuple(pltpu.HBM(a.shape, a.dtype) for a in arrays),
        in_specs=[_HBM] * (2 * n) + [_SEM, _SEM, pl.BlockSpec(memory_space=pl.ANY)],
        out_specs=tuple([_HBM] * (2 * n)),
        input_output_aliases={i: i for i in range(2 * n)},
        compiler_params=pltpu.CompilerParams(has_side_effects=pltpu.SideEffectType.DATAFLOW_SIDE_EFFECTING),
    )(*arrays, send_sems, recv_sems, after)
    return list(out[n:])


def _own_slot(block, me):
    zone = lax.empty((N_DEV, *block.shape), block.dtype)
    return lax.dynamic_update_slice(zone, block[None], (me,) + (0,) * block.ndim)


WEIGHT_NAMES = ('mix_norm_g', 'mem_norm_g', 'w_in', 'gate_b', 'conv_w', 'conv_b', 'conv_ln_g', 'conv_ln_b',
                'w_conv_out', 'rel_bias', 'w_attn_out', 'w_mem_kv', 'w_mem_out', 'w_o', 'ffn_norm_g', 'w_up',
                'ffn_conv_w', 'ffn_conv_b', 'w_down', 'final_norm_g')
BIG = (('w_in', False), ('w_conv_out', False), ('w_attn_out', False), ('w_mem_out', False), ('w_up', False),
       ('w_mem_kv', True), ('w_o', True), ('w_down', True))


MIN_SLOT_COLS = 512


def _as_matrix(gathered, by_rows):
    n, r, c = gathered.shape
    if by_rows:
        return gathered.reshape(1, n * r, c)
    if c >= MIN_SLOT_COLS:
        return gathered
    return jnp.transpose(gathered, (1, 0, 2)).reshape(1, r, n * c)


def _as_shards(grad, by_rows):
    s, r, c = grad.shape
    if by_rows:
        return grad.reshape(N_DEV, r // N_DEV, c)
    if s == N_DEV:
        return grad
    return jnp.transpose(grad.reshape(r, N_DEV, c // N_DEV), (1, 0, 2))


def _forward_layer(h, mem2, p, l, dims, after=None):
    W = p['gathered'][l]
    row = lambda name: p[name][l:l + 1]
    sv = {'h': h}
    sv['xn'] = rms_fwd(h, row('mix_norm_g'), name="mix_norm")
    sv['proj'] = mm_nn(sv['xn'], W['w_in'], out_dtype=BF16, after=after, name="w_in")
    sv['s'], sv['c'] = conv_fwd(sv['proj'], p['conv_w_full'][l], row('conv_b'), row('conv_ln_g'),
                                row('conv_ln_b'), name="conv_module")
    sv['bias'] = attn_bias(p['rel_pad'][l], name="attn_bias")
    sv['att'] = attn_fwd(sv['proj'], sv['bias'], qcol=dims['qcol'], dattn=dims['DA'], name="chunk_attn")
    sv['mn'] = rms_fwd(mem2, row('mem_norm_g'), name="mem_norm")
    sv['kv'] = mm_nn(sv['mn'], W['w_mem_kv'], out_dtype=BF16, name="w_mem_kv")
    sv['mo'] = memattn_fwd(sv['proj'], sv['kv'], qcol=dims['mcol'], name="mem_attn")
    sv['ys'] = (mm_nn(sv['s'], W['w_conv_out'], out_dtype=BF16, name="w_conv_out"),
                mm_nn(sv['att'], W['w_attn_out'], out_dtype=BF16, name="w_attn_out"),
                mm_nn(sv['mo'], W['w_mem_out'], out_dtype=BF16, name="w_mem_out"))
    sv['merged'] = merge_fwd(sv['proj'], row('gate_b'), sv['ys'], gcol=dims['gcol'], name="merge")
    sv['h1'] = mm_nn(sv['merged'], W['w_o'], out_dtype=F32, residual=h, name="w_o")
    sv['hn'] = rms_fwd(sv['h1'], row('ffn_norm_g'), name="ffn_norm")
    sv['up0'] = mm_nn(sv['hn'], W['w_up'], out_dtype=BF16, name="w_up")
    sv['act'] = ffn_act_fwd(sv['up0'], p['ffn_conv_w_full'][l], row('ffn_conv_b'), name="ffn_act")
    h2 = mm_nn(sv['act'], W['w_down'], out_dtype=F32, residual=sv['h1'], name="w_down")
    return h2, sv


def _backward_layer(dh, dhb, mem2, sv, p, l, dims, after=None):
    W = p['gathered'][l]
    row = lambda name: p[name][l:l + 1]
    g, small = {}, {}
    dact = mm_nt(dhb, W['w_down'], out_dtype=BF16, after=after, name="d_act")
    g['w_down'] = mm_tn(sv['act'], dhb, slots=1, out_dtype=BF16, name="g_w_down")
    dup = ffn_act_bwd(dact, sv['up0'], p['ffn_conv_w_full'][l], row('ffn_conv_b'), name="ffn_act_bwd")
    dup0, small['ffn_conv_w'], small['ffn_conv_b'] = ffn_conv_bwd(dup, sv['up0'], p['ffn_conv_w_full'][l],
                                                                  name="ffn_conv_bwd")
    dhn = mm_nt(dup0, W['w_up'], out_dtype=BF16, name="d_hn")
    g['w_up'] = mm_tn(sv['hn'], dup0, slots=N_DEV, out_dtype=BF16, name="g_w_up")
    dh1, dh1b, small['ffn_norm_g'] = rms_bwd(sv['h1'], row('ffn_norm_g'), dhn, dh, name="ffn_norm_bwd")
    dmerged = mm_nt(dh1b, W['w_o'], out_dtype=BF16, name="d_merged")
    g['w_o'] = mm_tn(sv['merged'], dh1b, slots=1, out_dtype=BF16, name="g_w_o")
    dproj, dy, small['gate_b'] = merge_bwd(dmerged, sv['proj'], row('gate_b'), sv['ys'], gcol=dims['gcol'],
                                           name="merge_bwd")
    ds = mm_nt(dy, W['w_conv_out'], lead=0, out_dtype=BF16, name="d_conv_out")
    g['w_conv_out'] = mm_tn(sv['s'], dy, lead=0, slots=W['w_conv_out'].shape[0], out_dtype=BF16,
                            name="g_w_conv_out")
    dc, small['conv_ln_g'], small['conv_ln_b'] = conv_bwd_ln(ds, sv['c'], row('conv_ln_g'), row('conv_ln_b'),
                                                             name="conv_ln_bwd")
    dproj, small['conv_w'], small['conv_b'] = conv_bwd_taps(dc, sv['proj'], p['conv_w_full'][l], dproj,
                                                            name="conv_taps_bwd")
    datt = mm_nt(dy, W['w_attn_out'], lead=1, out_dtype=BF16, name="d_attn_out")
    g['w_attn_out'] = mm_tn(sv['att'], dy, lead=1, slots=W['w_attn_out'].shape[0], out_dtype=BF16,
                            name="g_w_attn_out")
    dproj, dkw, dvw, dbias = attn_bwd(sv['proj'], sv['bias'], datt, dproj, qcol=dims['qcol'], dattn=dims['DA'],
                                      name="chunk_attn_bwd")
    dproj = window_sum(dkw, dproj, col=dims['qcol'] + dims['DA'], name="dk_windows")
    dproj = window_sum(dvw, dproj, col=dims['qcol'] + 2 * dims['DA'], name="dv_windows")
    small['rel_bias'] = attn_bias_bwd(dbias, name="attn_bias_bwd")[:, :2 * MAX_REL + 1]
    dmo = mm_nt(dy, W['w_mem_out'], lead=2, out_dtype=BF16, name="d_mem_out")
    g['w_mem_out'] = mm_tn(sv['mo'], dy, lead=2, slots=W['w_mem_out'].shape[0], out_dtype=BF16,
                           name="g_w_mem_out")
    dproj, dkv = memattn_bwd(sv['proj'], sv['kv'], dmo, dproj, qcol=dims['mcol'], name="mem_attn_bwd")
    dkvb = dkv.astype(BF16)
    g['w_mem_kv'] = mm_tn(sv['mn'], dkvb, slots=1, out_dtype=BF16, name="g_w_mem_kv")
    dmn = mm_nt(dkvb, W['w_mem_kv'], out_dtype=BF16, name="d_mem_norm")
    _, _, small['mem_norm_g'] = rms_bwd(mem2, row('mem_norm_g'), dmn, jnp.zeros(mem2.shape, F32),
                                        name="mem_norm_bwd")
    dxn = mm_nt(dproj, W['w_in'], out_dtype=BF16, name="d_xn")
    g['w_in'] = mm_tn(sv['xn'], dproj, slots=N_DEV, out_dtype=BF16, name="g_w_in")
    dh0, dh0b, small['mix_norm_g'] = rms_bwd(sv['h'], row('mix_norm_g'), dxn, dh1, name="mix_norm_bwd")
    return dh0, dh0b, g, small


def kernel(x, mem, mix_norm_g, mem_norm_g, w_in, gate_b, conv_w, conv_b, conv_ln_g, conv_ln_b, w_conv_out, rel_bias, w_attn_out, w_mem_kv, w_mem_out, w_o, ffn_norm_g, w_up, ffn_conv_w, ffn_conv_b, w_down, final_norm_g, loss_target, m_mix_norm_g, m_mem_norm_g, m_w_in, m_gate_b, m_conv_w, m_conv_b, m_conv_ln_g, m_conv_ln_b, m_w_conv_out, m_rel_bias, m_w_attn_out, m_w_mem_kv, m_w_mem_out, m_w_o, m_ffn_norm_g, m_w_up, m_ffn_conv_w, m_ffn_conv_b, m_w_down, m_final_norm_g, v_mix_norm_g, v_mem_norm_g, v_w_in, v_gate_b, v_conv_w, v_conv_b, v_conv_ln_g, v_conv_ln_b, v_w_conv_out, v_rel_bias, v_w_attn_out, v_w_mem_kv, v_w_mem_out, v_w_o, v_ffn_norm_g, v_w_up, v_ffn_conv_w, v_ffn_conv_b, v_w_down, v_final_norm_g):
    env = locals()
    w = {n: env[n] for n in WEIGHT_NAMES}
    mom = {n: env['m_' + n] for n in WEIGHT_NAMES}
    var = {n: env['v_' + n] for n in WEIGHT_NAMES}
    T, D = x.shape[-2:]
    L = w_in.shape[0]
    DC = conv_b.shape[-1]
    DA = N_DEV * w_attn_out.shape[-1] // 2
    dims = {'DA': DA, 'qcol': 2 * DC, 'mcol': 2 * DC + 3 * DA, 'gcol': 2 * DC + 3 * DA + D // 2}
    x2, mem2, target = x.reshape(T, D), mem.reshape(-1, D), loss_target.reshape(T, D)
    me = 4 * lax.axis_index("x") + 2 * lax.axis_index("y") + lax.axis_index("c")

    p = dict(w)

    def start_gather(l, first):
        blocks = [w[n][l].astype(BF16) for n, _ in BIG]
        first, blocks = lax.optimization_barrier((first, blocks))
        return push_start(blocks, [_own_slot(b, me) for b in blocks], kind="gather", name="gather_start_%d" % l)

    def finish_gather(l, started, after):
        got = push_wait(*started[:4], after, kind="gather", name="gather_wait_%d" % l)
        return {n: _as_matrix(a, by_rows) for (n, by_rows), a in zip(BIG, got)}

    taps, ffn_taps = all_gather([conv_w, ffn_conv_w], name="gather_taps")
    started = start_gather(0, taps)
    p['conv_w_full'] = jnp.moveaxis(taps, 0, 2).reshape(L, conv_w.shape[1], -1)
    p['ffn_conv_w_full'] = jnp.moveaxis(ffn_taps, 0, 2).reshape(L, ffn_conv_w.shape[1], -1)
    p['rel_pad'] = jnp.pad(rel_bias, ((0, 0), (0, 0), (0, REL_PAD - rel_bias.shape[-1])))
    p['gathered'] = [finish_gather(0, started, started[4])] + [None] * (L - 1)

    h = x2
    saved = []
    for l in range(L):
        started = start_gather(l + 1, p['gathered'][l]['w_in']) if l + 1 < L else None
        h, sv = _forward_layer(h, mem2, p, l, dims, after=started[4] if started else None)
        saved.append(sv)
        if started:
            p['gathered'][l + 1] = finish_gather(l + 1, started, h)
    loss_part, dh, dhb, d_final = loss_head(h, final_norm_g.reshape(1, D), target, name="loss_head")
    loss = lax.psum(loss_part[0, 0], ("x", "y", "c"))

    small, landed, started = [None] * L, [None] * L, None
    for l in reversed(range(L)):
        dh, dhb, g, small[l] = _backward_layer(dh, dhb, mem2, saved[l], p, l, dims,
                                               after=started[4] if started else None)
        if started:
            landed[l + 1] = push_wait(*started[:4], dh, kind="scatter", name="scatter_wait_%d" % (l + 1))
        grads = [_as_shards(g[n], by_rows) for n, by_rows in BIG]
        zones = [_own_slot(lax.dynamic_index_in_dim(a, me, 0, keepdims=False), me) for a in grads]
        started = push_start(grads, zones, kind="scatter", name="scatter_start_%d" % l)

    big_names = [n for n, _ in BIG]
    small_names = [n for n in WEIGHT_NAMES if n not in big_names and n != 'final_norm_g']
    stacked = [jnp.stack([small[l][n] for l in range(L)]) for n in small_names] + [d_final]
    got = dict(zip(small_names + ['final_norm_g'], all_gather(stacked, name="gather_small_grads")))
    landed[0] = push_wait(*started[:4], got['final_norm_g'], kind="scatter", name="scatter_wait_0")
    for n in ('conv_w', 'ffn_conv_w'):
        cs = w[n].shape[-1]
        got[n] = lax.dynamic_slice_in_dim(got[n], me * cs, cs, axis=3)
    contribs = {n: [landed[l][t] for l in range(L)] for t, n in enumerate(big_names)}
    for n in small_names + ['final_norm_g']:
        contribs[n] = [got[n].reshape(N_DEV, -1, w[n].shape[-1])]

    outs = {}
    for n in WEIGHT_NAMES:
        shp = (len(contribs[n]),) + contribs[n][0].shape[1:]
        res = adamw(contribs[n], w[n].reshape(shp), mom[n].reshape(shp), var[n].reshape(shp), name="adamw_" + n)
        outs[n] = [r.reshape(w[n].shape) for r in res]
    return (loss, dh.reshape(x.shape),
            *[outs[n][0] for n in WEIGHT_NAMES], *[outs[n][1] for n in WEIGHT_NAMES],
            *[outs[n][2] for n in WEIGHT_NAMES], *[outs[n][3] for n in WEIGHT_NAMES])
```

```python
import functools

import numpy as np
import jax
import jax.numpy as jnp
from jax import lax
from jax.experimental import pallas as pl
from jax.experimental.pallas import tpu as pltpu

F32 = jnp.float32
BF16 = jnp.bfloat16

CHUNK = 64
LEFT_CHUNKS = 8
MAX_REL = 256
N_MEM_HEADS = 4
ATTN_HEAD_DIM = 64
CONV_WIDTH = 31
FFN_CONV_WIDTH = 3
EPS = 1e-6
NEG_INF = -1e30
ADAM_LR = 0.001
ADAM_B1 = 0.9
ADAM_B2 = 0.999
ADAM_EPS = 1e-08
ADAM_WD = 0.01
ADAM_STEP = 10
N_DEV = 8

VMEM_LIMIT_BYTES = 56 * 1024 * 1024
LANES = 128


def _cparams(sem=None):
    return pltpu.CompilerParams(dimension_semantics=sem, vmem_limit_bytes=VMEM_LIMIT_BYTES)


def _tile(n, want):
    if n <= want:
        return n
    t = (want // LANES) * LANES
    while t >= LANES:
        if n % t == 0:
            return t if 4 * t >= want or n > 2 * want else n
        t -= LANES
    return n


MM_TM, MM_TN, MM_TK = 1024, 1536, 2048


def _matmul(name, a, b, *, dims, grid, nk, a_spec, b_spec, out_spec, out_shape, acc_shape, residual=None,
            after=None):
    n_in = 2 + (residual is not None) + (after is not None)

    def body(*refs):
        a_ref, b_ref = refs[:2]
        o_ref = refs[n_in]

        def finish(r):
            if residual is not None:
                r = r + refs[2][...]
            o_ref[...] = r.astype(o_ref.dtype)

        part = lax.dot_general(a_ref[...], b_ref[...], (dims, ((), ())), preferred_element_type=F32)
        if nk == 1:
            finish(part)
            return
        acc = refs[-1]
        k = pl.program_id(2)

        @pl.when(k == 0)
        def _():
            acc[...] = part

        @pl.when((k > 0) & (k < nk - 1))
        def _():
            acc[...] += part

        @pl.when(k == nk - 1)
        def _():
            finish(acc[...] + part)

    in_specs, args = [a_spec, b_spec], [a, b]
    if residual is not None:
        in_specs.append(out_spec)
        args.append(residual)
    if after is not None:
        in_specs.append(pl.BlockSpec(memory_space=pl.ANY))
        args.append(after)
    return pl.pallas_call(
        body, name=name, grid=grid, in_specs=in_specs, out_specs=out_spec, out_shape=out_shape,
        scratch_shapes=[pltpu.VMEM(acc_shape, F32)] if nk > 1 else [],
        compiler_params=_cparams(("parallel", "parallel", "arbitrary")),
    )(*args)


def _lead_spec(arr, lead, block, index):
    if lead is None:
        assert arr.ndim == 2
        return pl.BlockSpec(block, index)
    assert arr.ndim == 3
    if lead == "planes":
        per_plane = arr.shape[2] // block[1]
        assert arr.shape[2] % block[1] == 0

        def planes_index(i, j, k):
            r, c = index(i, j, k)
            return c // per_plane, r, c % per_plane

        return pl.BlockSpec((None, *block), planes_index)
    return pl.BlockSpec((None, *block), lambda i, j, k: (lead, *index(i, j, k)))


def _matrix_shape(arr, lead):
    return (arr.shape[1], arr.shape[0] * arr.shape[2]) if lead == "planes" else arr.shape[-2:]


def mm_nn(a, w, *, out_dtype, residual=None, after=None, tm=MM_TM, tn=MM_TN, tk=MM_TK, name):
    M, K = a.shape
    S, K2, Ns = w.shape
    assert K == K2
    tm, tn, tk = _tile(M, tm), _tile(Ns, tn), _tile(K, tk)
    nb = Ns // tn
    return _matmul(
        name, a, w, dims=((1,), (0,)), grid=(M // tm, S * nb, K // tk), nk=K // tk,
        a_spec=pl.BlockSpec((tm, tk), lambda i, j, k: (i, k)),
        b_spec=pl.BlockSpec((None, tk, tn), lambda i, j, k: (j // nb, k, j % nb)),
        out_spec=pl.BlockSpec((tm, tn), lambda i, j, k: (i, j)),
        out_shape=jax.ShapeDtypeStruct((M, S * Ns), out_dtype), acc_shape=(tm, tn), residual=residual,
        after=after)


def mm_nt(a, w, *, out_dtype, lead=None, after=None, tm=MM_TM, tn=MM_TN, tk=MM_TK, name):
    M, N = _matrix_shape(a, lead)
    S, K, Ns = w.shape
    assert N == S * Ns
    tm, tn, tk = _tile(M, tm), _tile(K, tn), _tile(Ns, tk)
    nb = Ns // tk
    return _matmul(
        name, a, w, dims=((1,), (1,)), grid=(M // tm, K // tn, S * nb), nk=S * nb,
        a_spec=_lead_spec(a, lead, (tm, tk), lambda i, j, k: (i, k)),
        b_spec=pl.BlockSpec((None, tn, tk), lambda i, j, k: (k // nb, j, k % nb)),
        out_spec=pl.BlockSpec((tm, tn), lambda i, j, k: (i, j)),
        out_shape=jax.ShapeDtypeStruct((M, K), out_dtype), acc_shape=(tm, tn), after=after)


def mm_tn(a, b, *, slots, out_dtype, lead=None, tm=MM_TM, tn=MM_TN, tk=MM_TK, name):
    T, K = a.shape
    T2, N = _matrix_shape(b, lead)
    assert T == T2 and N % slots == 0
    Ns = N // slots
    tm, tn, tk = _tile(K, tm), _tile(Ns, tn), _tile(T, tk)
    nb = Ns // tn
    return _matmul(
        name, a, b, dims=((0,), (0,)), grid=(K // tm, slots * nb, T // tk), nk=T // tk,
        a_spec=pl.BlockSpec((tk, tm), lambda i, j, k: (k, i)),
        b_spec=_lead_spec(b, lead, (tk, tn), lambda i, j, k: (k, j)),
        out_spec=pl.BlockSpec((None, tm, tn), lambda i, j, k: (j // nb, i, j % nb)),
        out_shape=jax.ShapeDtypeStruct((slots, K, Ns), out_dtype), acc_shape=(tm, tn))


def rms_fwd(h, g, *, tr=256, name):
    T, D = h.shape
    tr = min(tr, T)

    def body(h_ref, g_ref, o_ref):
        x = h_ref[...]
        r = lax.rsqrt(jnp.mean(x * x, axis=-1, keepdims=True) + EPS)
        o_ref[...] = (x * r * g_ref[...]).astype(o_ref.dtype)

    return pl.pallas_call(
        body, name=name, grid=(T // tr,),
        in_specs=[pl.BlockSpec((tr, D), lambda i: (i, 0)), pl.BlockSpec((1, D), lambda i: (0, 0))],
        out_specs=pl.BlockSpec((tr, D), lambda i: (i, 0)),
        out_shape=jax.ShapeDtypeStruct((T, D), BF16),
        compiler_params=_cparams(("parallel",)),
    )(h, g)


def rms_bwd(h, g, dxn, dres, *, tr=256, name):
    T, D = h.shape
    tr = min(tr, T)

    def body(h_ref, g_ref, d_ref, r_ref, dh_ref, dhb_ref, dg_ref):
        i = pl.program_id(0)
        x = h_ref[...]
        r = lax.rsqrt(jnp.mean(x * x, axis=-1, keepdims=True) + EPS)
        xh = x * r
        d = d_ref[...].astype(F32)

        @pl.when(i == 0)
        def _():
            dg_ref[...] = jnp.zeros_like(dg_ref)

        dg_ref[...] += jnp.sum(d * xh, axis=0, keepdims=True)
        dxh = d * g_ref[...]
        dh = r * (dxh - xh * jnp.mean(dxh * xh, axis=-1, keepdims=True)) + r_ref[...]
        dh_ref[...] = dh
        dhb_ref[...] = dh.astype(BF16)

    row = pl.BlockSpec((tr, D), lambda i: (i, 0))
    vec = pl.BlockSpec((1, D), lambda i: (0, 0))
    return pl.pallas_call(
        body, name=name, grid=(T // tr,),
        in_specs=[row, vec, row, row],
        out_specs=[row, row, vec],
        out_shape=[jax.ShapeDtypeStruct((T, D), F32), jax.ShapeDtypeStruct((T, D), BF16),
                   jax.ShapeDtypeStruct((1, D), F32)],
        compiler_params=_cparams(("arbitrary",)),
    )(h, g, dxn, dres)


def loss_head(h, g, target, *, tr=256, name):
    T, D = h.shape
    tr = min(tr, T)

    def body(h_ref, g_ref, t_ref, loss_ref, dh_ref, dhb_ref, dg_ref):
        i = pl.program_id(0)
        x = h_ref[...]
        r = lax.rsqrt(jnp.mean(x * x, axis=-1, keepdims=True) + EPS)
        xh = x * r
        gg = g_ref[...]
        err = xh * gg - t_ref[...]

        @pl.when(i == 0)
        def _():
            dg_ref[...] = jnp.zeros_like(dg_ref)
            loss_ref[...] = jnp.zeros_like(loss_ref)

        loss_ref[...] += 0.5 * jnp.sum(jnp.mean(err * err, axis=-1, keepdims=True))
        dy = err * (1.0 / D)
        dg_ref[...] += jnp.sum(dy * xh, axis=0, keepdims=True)
        dxh = dy * gg
        dh = r * (dxh - xh * jnp.mean(dxh * xh, axis=-1, keepdims=True))
        dh_ref[...] = dh
        dhb_ref[...] = dh.astype(BF16)

    row = pl.BlockSpec((tr, D), lambda i: (i, 0))
    vec = pl.BlockSpec((1, D), lambda i: (0, 0))
    return pl.pallas_call(
        body, name=name, grid=(T // tr,),
        in_specs=[row, vec, row],
        out_specs=[pl.BlockSpec((8, LANES), lambda i: (0, 0)), row, row, vec],
        out_shape=[jax.ShapeDtypeStruct((8, LANES), F32), jax.ShapeDtypeStruct((T, D), F32),
                   jax.ShapeDtypeStruct((T, D), BF16), jax.ShapeDtypeStruct((1, D), F32)],
        compiler_params=_cparams(("arbitrary",)),
    )(h, g, target)


CONV_HALO = 32


def _glu_ext(prev_ref, main_ref, hs_ref, i, dc, tt):
    up = prev_ref[...].astype(F32)
    hp = up[:, :dc] * jax.nn.sigmoid(up[:, dc:])
    hs_ref[pl.ds(0, CONV_HALO), :] = jnp.where(i > 0, hp, 0.0)
    um = main_ref[...].astype(F32)
    sig = jax.nn.sigmoid(um[:, dc:])
    hs_ref[pl.ds(CONV_HALO, tt), :] = um[:, :dc] * sig
    return um[:, :dc], sig


def conv_fwd(proj, conv_w, conv_b, ln_g, ln_b, *, tt=256, name):
    T = proj.shape[0]
    W, DC = conv_w.shape
    tt = min(tt, T)
    hb = tt // CONV_HALO

    def body(prev_ref, main_ref, w_ref, b_ref, g_ref, bb_ref, s_ref, c_ref, hs_ref):
        i = pl.program_id(0)
        _glu_ext(prev_ref, main_ref, hs_ref, i, DC, tt)
        c = jnp.zeros((tt, DC), F32) + b_ref[...]
        for k in range(W):
            c = c + w_ref[pl.ds(k, 1), :] * hs_ref[pl.ds(CONV_HALO - (W - 1) + k, tt), :]
        c_ref[...] = c
        mu = jnp.mean(c, axis=-1, keepdims=True)
        xc = c - mu
        var = jnp.mean(xc * xc, axis=-1, keepdims=True)
        y = xc * lax.rsqrt(var + EPS) * g_ref[...] + bb_ref[...]
        s_ref[...] = (y * jax.nn.sigmoid(y)).astype(s_ref.dtype)

    vec = pl.BlockSpec((1, DC), lambda i: (0, 0))
    return pl.pallas_call(
        body, name=name, grid=(T // tt,),
        in_specs=[pl.BlockSpec((CONV_HALO, 2 * DC), lambda i: (jnp.maximum(i * hb - 1, 0), 0)),
                  pl.BlockSpec((tt, 2 * DC), lambda i: (i, 0)),
                  pl.BlockSpec((W, DC), lambda i: (0, 0)), vec, vec, vec],
        out_specs=[pl.BlockSpec((tt, DC), lambda i: (i, 0)), pl.BlockSpec((tt, DC), lambda i: (i, 0))],
        out_shape=[jax.ShapeDtypeStruct((T, DC), BF16), jax.ShapeDtypeStruct((T, DC), F32)],
        scratch_shapes=[pltpu.VMEM((tt + CONV_HALO, DC), F32)],
        compiler_params=_cparams(("parallel",)),
    )(proj, proj, conv_w, conv_b, ln_g, ln_b)


def conv_bwd_ln(ds, c, ln_g, ln_b, *, tt=256, name):
    T, DC = c.shape
    tt = min(tt, T)

    def body(ds_ref, c_ref, g_ref, bb_ref, dc_ref, dg_ref, db_ref):
        i = pl.program_id(0)
        c = c_ref[...]
        mu = jnp.mean(c, axis=-1, keepdims=True)
        xc = c - mu
        var = jnp.mean(xc * xc, axis=-1, keepdims=True)
        rstd = lax.rsqrt(var + EPS)
        xh = xc * rstd
        y = xh * g_ref[...] + bb_ref[...]
        sg = jax.nn.sigmoid(y)
        dy = ds_ref[...].astype(F32) * (sg * (1.0 + y * (1.0 - sg)))

        @pl.when(i == 0)
        def _():
            dg_ref[...] = jnp.zeros_like(dg_ref)
            db_ref[...] = jnp.zeros_like(db_ref)

        db_ref[...] += jnp.sum(dy, axis=0, keepdims=True)
        dg_ref[...] += jnp.sum(dy * xh, axis=0, keepdims=True)
        dxh = dy * g_ref[...]
        dc_ref[...] = rstd * (dxh - jnp.mean(dxh, axis=-1, keepdims=True)
                              - xh * jnp.mean(dxh * xh, axis=-1, keepdims=True))

    row = pl.BlockSpec((tt, DC), lambda i: (i, 0))
    vec = pl.BlockSpec((1, DC), lambda i: (0, 0))
    return pl.pallas_call(
        body, name=name, grid=(T // tt,),
        in_specs=[row, row, vec, vec], out_specs=[row, vec, vec],
        out_shape=[jax.ShapeDtypeStruct((T, DC), F32), jax.ShapeDtypeStruct((1, DC), F32),
                   jax.ShapeDtypeStruct((1, DC), F32)],
        compiler_params=_cparams(("arbitrary",)),
    )(ds, c, ln_g, ln_b)


def conv_bwd_taps(dc, proj, conv_w, dproj, *, tt=256, name):
    T, DC = dc.shape
    W = conv_w.shape[0]
    tt = min(tt, T)
    hb = tt // CONV_HALO
    n_t = T // tt
    last_halo = T // CONV_HALO - 1

    def body(dc_ref, dcn_ref, prev_ref, main_ref, w_ref, _, dp_ref, dw_ref, db_ref, hs_ref, ds_ref):
        i = pl.program_id(0)
        a, sig = _glu_ext(prev_ref, main_ref, hs_ref, i, DC, tt)
        d = dc_ref[...]
        ds_ref[pl.ds(0, tt), :] = d
        ds_ref[pl.ds(tt, CONV_HALO), :] = jnp.where(i < n_t - 1, dcn_ref[...], 0.0)

        @pl.when(i == 0)
        def _():
            dw_ref[...] = jnp.zeros_like(dw_ref)
            db_ref[...] = jnp.zeros_like(db_ref)

        db_ref[...] += jnp.sum(d, axis=0, keepdims=True)
        dh = jnp.zeros((tt, DC), F32)
        for k in range(W):
            dh = dh + w_ref[pl.ds(k, 1), :] * ds_ref[pl.ds(W - 1 - k, tt), :]
            dw_ref[pl.ds(k, 1), :] += jnp.sum(d * hs_ref[pl.ds(CONV_HALO - (W - 1) + k, tt), :],
                                              axis=0, keepdims=True)
        da = dh * sig
        dgt = dh * a * sig * (1.0 - sig)
        dp_ref[...] = jnp.concatenate([da, dgt], axis=1).astype(dp_ref.dtype)

    return pl.pallas_call(
        body, name=name, grid=(n_t,),
        in_specs=[pl.BlockSpec((tt, DC), lambda i: (i, 0)),
                  pl.BlockSpec((CONV_HALO, DC), lambda i: (jnp.minimum((i + 1) * hb, last_halo), 0)),
                  pl.BlockSpec((CONV_HALO, 2 * DC), lambda i: (jnp.maximum(i * hb - 1, 0), 0)),
                  pl.BlockSpec((tt, 2 * DC), lambda i: (i, 0)),
                  pl.BlockSpec((W, DC), lambda i: (0, 0)),
                  pl.BlockSpec(memory_space=pl.ANY)],
        out_specs=[pl.BlockSpec((tt, 2 * DC), lambda i: (i, 0)),
                   pl.BlockSpec((W, DC), lambda i: (0, 0)), pl.BlockSpec((1, DC), lambda i: (0, 0))],
        out_shape=[jax.ShapeDtypeStruct(dproj.shape, dproj.dtype), jax.ShapeDtypeStruct((W, DC), F32),
                   jax.ShapeDtypeStruct((1, DC), F32)],
        scratch_shapes=[pltpu.VMEM((tt + CONV_HALO, DC), F32), pltpu.VMEM((tt + CONV_HALO, DC), F32)],
        input_output_aliases={5: 0},
        compiler_params=_cparams(("arbitrary",)),
    )(dc, dc, proj, proj, conv_w, dproj)


ATT_TQ = 256
ATT_NB = 1 + (LEFT_CHUNKS * CHUNK) // ATT_TQ
ATT_WIN = ATT_NB * ATT_TQ
ATT_PERIOD = 1024
REL_PAD = 640


def _rel_onehot():
    m = lax.broadcasted_iota(jnp.int32, (REL_PAD, ATT_PERIOD), 1)
    r = lax.broadcasted_iota(jnp.int32, (REL_PAD, ATT_PERIOD), 0)
    qk = jnp.where(m < ATT_PERIOD - ATT_TQ, -m, ATT_PERIOD - m)
    idx = jnp.clip(LEFT_CHUNKS * CHUNK + qk, -MAX_REL, MAX_REL) + MAX_REL
    return (idx == r).astype(F32)


def _band_valid():
    q = lax.broadcasted_iota(jnp.int32, (ATT_TQ, ATT_WIN), 0)
    k = lax.broadcasted_iota(jnp.int32, (ATT_TQ, ATT_WIN), 1)
    j = k // CHUNK - q // CHUNK
    return (j >= 0) & (j <= LEFT_CHUNKS)


def attn_bias(rel_pad, *, name):
    H = rel_pad.shape[0]

    def body(rb_ref, o_ref):
        g = jnp.dot(rb_ref[...], _rel_onehot(), preferred_element_type=F32, precision=lax.Precision.HIGHEST)
        valid = _band_valid()
        for h in range(H):
            row = jnp.broadcast_to(g[h:h + 1, :], (ATT_TQ, ATT_PERIOD))
            t = pltpu.roll(row, 0, 1, stride=1, stride_axis=0)
            o_ref[h] = jnp.where(valid, t[:, :ATT_WIN], NEG_INF)

    return pl.pallas_call(
        body, name=name,
        in_specs=[pl.BlockSpec(memory_space=pltpu.VMEM)], out_specs=pl.BlockSpec(memory_space=pltpu.VMEM),
        out_shape=jax.ShapeDtypeStruct((H, ATT_TQ, ATT_WIN), F32),
        compiler_params=_cparams(),
    )(rel_pad)


def attn_bias_bwd(dbias, *, name):
    H = dbias.shape[0]
    band = (LEFT_CHUNKS + 1) * CHUNK

    def body(d_ref, o_ref, acc_ref):
        r = lax.broadcasted_iota(jnp.int32, (CHUNK, CHUNK), 0)
        c = lax.broadcasted_iota(jnp.int32, (CHUNK, CHUNK), 1)
        rev = (r + c == CHUNK - 1).astype(F32)
        pad = jnp.zeros((CHUNK, ATT_PERIOD - ATT_WIN), F32)
        for h in range(H):
            tab = d_ref[h, pl.ds(0, CHUNK), :]
            for cq in range(1, ATT_TQ // CHUNK):
                tab = tab + pltpu.roll(d_ref[h, pl.ds(cq * CHUNK, CHUNK), :], ATT_WIN - cq * CHUNK, 1)
            t = jnp.dot(rev, tab, preferred_element_type=F32, precision=lax.Precision.HIGHEST)
            u = pltpu.roll(jnp.concatenate([t, pad], axis=1), 0, 1, stride=1, stride_axis=0)
            acc_ref[pl.ds(h, 1), :] = jnp.sum(u, axis=0, keepdims=True)
        j = lax.broadcasted_iota(jnp.int32, (REL_PAD, ATT_PERIOD), 1)
        rr = lax.broadcasted_iota(jnp.int32, (REL_PAD, ATT_PERIOD), 0)
        idx = jnp.clip(LEFT_CHUNKS * CHUNK + CHUNK - 1 - j, -MAX_REL, MAX_REL) + MAX_REL
        onehot = ((idx == rr) & (j < band + CHUNK - 1)).astype(F32)
        o_ref[...] = lax.dot_general(acc_ref[...], onehot, (((1,), (1,)), ((), ())),
                                     preferred_element_type=F32, precision=lax.Precision.HIGHEST)

    return pl.pallas_call(
        body, name=name,
        in_specs=[pl.BlockSpec(memory_space=pltpu.VMEM)], out_specs=pl.BlockSpec(memory_space=pltpu.VMEM),
        out_shape=jax.ShapeDtypeStruct((H, REL_PAD), F32),
        scratch_shapes=[pltpu.VMEM((H, ATT_PERIOD), F32)],
        compiler_params=_cparams(),
    )(dbias)


def _attn_specs(T, qcol, dattn):
    dn = dattn // LANES

    def kv(col0, back):
        return pl.BlockSpec((ATT_TQ, LANES), lambda hp, i: (jnp.maximum(i - back, 0), col0 // LANES + hp))

    q = pl.BlockSpec((ATT_TQ, LANES), lambda hp, i: (i, qcol // LANES + hp))
    ks = [kv(qcol + dattn, ATT_NB - 1 - b) for b in range(ATT_NB)]
    vs = [kv(qcol + 2 * dattn, ATT_NB - 1 - b) for b in range(ATT_NB)]
    return q, ks, vs


def _attn_scores(q, kw, bias, i):
    s = lax.dot_general(q, kw, (((1,), (1,)), ((), ())), preferred_element_type=F32)
    s = s * (ATTN_HEAD_DIM ** -0.5) + bias
    col = lax.broadcasted_iota(jnp.int32, s.shape, 1)
    s = jnp.where(col // ATT_TQ + i >= ATT_NB - 1, s, NEG_INF)
    s = s - jnp.max(s, axis=-1, keepdims=True)
    p = jnp.exp(s)
    return p * (1.0 / jnp.sum(p, axis=-1, keepdims=True))


def attn_fwd(proj, bias, *, qcol, dattn, name):
    T = proj.shape[0]
    HP = dattn // LANES
    q_spec, k_specs, v_specs = _attn_specs(T, qcol, dattn)

    def body(*refs):
        q_ref = refs[0]
        k_refs = refs[1:1 + ATT_NB]
        v_refs = refs[1 + ATT_NB:1 + 2 * ATT_NB]
        b_ref, o_ref = refs[1 + 2 * ATT_NB:]
        i = pl.program_id(1)
        lane = lax.broadcasted_iota(jnp.int32, (ATT_TQ, LANES), 1)
        kw = jnp.concatenate([r[...] for r in k_refs], axis=0)
        vw = jnp.concatenate([r[...] for r in v_refs], axis=0)
        q = q_ref[...]
        out = jnp.zeros((ATT_TQ, LANES), F32)
        for hh in range(2):
            mine = (lane // ATTN_HEAD_DIM) == hh
            p = _attn_scores(jnp.where(mine, q, jnp.zeros_like(q)), kw, b_ref[hh], i)
            o = jnp.dot(p.astype(BF16), vw, preferred_element_type=F32)
            out = jnp.where(mine, o, out)
        o_ref[...] = out.astype(o_ref.dtype)

    return pl.pallas_call(
        body, name=name, grid=(HP, T // ATT_TQ),
        in_specs=[q_spec, *k_specs, *v_specs,
                  pl.BlockSpec((2, ATT_TQ, ATT_WIN), lambda hp, i: (hp, 0, 0))],
        out_specs=pl.BlockSpec((ATT_TQ, LANES), lambda hp, i: (i, hp)),
        out_shape=jax.ShapeDtypeStruct((T, dattn), BF16),
        compiler_params=_cparams(("parallel", "parallel")),
    )(*([proj] * (1 + 2 * ATT_NB)), bias)


def attn_bwd(proj, bias, dout, dproj, *, qcol, dattn, name):
    T = proj.shape[0]
    HP = dattn // LANES
    nq = T // ATT_TQ
    q_spec, k_specs, v_specs = _attn_specs(T, qcol, dattn)

    def body(*refs):
        q_ref = refs[0]
        k_refs = refs[1:1 + ATT_NB]
        v_refs = refs[1 + ATT_NB:1 + 2 * ATT_NB]
        b_ref, do_ref, _, dq_ref, dk_ref, dv_ref, db_ref = refs[1 + 2 * ATT_NB:]
        i = pl.program_id(1)
        lane = lax.broadcasted_iota(jnp.int32, (ATT_TQ, LANES), 1)
        kw = jnp.concatenate([r[...] for r in k_refs], axis=0)
        vw = jnp.concatenate([r[...] for r in v_refs], axis=0)
        q = q_ref[...]
        do = do_ref[...]
        dq = jnp.zeros((ATT_TQ, LANES), F32)
        dk = jnp.zeros((ATT_WIN, LANES), F32)
        dv = jnp.zeros((ATT_WIN, LANES), F32)

        @pl.when(i == 0)
        def _():
            db_ref[...] = jnp.zeros_like(db_ref)

        for hh in range(2):
            mine = (lane // ATTN_HEAD_DIM) == hh
            qh = jnp.where(mine, q, jnp.zeros_like(q))
            doh = jnp.where(mine, do, jnp.zeros_like(do))
            p = _attn_scores(qh, kw, b_ref[hh], i)
            dp = lax.dot_general(doh, vw, (((1,), (1,)), ((), ())), preferred_element_type=F32)
            dv = dv + lax.dot_general(p.astype(BF16), doh, (((0,), (0,)), ((), ())), preferred_element_type=F32)
            ds = p * (dp - jnp.sum(dp * p, axis=-1, keepdims=True))
            db_ref[hh] += ds
            dsb = (ds * (ATTN_HEAD_DIM ** -0.5)).astype(BF16)
            dq = jnp.where(mine, jnp.dot(dsb, kw, preferred_element_type=F32), dq)
            dk = dk + lax.dot_general(dsb, qh, (((0,), (0,)), ((), ())), preferred_element_type=F32)
        dq_ref[...] = dq.astype(dq_ref.dtype)
        dk_ref[...] = dk
        dv_ref[...] = dv

    win = pl.BlockSpec((None, ATT_WIN, LANES), lambda hp, i: (i, 0, hp))
    return pl.pallas_call(
        body, name=name, grid=(HP, nq),
        in_specs=[q_spec, *k_specs, *v_specs,
                  pl.BlockSpec((2, ATT_TQ, ATT_WIN), lambda hp, i: (hp, 0, 0)),
                  pl.BlockSpec((ATT_TQ, LANES), lambda hp, i: (i, hp)),
                  pl.BlockSpec(memory_space=pl.ANY)],
        out_specs=[pl.BlockSpec((ATT_TQ, LANES), lambda hp, i: (i, qcol // LANES + hp)), win, win,
                   pl.BlockSpec((2, ATT_TQ, ATT_WIN), lambda hp, i: (hp, 0, 0))],
        out_shape=[jax.ShapeDtypeStruct(dproj.shape, dproj.dtype),
                   jax.ShapeDtypeStruct((nq, ATT_WIN, dattn), F32), jax.ShapeDtypeStruct((nq, ATT_WIN, dattn), F32),
                   jax.ShapeDtypeStruct((2 * HP, ATT_TQ, ATT_WIN), F32)],
        input_output_aliases={3 + 2 * ATT_NB: 0},
        compiler_params=_cparams(("arbitrary", "arbitrary")),
    )(*([proj] * (1 + 2 * ATT_NB)), bias, dout, dproj)


def window_sum(win, dproj, *, col, name):
    nq, _, C = win.shape
    assert col % C == 0

    def body(*refs):
        w_refs = refs[:ATT_NB]
        o_ref = refs[ATT_NB + 1]
        j = pl.program_id(0)
        acc = w_refs[0][...]
        for b in range(1, ATT_NB):
            acc = acc + jnp.where(j + b < nq, w_refs[b][...], 0.0)
        o_ref[...] = acc.astype(o_ref.dtype)

    def part(b):
        return pl.BlockSpec((None, ATT_TQ, C), lambda j: (jnp.minimum(j + b, nq - 1), ATT_NB - 1 - b, 0))

    return pl.pallas_call(
        body, name=name, grid=(nq,),
        in_specs=[part(b) for b in range(ATT_NB)] + [pl.BlockSpec(memory_space=pl.ANY)],
        out_specs=pl.BlockSpec((ATT_TQ, C), lambda j: (j, col // C)),
        out_shape=jax.ShapeDtypeStruct(dproj.shape, dproj.dtype),
        input_output_aliases={ATT_NB: 0},
        compiler_params=_cparams(("arbitrary",)),
    )(*([win] * ATT_NB), dproj)


def _mem_probs(q, km, scale):
    s = lax.dot_general(q, km, (((1,), (1,)), ((), ())), preferred_element_type=F32) * scale
    s = s - jnp.max(s, axis=-1, keepdims=True)
    p = jnp.exp(s)
    return p * (1.0 / jnp.sum(p, axis=-1, keepdims=True))


def memattn_fwd(proj, kv, *, qcol, tq=512, name):
    T = proj.shape[0]
    M, dm2 = kv.shape
    DM = dm2 // 2
    hd = DM // N_MEM_HEADS
    tq = min(tq, T)

    def body(q_ref, kv_ref, o_ref):
        for h in range(N_MEM_HEADS):
            sl = pl.ds(h * hd, hd)
            p = _mem_probs(q_ref[:, sl], kv_ref[:, sl], hd ** -0.5)
            o = jnp.dot(p.astype(BF16), kv_ref[:, pl.ds(DM + h * hd, hd)], preferred_element_type=F32)
            o_ref[:, sl] = o.astype(o_ref.dtype)

    return pl.pallas_call(
        body, name=name, grid=(T // tq,),
        in_specs=[pl.BlockSpec((tq, DM), lambda i: (i, qcol // DM)),
                  pl.BlockSpec((M, 2 * DM), lambda i: (0, 0))],
        out_specs=pl.BlockSpec((tq, DM), lambda i: (i, 0)),
        out_shape=jax.ShapeDtypeStruct((T, DM), BF16),
        compiler_params=_cparams(("parallel",)),
    )(proj, kv)


def memattn_bwd(proj, kv, dout, dproj, *, qcol, tq=512, name):
    T = proj.shape[0]
    M, dm2 = kv.shape
    DM = dm2 // 2
    hd = DM // N_MEM_HEADS
    tq = min(tq, T)

    def body(q_ref, kv_ref, do_ref, _, dq_ref, dkv_ref):
        i = pl.program_id(0)

        @pl.when(i == 0)
        def _():
            dkv_ref[...] = jnp.zeros_like(dkv_ref)

        for h in range(N_MEM_HEADS):
            sl = pl.ds(h * hd, hd)
            vsl = pl.ds(DM + h * hd, hd)
            q = q_ref[:, sl]
            do = do_ref[:, sl]
            p = _mem_probs(q, kv_ref[:, sl], hd ** -0.5)
            dp = lax.dot_general(do, kv_ref[:, vsl], (((1,), (1,)), ((), ())), preferred_element_type=F32)
            dkv_ref[:, vsl] += lax.dot_general(p.astype(BF16), do, (((0,), (0,)), ((), ())),
                                               preferred_element_type=F32)
            ds = p * (dp - jnp.sum(dp * p, axis=-1, keepdims=True))
            dsb = (ds * (hd ** -0.5)).astype(BF16)
            dq_ref[:, sl] = jnp.dot(dsb, kv_ref[:, sl], preferred_element_type=F32).astype(dq_ref.dtype)
            dkv_ref[:, sl] += lax.dot_general(dsb, q, (((0,), (0,)), ((), ())), preferred_element_type=F32)

    return pl.pallas_call(
        body, name=name, grid=(T // tq,),
        in_specs=[pl.BlockSpec((tq, DM), lambda i: (i, qcol // DM)),
                  pl.BlockSpec((M, 2 * DM), lambda i: (0, 0)),
                  pl.BlockSpec((tq, DM), lambda i: (i, 0)),
                  pl.BlockSpec(memory_space=pl.ANY)],
        out_specs=[pl.BlockSpec((tq, DM), lambda i: (i, qcol // DM)),
                   pl.BlockSpec((M, 2 * DM), lambda i: (0, 0))],
        out_shape=[jax.ShapeDtypeStruct(dproj.shape, dproj.dtype), jax.ShapeDtypeStruct((M, 2 * DM), F32)],
        input_output_aliases={3: 0},
        compiler_params=_cparams(("arbitrary",)),
    )(proj, kv, dout, dproj)


def merge_fwd(proj, gate_b, ys, *, gcol, tr=512, tc=512, name):
    T = proj.shape[0]
    D = ys[0].shape[1]
    tr, tc = min(tr, T), _tile(D, tc)
    nd = D // tc

    def body(g0, g1, g2, b0, b1, b2, y0, y1, y2, o_ref):
        acc = jnp.zeros((tr, tc), F32)
        for g, b, y in ((g0, b0, y0), (g1, b1, y1), (g2, b2, y2)):
            acc = acc + jax.nn.sigmoid(g[...].astype(F32) + b[...]) * y[...].astype(F32)
        o_ref[...] = acc.astype(o_ref.dtype)

    def gate(b):
        return pl.BlockSpec((tr, tc), lambda i, j: (i, gcol // tc + b * nd + j))

    def bias(b):
        return pl.BlockSpec((1, tc), lambda i, j: (0, b * nd + j))

    blk = pl.BlockSpec((tr, tc), lambda i, j: (i, j))
    return pl.pallas_call(
        body, name=name, grid=(T // tr, nd),
        in_specs=[gate(0), gate(1), gate(2), bias(0), bias(1), bias(2), blk, blk, blk],
        out_specs=blk,
        out_shape=jax.ShapeDtypeStruct((T, D), BF16),
        compiler_params=_cparams(("parallel", "parallel")),
    )(proj, proj, proj, gate_b, gate_b, gate_b, *ys)


def merge_bwd(dmerged, proj, gate_b, ys, *, gcol, tr=512, tc=512, name):
    T, D_IN = proj.shape
    D = ys[0].shape[1]
    tr, tc = min(tr, T), _tile(D, tc)
    nd = D // tc

    def body(dm_ref, g_ref, b_ref, y0, y1, y2, dp_ref, dy_ref, db_ref):
        br = pl.program_id(0)
        i = pl.program_id(2)
        y = jnp.where(br == 0, y0[...], jnp.where(br == 1, y1[...], y2[...])).astype(F32)
        dm = dm_ref[...].astype(F32)
        sg = jax.nn.sigmoid(g_ref[...].astype(F32) + b_ref[...])
        dy_ref[...] = (dm * sg).astype(dy_ref.dtype)
        dg = dm * y * sg * (1.0 - sg)
        dp_ref[...] = dg.astype(dp_ref.dtype)

        @pl.when(i == 0)
        def _():
            db_ref[...] = jnp.zeros_like(db_ref)

        db_ref[...] += jnp.sum(dg, axis=0, keepdims=True)

    def ysp(b):
        return pl.BlockSpec((tr, tc), lambda br, j, i: (jnp.where(br == b, i, 0), jnp.where(br == b, j, 0)))

    return pl.pallas_call(
        body, name=name, grid=(3, nd, T // tr),
        in_specs=[pl.BlockSpec((tr, tc), lambda br, j, i: (i, j)),
                  pl.BlockSpec((tr, tc), lambda br, j, i: (i, gcol // tc + br * nd + j)),
                  pl.BlockSpec((1, tc), lambda br, j, i: (0, br * nd + j)),
                  ysp(0), ysp(1), ysp(2)],
        out_specs=[pl.BlockSpec((tr, tc), lambda br, j, i: (i, gcol // tc + br * nd + j)),
                   pl.BlockSpec((None, tr, tc), lambda br, j, i: (br, i, j)),
                   pl.BlockSpec((1, tc), lambda br, j, i: (0, br * nd + j))],
        out_shape=[jax.ShapeDtypeStruct((T, D_IN), BF16), jax.ShapeDtypeStruct((3, T, D), BF16),
                   jax.ShapeDtypeStruct((1, 3 * D), F32)],
        compiler_params=_cparams(("arbitrary", "arbitrary", "arbitrary")),
    )(dmerged, proj, gate_b, *ys)


FFN_CHUNK = 16
FFN_ROWS = 32
SUBLANES = 8


def _shift_down(tail, x, s):
    return pltpu.roll(jnp.concatenate([tail, x], axis=0), s, 0)[SUBLANES:]


def _shift_up(x, head, s):
    n = x.shape[0]
    return pltpu.roll(jnp.concatenate([x, head], axis=0), n + SUBLANES - s, 0)[:n]


def _ffn_taps(w_ref, b_ref):
    return [w_ref[pl.ds(k, 1), :] for k in range(FFN_CONV_WIDTH)] + [b_ref[...]]


def _ffn_conv(tail, x, taps):
    w0, w1, w2, b = taps
    x1, x2 = _shift_down(tail, x, 1), _shift_down(tail, x, 2)
    return w0 * x2 + w1 * x1 + w2 * x + b, x1, x2


def _ffn_rows(c):
    return pl.ds(c * FFN_ROWS if isinstance(c, int) else pl.multiple_of(c * FFN_ROWS, FFN_ROWS), FFN_ROWS)


def _ffn_specs(tt, tc, nv, order, with_next):
    hb = tt // FFN_CHUNK

    def mk(shape, f):
        return pl.BlockSpec(shape, (lambda i, j: f(i, j)) if order == "ij" else (lambda j, i: f(i, j)))

    def halo_after(T):
        return lambda i, j, off: (jnp.minimum((i + 1) * hb, T // FFN_CHUNK - 1), off + j)

    out = []
    for off in (0, nv):
        out += [mk((FFN_CHUNK, tc), lambda i, j, off=off: (jnp.maximum(i * hb - 1, 0), off + j)),
                mk((tt, tc), lambda i, j, off=off: (i, off + j))]
        if with_next:
            out.append(mk((FFN_CHUNK, tc), lambda i, j, off=off: halo_after(with_next)(i, j, off)))
    out += [mk((FFN_CONV_WIDTH, tc), lambda i, j, off=off: (0, off + j)) for off in (0, nv)]
    out += [mk((1, tc), lambda i, j, off=off: (0, off + j)) for off in (0, nv)]
    return out


def ffn_act_fwd(up0, w, b, *, tt=1024, tc=256, name):
    T, dff2 = up0.shape
    DFF = dff2 // 2
    tt, tc = min(tt, T), _tile(DFF, tc)
    nv = DFF // tc

    def body(vp, vm, gp, gm, wv, wg, bv, bg, o_ref):
        i = pl.program_id(0)
        taps_v, taps_g = _ffn_taps(wv, bv), _ffn_taps(wg, bg)

        def chunk(c, tails):
            rows = _ffn_rows(c)
            xv, xg = vm[rows, :].astype(F32), gm[rows, :].astype(F32)
            val = _ffn_conv(tails[0], xv, taps_v)[0]
            gt = _ffn_conv(tails[1], xg, taps_g)[0]
            o_ref[rows, :] = (gt * jax.nn.sigmoid(gt) * val).astype(o_ref.dtype)
            return xv[-SUBLANES:], xg[-SUBLANES:]

        before = lambda ref: jnp.where(i > 0, ref[...].astype(F32)[-SUBLANES:], 0.0)
        lax.fori_loop(0, tt // FFN_ROWS, chunk, (before(vp), before(gp)))

    return pl.pallas_call(
        body, name=name, grid=(T // tt, nv),
        in_specs=_ffn_specs(tt, tc, nv, "ij", None),
        out_specs=pl.BlockSpec((tt, tc), lambda i, j: (i, j)),
        out_shape=jax.ShapeDtypeStruct((T, DFF), BF16),
        compiler_params=_cparams(("parallel", "parallel")),
    )(up0, up0, up0, up0, w, w, b, b)


def ffn_bwd(dact, up0, w, b, *, tt=1024, tc=256, name):
    T, dff2 = up0.shape
    DFF = dff2 // 2
    tt, tc = min(tt, T), _tile(DFF, tc)
    nv = DFF // tc
    hb = tt // FFN_CHUNK
    n_t = T // tt
    n = tt // FFN_ROWS

    def body(da, dan, vp, vm, vn, gp, gm, gn, wv, wg, bv, bg, o_ref, dw_ref, db_ref, acc):
        i = pl.program_id(1)
        taps_v, taps_g = _ffn_taps(wv, bv), _ffn_taps(wg, bg)
        f32 = lambda ref, rows=slice(None): ref[rows, :].astype(F32)

        def act_grads(d_act, tail_v, xv, tail_g, xg):
            val, v1, v2 = _ffn_conv(tail_v, xv, taps_v)
            gt, g1, g2 = _ffn_conv(tail_g, xg, taps_g)
            sg = jax.nn.sigmoid(gt)
            return d_act * gt * sg, d_act * val * (sg * (1.0 + gt * (1.0 - sg))), (xv, v1, v2), (xg, g1, g2)

        tail = lambda ref, end: f32(ref, pl.ds(end - FFN_CHUNK, FFN_CHUNK))[-SUBLANES:]
        dnv, dng, _, _ = act_grads(f32(dan), tail(vm, tt), f32(vn), tail(gm, tt), f32(gn))
        heads = (jnp.where(i < n_t - 1, dnv[:SUBLANES], 0.0), jnp.where(i < n_t - 1, dng[:SUBLANES], 0.0))
        acc[...] = jnp.zeros_like(acc)

        def emit(c, tail_v, tail_g, heads):
            rows = _ffn_rows(c)
            dv, dg, xs_v, xs_g = act_grads(f32(da, rows), tail_v, f32(vm, rows), tail_g, f32(gm, rows))
            for half, (d, head, xs, taps) in enumerate(((dv, heads[0], xs_v, taps_v), (dg, heads[1], xs_g, taps_g))):
                w0, w1, w2, _ = taps
                o_ref[half, rows, :] = (w2 * d + w1 * _shift_up(d, head, 1) + w0 * _shift_up(d, head, 2)
                                        ).astype(o_ref.dtype)
                for k in range(FFN_CONV_WIDTH):
                    acc[4 * half + k] += d * xs[FFN_CONV_WIDTH - 1 - k]
                acc[4 * half + 3] += d
            return dv[:SUBLANES], dg[:SUBLANES]

        def chunk(k, heads):
            c = n - 1 - k
            end = pl.multiple_of(c * FFN_ROWS, FFN_ROWS)
            return emit(c, tail(vm, end), tail(gm, end), heads)

        heads = lax.fori_loop(0, n - 1, chunk, heads)
        before = lambda ref: jnp.where(i > 0, f32(ref)[-SUBLANES:], 0.0)
        emit(0, before(vp), before(gp), heads)

        @pl.when(i == 0)
        def _():
            dw_ref[...] = jnp.zeros_like(dw_ref)
            db_ref[...] = jnp.zeros_like(db_ref)

        for half in range(2):
            for k in range(FFN_CONV_WIDTH):
                dw_ref[half, pl.ds(k, 1), :] += jnp.sum(acc[4 * half + k], axis=0, keepdims=True)
            db_ref[half] += jnp.sum(acc[4 * half + 3], axis=0, keepdims=True)

    last_halo = T // FFN_CHUNK - 1
    return pl.pallas_call(
        body, name=name, grid=(nv, n_t),
        in_specs=[pl.BlockSpec((tt, tc), lambda j, i: (i, j)),
                  pl.BlockSpec((FFN_CHUNK, tc), lambda j, i: (jnp.minimum((i + 1) * hb, last_halo), j))]
                 + _ffn_specs(tt, tc, nv, "ji", T),
        out_specs=[pl.BlockSpec((2, tt, tc), lambda j, i: (0, i, j)),
                   pl.BlockSpec((2, FFN_CONV_WIDTH, tc), lambda j, i: (0, 0, j)),
                   pl.BlockSpec((2, 1, tc), lambda j, i: (0, 0, j))],
        out_shape=[jax.ShapeDtypeStruct((2, T, DFF), BF16), jax.ShapeDtypeStruct((2, FFN_CONV_WIDTH, DFF), F32),
                   jax.ShapeDtypeStruct((2, 1, DFF), F32)],
        scratch_shapes=[pltpu.VMEM((8, FFN_ROWS, tc), F32)],
        compiler_params=_cparams(("arbitrary", "arbitrary")),
    )(dact, dact, up0, up0, up0, up0, up0, up0, w, w, b, b)


def adamw(contribs, w, m, v, *, name):
    L, R, C = w.shape
    assert len(contribs) == L and all(c.shape == (N_DEV, R, C) for c in contribs)
    tr = R
    if R * C > 256 * 1024 and R % 8 == 0:
        tr = 8
        while R % (tr * 2) == 0 and tr * 2 * C <= 256 * 1024:
            tr *= 2
    c1 = 1.0 - ADAM_B1 ** ADAM_STEP
    c2 = 1.0 - ADAM_B2 ** ADAM_STEP

    def body(*refs):
        c_refs = refs[:L]
        w_ref, m_ref, v_ref, g_ref, d_ref, nm_ref, nv_ref = refs[L:]
        layer = pl.program_id(0)
        for lp in range(L):
            @pl.when(layer == lp)
            def _(c_ref=c_refs[lp]):
                g = c_ref[0].astype(F32)
                for s in range(1, N_DEV):
                    g = g + c_ref[s].astype(F32)
                g_ref[...] = g

        g = g_ref[...]
        nm = ADAM_B1 * m_ref[...] + (1.0 - ADAM_B1) * g
        nv = ADAM_B2 * v_ref[...] + (1.0 - ADAM_B2) * (g * g)
        nm_ref[...] = nm
        nv_ref[...] = nv
        d_ref[...] = -ADAM_LR * ((nm / c1) / (jnp.sqrt(nv / c2) + ADAM_EPS) + ADAM_WD * w_ref[...])

    def contrib_spec(lp):
        return pl.BlockSpec((N_DEV, tr, C), lambda l, i: (0, jnp.where(l == lp, i, 0), 0))

    blk = pl.BlockSpec((None, tr, C), lambda l, i: (l, i, 0))
    out = jax.ShapeDtypeStruct((L, R, C), F32)
    return pl.pallas_call(
        body, name=name, grid=(L, R // tr),
        in_specs=[contrib_spec(lp) for lp in range(L)] + [blk, blk, blk],
        out_specs=[blk, blk, blk, blk], out_shape=[out, out, out, out],
        compiler_params=_cparams(("arbitrary", "arbitrary")),
    )(*contribs, w, m, v)


def _my_position():
    x, y, c = lax.axis_index("x"), lax.axis_index("y"), lax.axis_index("c")
    return x, y, c, 4 * x + 2 * y + c


def _peers(x, y, c):
    out = []
    for r in range(1, N_DEV):
        px = 1 - x if r & 4 else x
        py = 1 - y if r & 2 else y
        pc = 1 - c if r & 1 else c
        out.append(((px, py, pc), 4 * px + 2 * py + pc))
    return out


def _run_exchange(plan, n, send_sems, recv_sems, local_sems):
    x, y, c, me = _my_position()
    copies = []
    for t in range(n):
        src, dst = plan(t, None, me)
        own = pltpu.make_async_copy(src, dst, local_sems.at[t])
        own.start()
        copies.append(own)
    remote = []
    for r, (peer, peer_index) in enumerate(_peers(x, y, c)):
        for t in range(n):
            src, dst = plan(t, peer_index, me)
            cp = pltpu.make_async_remote_copy(src_ref=src, dst_ref=dst, send_sem=send_sems.at[t, r],
                                              recv_sem=recv_sems.at[t, r], device_id=peer,
                                              device_id_type=pl.DeviceIdType.MESH)
            cp.start()
            remote.append(cp)
    for cp in remote:
        cp.wait_send()
    for cp in remote:
        cp.wait_recv()
    for cp in copies:
        cp.wait()


def all_gather(blocks, *, name):
    n = len(blocks)

    def body(*refs):
        srcs, outs = refs[:n], refs[n:2 * n]
        send_sems, recv_sems, local_sems = refs[2 * n:]
        _run_exchange(lambda t, peer_index, me: (srcs[t], outs[t].at[me]), n, send_sems, recv_sems, local_sems)

    hbm = pl.BlockSpec(memory_space=pl.ANY)
    return pl.pallas_call(
        body, name=name, in_specs=[hbm] * n, out_specs=[hbm] * n,
        out_shape=[jax.ShapeDtypeStruct((N_DEV, *b.shape), b.dtype) for b in blocks],
        scratch_shapes=[pltpu.SemaphoreType.DMA((n, N_DEV - 1)), pltpu.SemaphoreType.DMA((n, N_DEV - 1)),
                        pltpu.SemaphoreType.DMA((n,))],
    )(*blocks)


_HBM = pl.BlockSpec(memory_space=pltpu.HBM)
_SEM = pl.BlockSpec(memory_space=pltpu.SEMAPHORE)


def _push_copies(kind, src_refs, land_refs, send_sems, recv_sems):
    x, y, c, me = _my_position()
    copies = []
    for r, (peer, peer_index) in enumerate(_peers(x, y, c)):
        for t, (src, land) in enumerate(zip(src_refs, land_refs)):
            copies.append(pltpu.make_async_remote_copy(
                src_ref=src if kind == "gather" else src.at[peer_index], dst_ref=land.at[me],
                send_sem=send_sems.at[t * (N_DEV - 1) + r], recv_sem=recv_sems.at[t * (N_DEV - 1) + r],
                device_id=peer, device_id_type=pl.DeviceIdType.MESH))
    return copies


def push_start(srcs, lands, *, kind, name):
    n = len(srcs)

    def body(*refs):
        send_sems, recv_sems = refs[2 * n:2 * n + 2]
        token = refs[-1]
        for cp in _push_copies(kind, refs[:n], refs[n:2 * n], send_sems, recv_sems):
            cp.start()
        token[...] = jnp.zeros_like(token)

    sems = pltpu.SemaphoreType.DMA((n * (N_DEV - 1),))
    arrays = [*srcs, *lands]
    out = pl.pallas_call(
        body, name=name,
        out_shape=(sems, sems, *[pltpu.HBM(a.shape, a.dtype) for a in arrays],
                   jax.ShapeDtypeStruct((8, LANES), F32)),
        in_specs=[_HBM] * (2 * n),
        out_specs=(_SEM, _SEM, *[_HBM] * (2 * n), pl.BlockSpec(memory_space=pltpu.VMEM)),
        input_output_aliases={i: 2 + i for i in range(2 * n)},
        compiler_params=pltpu.CompilerParams(has_side_effects=pltpu.SideEffectType.DATAFLOW_SIDE_EFFECTING),
    )(*[pltpu.with_memory_space_constraint(a, pltpu.HBM) for a in arrays])
    return out[0], out[1], list(out[2:2 + n]), list(out[2 + n:2 + 2 * n]), out[-1]


def push_wait(send_sems, recv_sems, srcs, lands, after, *, kind, name):
    n = len(srcs)

    def body(*refs):
        for cp in _push_copies(kind, refs[:n], refs[n:2 * n], refs[2 * n], refs[2 * n + 1]):
            cp.wait_send()
            cp.wait_recv()

    arrays = [*srcs, *lands]
    out = pl.pallas_call(
        body, name=name,
        out_shape=tuple(pltpu.HBM(a.shape, a.dtype) for a in arrays),
        in_specs=[_HBM] * (2 * n) + [_SEM, _SEM, pl.BlockSpec(memory_space=pl.ANY)],
        out_specs=tuple([_HBM] * (2 * n)),
        input_output_aliases={i: i for i in range(2 * n)},
        compiler_params=pltpu.CompilerParams(has_side_effects=pltpu.SideEffectType.DATAFLOW_SIDE_EFFECTING),
    )(*arrays, send_sems, recv_sems, after)
    return list(out[n:])


def _own_slot(block, me):
    zone = lax.empty((N_DEV, *block.shape), block.dtype)
    return lax.dynamic_update_slice(zone, block[None], (me,) + (0,) * block.ndim)


WEIGHT_NAMES = ('mix_norm_g', 'mem_norm_g', 'w_in', 'gate_b', 'conv_w', 'conv_b', 'conv_ln_g', 'conv_ln_b',
                'w_conv_out', 'rel_bias', 'w_attn_out', 'w_mem_kv', 'w_mem_out', 'w_o', 'ffn_norm_g', 'w_up',
                'ffn_conv_w', 'ffn_conv_b', 'w_down', 'final_norm_g')
BIG = (('w_in', False), ('w_conv_out', False), ('w_attn_out', False), ('w_mem_out', False), ('w_up', False),
       ('w_mem_kv', True), ('w_o', True), ('w_down', True))
BY_ROWS = dict(BIG)
GATHER_GROUPS = (('w_in',), ('w_mem_kv', 'w_conv_out', 'w_attn_out', 'w_mem_out', 'w_o'), ('w_up', 'w_down'))


MIN_SLOT_COLS = 512


def _as_matrix(gathered, by_rows):
    n, r, c = gathered.shape
    if by_rows:
        return gathered.reshape(1, n * r, c)
    if c >= MIN_SLOT_COLS:
        return gathered
    return jnp.transpose(gathered, (1, 0, 2)).reshape(1, r, n * c)


def _as_shards(grad, by_rows):
    s, r, c = grad.shape
    if by_rows:
        return grad.reshape(N_DEV, r // N_DEV, c)
    if s == N_DEV:
        return grad
    return jnp.transpose(grad.reshape(r, N_DEV, c // N_DEV), (1, 0, 2))


class _LayerWeights:
    def __init__(self, pending):
        self.pending = pending
        self.ready = {}

    def get(self, name, after):
        for item in self.pending:
            names, started, wait_name = item
            if name in names and started is not None:
                got = push_wait(*started[:4], after, kind="gather", name=wait_name)
                self.ready.update({n: _as_matrix(a, BY_ROWS[n]) for n, a in zip(names, got)})
                item[1] = None
        return self.ready[name]

    def __getitem__(self, name):
        return self.ready[name]


def _forward_layer(h, mem2, p, l, dims, prefetch):
    W = p['gathered'][l]
    row = lambda name: p[name][l:l + 1]
    sv = {'h': h}
    sv['xn'] = rms_fwd(h, row('mix_norm_g'), name="mix_norm")
    sv['proj'] = mm_nn(sv['xn'], W.get('w_in', sv['xn']), out_dtype=BF16, name="w_in")
    sv['s'], sv['c'] = conv_fwd(sv['proj'], p['conv_w_full'][l], row('conv_b'), row('conv_ln_g'),
                                row('conv_ln_b'), name="conv_module")
    sv['bias'] = attn_bias(p['rel_pad'][l], name="attn_bias")
    sv['att'] = attn_fwd(sv['proj'], sv['bias'], qcol=dims['qcol'], dattn=dims['DA'], name="chunk_attn")
    sv['mn'] = rms_fwd(mem2, row('mem_norm_g'), name="mem_norm")
    w_mem_kv = W.get('w_mem_kv', sv['att'])
    sv['kv'] = mm_nn(sv['mn'], w_mem_kv, out_dtype=BF16, after=prefetch(w_mem_kv), name="w_mem_kv")
    sv['mo'] = memattn_fwd(sv['proj'], sv['kv'], qcol=dims['mcol'], name="mem_attn")
    sv['ys'] = (mm_nn(sv['s'], W['w_conv_out'], out_dtype=BF16, name="w_conv_out"),
                mm_nn(sv['att'], W['w_attn_out'], out_dtype=BF16, name="w_attn_out"),
                mm_nn(sv['mo'], W['w_mem_out'], out_dtype=BF16, name="w_mem_out"))
    sv['merged'] = merge_fwd(sv['proj'], row('gate_b'), sv['ys'], gcol=dims['gcol'], name="merge")
    sv['h1'] = mm_nn(sv['merged'], W['w_o'], out_dtype=F32, residual=h, name="w_o")
    sv['hn'] = rms_fwd(sv['h1'], row('ffn_norm_g'), name="ffn_norm")
    sv['up0'] = mm_nn(sv['hn'], W.get('w_up', sv['hn']), out_dtype=BF16, name="w_up")
    sv['act'] = ffn_act_fwd(sv['up0'], p['ffn_conv_w_full'][l], row('ffn_conv_b'), name="ffn_act")
    h2 = mm_nn(sv['act'], W['w_down'], out_dtype=F32, residual=sv['h1'], name="w_down")
    return h2, sv


SCATTER_GROUPS = (('w_down', 'w_up'), ('w_o', 'w_conv_out', 'w_attn_out', 'w_mem_out', 'w_mem_kv'), ('w_in',))


def _backward_layer(dh, dhb, mem2, sv, p, l, dims, scatter, after=None):
    W = p['gathered'][l]
    row = lambda name: p[name][l:l + 1]
    g, small = {}, {}
    dact = mm_nt(dhb, W['w_down'], out_dtype=BF16, after=after, name="d_act")
    g['w_down'] = mm_tn(sv['act'], dhb, slots=1, out_dtype=BF16, name="g_w_down")
    dup0, dtaps, dbias = ffn_bwd(dact, sv['up0'], p['ffn_conv_w_full'][l], row('ffn_conv_b'), name="ffn_bwd")
    small['ffn_conv_w'] = jnp.transpose(dtaps, (1, 0, 2)).reshape(FFN_CONV_WIDTH, -1)
    small['ffn_conv_b'] = jnp.transpose(dbias, (1, 0, 2)).reshape(1, -1)
    dhn = mm_nt(dup0, W['w_up'], lead="planes", out_dtype=BF16, name="d_hn")
    g['w_up'] = mm_tn(sv['hn'], dup0, lead="planes", slots=W['w_up'].shape[0], out_dtype=BF16, name="g_w_up")
    dh1, dh1b, small['ffn_norm_g'] = rms_bwd(sv['h1'], row('ffn_norm_g'), dhn, dh, name="ffn_norm_bwd")
    dmerged = mm_nt(dh1b, W['w_o'], out_dtype=BF16, after=scatter(SCATTER_GROUPS[0], g), name="d_merged")
    g['w_o'] = mm_tn(sv['merged'], dh1b, slots=1, out_dtype=BF16, name="g_w_o")
    dproj, dy, small['gate_b'] = merge_bwd(dmerged, sv['proj'], row('gate_b'), sv['ys'], gcol=dims['gcol'],
                                           name="merge_bwd")
    ds = mm_nt(dy, W['w_conv_out'], lead=0, out_dtype=BF16, name="d_conv_out")
    g['w_conv_out'] = mm_tn(sv['s'], dy, lead=0, slots=W['w_conv_out'].shape[0], out_dtype=BF16,
                            name="g_w_conv_out")
    dc, small['conv_ln_g'], small['conv_ln_b'] = conv_bwd_ln(ds, sv['c'], row('conv_ln_g'), row('conv_ln_b'),
                                                             name="conv_ln_bwd")
    dproj, small['conv_w'], small['conv_b'] = conv_bwd_taps(dc, sv['proj'], p['conv_w_full'][l], dproj,
                                                            name="conv_taps_bwd")
    datt = mm_nt(dy, W['w_attn_out'], lead=1, out_dtype=BF16, name="d_attn_out")
    g['w_attn_out'] = mm_tn(sv['att'], dy, lead=1, slots=W['w_attn_out'].shape[0], out_dtype=BF16,
                            name="g_w_attn_out")
    dproj, dkw, dvw, dbias = attn_bwd(sv['proj'], sv['bias'], datt, dproj, qcol=dims['qcol'], dattn=dims['DA'],
                                      name="chunk_attn_bwd")
    dproj = window_sum(dkw, dproj, col=dims['qcol'] + dims['DA'], name="dk_windows")
    dproj = window_sum(dvw, dproj, col=dims['qcol'] + 2 * dims['DA'], name="dv_windows")
    small['rel_bias'] = attn_bias_bwd(dbias, name="attn_bias_bwd")[:, :2 * MAX_REL + 1]
    dmo = mm_nt(dy, W['w_mem_out'], lead=2, out_dtype=BF16, name="d_mem_out")
    g['w_mem_out'] = mm_tn(sv['mo'], dy, lead=2, slots=W['w_mem_out'].shape[0], out_dtype=BF16,
                           name="g_w_mem_out")
    dproj, dkv = memattn_bwd(sv['proj'], sv['kv'], dmo, dproj, qcol=dims['mcol'], name="mem_attn_bwd")
    dkvb = dkv.astype(BF16)
    g['w_mem_kv'] = mm_tn(sv['mn'], dkvb, slots=1, out_dtype=BF16, name="g_w_mem_kv")
    dmn = mm_nt(dkvb, W['w_mem_kv'], out_dtype=BF16, name="d_mem_norm")
    _, _, small['mem_norm_g'] = rms_bwd(mem2, row('mem_norm_g'), dmn, jnp.zeros(mem2.shape, F32),
                                        name="mem_norm_bwd")
    dxn = mm_nt(dproj, W['w_in'], out_dtype=BF16, after=scatter(SCATTER_GROUPS[1], g), name="d_xn")
    g['w_in'] = mm_tn(sv['xn'], dproj, slots=N_DEV, out_dtype=BF16, name="g_w_in")
    dh0, dh0b, small['mix_norm_g'] = rms_bwd(sv['h'], row('mix_norm_g'), dxn, dh1, name="mix_norm_bwd")
    return dh0, dh0b, g['w_in'], small


def kernel(x, mem, mix_norm_g, mem_norm_g, w_in, gate_b, conv_w, conv_b, conv_ln_g, conv_ln_b, w_conv_out, rel_bias, w_attn_out, w_mem_kv, w_mem_out, w_o, ffn_norm_g, w_up, ffn_conv_w, ffn_conv_b, w_down, final_norm_g, loss_target, m_mix_norm_g, m_mem_norm_g, m_w_in, m_gate_b, m_conv_w, m_conv_b, m_conv_ln_g, m_conv_ln_b, m_w_conv_out, m_rel_bias, m_w_attn_out, m_w_mem_kv, m_w_mem_out, m_w_o, m_ffn_norm_g, m_w_up, m_ffn_conv_w, m_ffn_conv_b, m_w_down, m_final_norm_g, v_mix_norm_g, v_mem_norm_g, v_w_in, v_gate_b, v_conv_w, v_conv_b, v_conv_ln_g, v_conv_ln_b, v_w_conv_out, v_rel_bias, v_w_attn_out, v_w_mem_kv, v_w_mem_out, v_w_o, v_ffn_norm_g, v_w_up, v_ffn_conv_w, v_ffn_conv_b, v_w_down, v_final_norm_g):
    env = locals()
    w = {n: env[n] for n in WEIGHT_NAMES}
    mom = {n: env['m_' + n] for n in WEIGHT_NAMES}
    var = {n: env['v_' + n] for n in WEIGHT_NAMES}
    T, D = x.shape[-2:]
    L = w_in.shape[0]
    DC = conv_b.shape[-1]
    DA = N_DEV * w_attn_out.shape[-1] // 2
    dims = {'DA': DA, 'qcol': 2 * DC, 'mcol': 2 * DC + 3 * DA, 'gcol': 2 * DC + 3 * DA + D // 2}
    x2, mem2, target = x.reshape(T, D), mem.reshape(-1, D), loss_target.reshape(T, D)
    me = 4 * lax.axis_index("x") + 2 * lax.axis_index("y") + lax.axis_index("c")

    p = dict(w)

    def start_gather(l, names, first, tag):
        blocks = [w[n][l].astype(BF16) for n in names]
        first, blocks = lax.optimization_barrier((first, blocks))
        started = push_start(blocks, [_own_slot(b, me) for b in blocks], kind="gather",
                             name="gather_start_%d%s" % (l, tag))
        return [names, started, "gather_wait_%d%s" % (l, tag)]

    taps, ffn_taps = all_gather([conv_w, ffn_conv_w], name="gather_taps")
    p['conv_w_full'] = jnp.moveaxis(taps, 0, 2).reshape(L, conv_w.shape[1], -1)
    p['ffn_conv_w_full'] = jnp.moveaxis(ffn_taps, 0, 2).reshape(L, ffn_conv_w.shape[1], -1)
    p['rel_pad'] = jnp.pad(rel_bias, ((0, 0), (0, 0), (0, REL_PAD - rel_bias.shape[-1])))
    pending, first = [], taps
    for names, tag in zip(GATHER_GROUPS, "abc"):
        pending.append(start_gather(0, names, first, tag))
        first = pending[-1][1][4]
    p['gathered'] = [_LayerWeights(pending)] + [None] * (L - 1)

    h = x2
    saved = []
    for l in range(L):
        def prefetch(first, l=l):
            if l + 1 == L:
                return None
            nxt = start_gather(l + 1, [n for n, _ in BIG], first, "")
            p['gathered'][l + 1] = _LayerWeights([nxt])
            return nxt[1][4]

        h, sv = _forward_layer(h, mem2, p, l, dims, prefetch)
        saved.append(sv)
    loss_part, dh, dhb, d_final = loss_head(h, final_norm_g.reshape(1, D), target, name="loss_head")
    loss = lax.psum(loss_part[0, 0], ("x", "y", "c"))

    small, landed, inflight = [None] * L, [{} for _ in range(L)], []

    def scatter_start(l, names, g):
        tag = "abc"[SCATTER_GROUPS.index(names)]
        grads = [_as_shards(g[n], BY_ROWS[n]) for n in names]
        zones = [_own_slot(lax.dynamic_index_in_dim(a, me, 0, keepdims=False), me) for a in grads]
        started = push_start(grads, zones, kind="scatter", name="scatter_start_%d%s" % (l, tag))
        inflight.append((l, names, started, "scatter_wait_%d%s" % (l, tag)))
        return started[4]

    def scatter_finish(after, groups):
        for item in [i for i in inflight if (i[0], i[1]) in groups]:
            l, names, started, wait_name = item
            landed[l].update(zip(names, push_wait(*started[:4], after, kind="scatter", name=wait_name)))
            inflight.remove(item)

    token = None
    for l in reversed(range(L)):
        dh, dhb, g_w_in, small[l] = _backward_layer(dh, dhb, mem2, saved[l], p, l, dims,
                                                    functools.partial(scatter_start, l), after=token)
        scatter_finish(dh, [(l + 1, names) for names in SCATTER_GROUPS])
        if l > 0:
            token = scatter_start(l, SCATTER_GROUPS[2], {'w_in': g_w_in})

    big_names = [n for n, _ in BIG]
    small_names = [n for n in WEIGHT_NAMES if n not in big_names and n != 'final_norm_g']
    stacked = [jnp.stack([small[l][n] for l in range(L)]) for n in small_names] + [d_final]
    got = dict(zip(small_names + ['final_norm_g'], all_gather(stacked, name="gather_small_grads")))
    g_w_in, _ = lax.optimization_barrier((g_w_in, got['final_norm_g']))
    token = scatter_start(0, SCATTER_GROUPS[2], {'w_in': g_w_in})
    scatter_finish(token, [(0, names) for names in SCATTER_GROUPS[:2]])
    for n in ('conv_w', 'ffn_conv_w'):
        cs = w[n].shape[-1]
        got[n] = lax.dynamic_slice_in_dim(got[n], me * cs, cs, axis=3)
    contribs = {n: [got[n].reshape(N_DEV, -1, w[n].shape[-1])] for n in small_names + ['final_norm_g']}

    outs = {}
    for n in [n for n in WEIGHT_NAMES if n != 'w_in'] + ['w_in']:
        if n == 'w_in':
            done = lax.optimization_barrier(tuple(o[0] for o in outs.values()))
            scatter_finish(done[0], [(0, SCATTER_GROUPS[2])])
        if n in big_names:
            contribs[n] = [landed[l][n] for l in range(L)]
        shp = (len(contribs[n]),) + contribs[n][0].shape[1:]
        res = adamw(contribs[n], w[n].reshape(shp), mom[n].reshape(shp), var[n].reshape(shp), name="adamw_" + n)
        outs[n] = [r.reshape(w[n].shape) for r in res]
    return (loss, dh.reshape(x.shape),
            *[outs[n][0] for n in WEIGHT_NAMES], *[outs[n][1] for n in WEIGHT_NAMES],
            *[outs[n][2] for n in WEIGHT_NAMES], *[outs[n][3] for n in WEIGHT_NAMES])
```

```python
import functools

import numpy as np
import jax
import jax.numpy as jnp
from jax import lax
from jax.experimental import pallas as pl
from jax.experimental.pallas import tpu as pltpu

F32 = jnp.float32
BF16 = jnp.bfloat16

CHUNK = 64
LEFT_CHUNKS = 8
MAX_REL = 256
N_MEM_HEADS = 4
ATTN_HEAD_DIM = 64
CONV_WIDTH = 31
FFN_CONV_WIDTH = 3
EPS = 1e-6
NEG_INF = -1e30
ADAM_LR = 0.001
ADAM_B1 = 0.9
ADAM_B2 = 0.999
ADAM_EPS = 1e-08
ADAM_WD = 0.01
ADAM_STEP = 10
N_DEV = 8

VMEM_LIMIT_BYTES = 56 * 1024 * 1024
LANES = 128
SUBLANES = 8


def _cparams(sem=None):
    return pltpu.CompilerParams(dimension_semantics=sem, vmem_limit_bytes=VMEM_LIMIT_BYTES)


def _tile(n, want):
    if n <= want:
        return n
    t = (want // LANES) * LANES
    while t >= LANES:
        if n % t == 0:
            return t if 4 * t >= want or n > 2 * want else n
        t -= LANES
    return n


MM_TM, MM_TN, MM_TK = 1024, 1536, 2048


def _matmul(name, a, b, *, dims, grid, nk, a_spec, b_spec, out_spec, out_shape, acc_shape, residual=None,
            after=None):
    n_in = 2 + (residual is not None) + (after is not None)

    def body(*refs):
        a_ref, b_ref = refs[:2]
        o_ref = refs[n_in]

        def finish(r):
            if residual is not None:
                r = r + refs[2][...]
            o_ref[...] = r.astype(o_ref.dtype)

        part = lax.dot_general(a_ref[...], b_ref[...], (dims, ((), ())), preferred_element_type=F32)
        if nk == 1:
            finish(part)
            return
        acc = refs[-1]
        k = pl.program_id(2)

        @pl.when(k == 0)
        def _():
            acc[...] = part

        @pl.when((k > 0) & (k < nk - 1))
        def _():
            acc[...] += part

        @pl.when(k == nk - 1)
        def _():
            finish(acc[...] + part)

    in_specs, args = [a_spec, b_spec], [a, b]
    if residual is not None:
        in_specs.append(out_spec)
        args.append(residual)
    if after is not None:
        in_specs.append(pl.BlockSpec(memory_space=pl.ANY))
        args.append(after)
    return pl.pallas_call(
        body, name=name, grid=grid, in_specs=in_specs, out_specs=out_spec, out_shape=out_shape,
        scratch_shapes=[pltpu.VMEM(acc_shape, F32)] if nk > 1 else [],
        compiler_params=_cparams(("parallel", "parallel", "arbitrary")),
    )(*args)


def _lead_spec(arr, lead, block, index):
    if lead is None:
        assert arr.ndim == 2
        return pl.BlockSpec(block, index)
    assert arr.ndim == 3
    if lead == "planes":
        per_plane = arr.shape[2] // block[1]
        assert arr.shape[2] % block[1] == 0

        def planes_index(i, j, k):
            r, c = index(i, j, k)
            return c // per_plane, r, c % per_plane

        return pl.BlockSpec((None, *block), planes_index)
    return pl.BlockSpec((None, *block), lambda i, j, k: (lead, *index(i, j, k)))


def _matrix_shape(arr, lead):
    return (arr.shape[1], arr.shape[0] * arr.shape[2]) if lead == "planes" else arr.shape[-2:]


def mm_nn(a, w, *, out_dtype, residual=None, after=None, tm=MM_TM, tn=MM_TN, tk=MM_TK, name):
    M, K = a.shape
    S, K2, Ns = w.shape
    assert K == K2
    tm, tn, tk = _tile(M, tm), _tile(Ns, tn), _tile(K, tk)
    nb = Ns // tn
    return _matmul(
        name, a, w, dims=((1,), (0,)), grid=(M // tm, S * nb, K // tk), nk=K // tk,
        a_spec=pl.BlockSpec((tm, tk), lambda i, j, k: (i, k)),
        b_spec=pl.BlockSpec((None, tk, tn), lambda i, j, k: (j // nb, k, j % nb)),
        out_spec=pl.BlockSpec((tm, tn), lambda i, j, k: (i, j)),
        out_shape=jax.ShapeDtypeStruct((M, S * Ns), out_dtype), acc_shape=(tm, tn), residual=residual,
        after=after)


def mm_nt(a, w, *, out_dtype, lead=None, after=None, tm=MM_TM, tn=MM_TN, tk=MM_TK, name):
    M, N = _matrix_shape(a, lead)
    S, K, Ns = w.shape
    assert N == S * Ns
    tm, tn, tk = _tile(M, tm), _tile(K, tn), _tile(Ns, tk)
    nb = Ns // tk
    return _matmul(
        name, a, w, dims=((1,), (1,)), grid=(M // tm, K // tn, S * nb), nk=S * nb,
        a_spec=_lead_spec(a, lead, (tm, tk), lambda i, j, k: (i, k)),
        b_spec=pl.BlockSpec((None, tn, tk), lambda i, j, k: (k // nb, j, k % nb)),
        out_spec=pl.BlockSpec((tm, tn), lambda i, j, k: (i, j)),
        out_shape=jax.ShapeDtypeStruct((M, K), out_dtype), acc_shape=(tm, tn), after=after)


def mm_tn(a, b, *, slots, out_dtype, lead=None, tm=MM_TM, tn=MM_TN, tk=MM_TK, name):
    T, K = a.shape
    T2, N = _matrix_shape(b, lead)
    assert T == T2 and N % slots == 0
    Ns = N // slots
    tm, tn, tk = _tile(K, tm), _tile(Ns, tn), _tile(T, tk)
    nb = Ns // tn
    return _matmul(
        name, a, b, dims=((0,), (0,)), grid=(K // tm, slots * nb, T // tk), nk=T // tk,
        a_spec=pl.BlockSpec((tk, tm), lambda i, j, k: (k, i)),
        b_spec=_lead_spec(b, lead, (tk, tn), lambda i, j, k: (k, j)),
        out_spec=pl.BlockSpec((None, tm, tn), lambda i, j, k: (j // nb, i, j % nb)),
        out_shape=jax.ShapeDtypeStruct((slots, K, Ns), out_dtype), acc_shape=(tm, tn))


def rms_fwd(h, g, *, tr=256, name):
    T, D = h.shape
    tr = min(tr, T)

    def body(h_ref, g_ref, o_ref):
        x = h_ref[...]
        r = lax.rsqrt(jnp.mean(x * x, axis=-1, keepdims=True) + EPS)
        o_ref[...] = (x * r * g_ref[...]).astype(o_ref.dtype)

    return pl.pallas_call(
        body, name=name, grid=(T // tr,),
        in_specs=[pl.BlockSpec((tr, D), lambda i: (i, 0)), pl.BlockSpec((1, D), lambda i: (0, 0))],
        out_specs=pl.BlockSpec((tr, D), lambda i: (i, 0)),
        out_shape=jax.ShapeDtypeStruct((T, D), BF16),
        compiler_params=_cparams(("parallel",)),
    )(h, g)


def rms_bwd(h, g, dxn, dres, *, tr=256, name):
    T, D = h.shape
    tr = min(tr, T)

    def body(h_ref, g_ref, d_ref, r_ref, dh_ref, dhb_ref, dg_ref):
        i = pl.program_id(0)
        x = h_ref[...]
        r = lax.rsqrt(jnp.mean(x * x, axis=-1, keepdims=True) + EPS)
        xh = x * r
        d = d_ref[...].astype(F32)

        @pl.when(i == 0)
        def _():
            dg_ref[...] = jnp.zeros_like(dg_ref)

        dg_ref[...] += jnp.sum(d * xh, axis=0, keepdims=True)
        dxh = d * g_ref[...]
        dh = r * (dxh - xh * jnp.mean(dxh * xh, axis=-1, keepdims=True)) + r_ref[...]
        dh_ref[...] = dh
        dhb_ref[...] = dh.astype(BF16)

    row = pl.BlockSpec((tr, D), lambda i: (i, 0))
    vec = pl.BlockSpec((1, D), lambda i: (0, 0))
    return pl.pallas_call(
        body, name=name, grid=(T // tr,),
        in_specs=[row, vec, row, row],
        out_specs=[row, row, vec],
        out_shape=[jax.ShapeDtypeStruct((T, D), F32), jax.ShapeDtypeStruct((T, D), BF16),
                   jax.ShapeDtypeStruct((1, D), F32)],
        compiler_params=_cparams(("arbitrary",)),
    )(h, g, dxn, dres)


def loss_head(h, g, target, *, tr=256, name):
    T, D = h.shape
    tr = min(tr, T)

    def body(h_ref, g_ref, t_ref, loss_ref, dh_ref, dhb_ref, dg_ref):
        i = pl.program_id(0)
        x = h_ref[...]
        r = lax.rsqrt(jnp.mean(x * x, axis=-1, keepdims=True) + EPS)
        xh = x * r
        gg = g_ref[...]
        err = xh * gg - t_ref[...]

        @pl.when(i == 0)
        def _():
            dg_ref[...] = jnp.zeros_like(dg_ref)
            loss_ref[...] = jnp.zeros_like(loss_ref)

        loss_ref[...] += 0.5 * jnp.sum(jnp.mean(err * err, axis=-1, keepdims=True))
        dy = err * (1.0 / D)
        dg_ref[...] += jnp.sum(dy * xh, axis=0, keepdims=True)
        dxh = dy * gg
        dh = r * (dxh - xh * jnp.mean(dxh * xh, axis=-1, keepdims=True))
        dh_ref[...] = dh
        dhb_ref[...] = dh.astype(BF16)

    row = pl.BlockSpec((tr, D), lambda i: (i, 0))
    vec = pl.BlockSpec((1, D), lambda i: (0, 0))
    return pl.pallas_call(
        body, name=name, grid=(T // tr,),
        in_specs=[row, vec, row],
        out_specs=[pl.BlockSpec((8, LANES), lambda i: (0, 0)), row, row, vec],
        out_shape=[jax.ShapeDtypeStruct((8, LANES), F32), jax.ShapeDtypeStruct((T, D), F32),
                   jax.ShapeDtypeStruct((T, D), BF16), jax.ShapeDtypeStruct((1, D), F32)],
        compiler_params=_cparams(("arbitrary",)),
    )(h, g, target)


CONV_HALO = 32


def _glu_ext(prev_ref, main_ref, hs_ref, i, dc, tt):
    up = prev_ref[...].astype(F32)
    hp = up[:, :dc] * jax.nn.sigmoid(up[:, dc:])
    hs_ref[pl.ds(0, CONV_HALO), :] = jnp.where(i > 0, hp, 0.0)
    um = main_ref[...].astype(F32)
    sig = jax.nn.sigmoid(um[:, dc:])
    hs_ref[pl.ds(CONV_HALO, tt), :] = um[:, :dc] * sig


CONV_ROWS = 32


def _phase_copies(src_ref, ph_ref, rows):
    for b in range(1, SUBLANES):
        ph_ref[b - 1, pl.ds(0, rows), :] = src_ref[pl.ds(b, rows), :]


def _tap_rows(src_ref, ph_ref, offset, r0, n):
    a, b = divmod(offset, SUBLANES)
    ref = src_ref if b == 0 else ph_ref.at[b - 1]
    return ref[pl.ds(r0 + a * SUBLANES, n), :]


def _fold_rows(x):
    out = x[0:SUBLANES]
    for r in range(SUBLANES, x.shape[0], SUBLANES):
        out = out + x[r:r + SUBLANES]
    return out


def conv_fwd(proj, conv_w, conv_b, ln_g, ln_b, *, tt=256, name):
    T = proj.shape[0]
    W, DC = conv_w.shape
    tt = min(tt, T)
    hb = tt // CONV_HALO
    ph_rows = tt + CONV_HALO - SUBLANES

    def body(prev_ref, main_ref, w_ref, b_ref, g_ref, bb_ref, s_ref, c_ref, hs_ref, ph_ref):
        i = pl.program_id(0)
        _glu_ext(prev_ref, main_ref, hs_ref, i, DC, tt)
        _phase_copies(hs_ref, ph_ref, ph_rows)

        def chunk(cc, carry):
            r0 = pl.multiple_of(cc * CONV_ROWS, CONV_ROWS)
            c = jnp.zeros((CONV_ROWS, DC), F32) + b_ref[...]
            for k in range(W):
                c = c + w_ref[pl.ds(k, 1), :] * _tap_rows(hs_ref, ph_ref, CONV_HALO - (W - 1) + k, r0, CONV_ROWS)
            rows = pl.ds(r0, CONV_ROWS)
            c_ref[rows, :] = c
            mu = jnp.mean(c, axis=-1, keepdims=True)
            xc = c - mu
            var = jnp.mean(xc * xc, axis=-1, keepdims=True)
            y = xc * lax.rsqrt(var + EPS) * g_ref[...] + bb_ref[...]
            s_ref[rows, :] = (y * jax.nn.sigmoid(y)).astype(s_ref.dtype)
            return carry

        lax.fori_loop(0, tt // CONV_ROWS, chunk, 0)

    vec = pl.BlockSpec((1, DC), lambda i: (0, 0))
    return pl.pallas_call(
        body, name=name, grid=(T // tt,),
        in_specs=[pl.BlockSpec((CONV_HALO, 2 * DC), lambda i: (jnp.maximum(i * hb - 1, 0), 0)),
                  pl.BlockSpec((tt, 2 * DC), lambda i: (i, 0)),
                  pl.BlockSpec((W, DC), lambda i: (0, 0)), vec, vec, vec],
        out_specs=[pl.BlockSpec((tt, DC), lambda i: (i, 0)), pl.BlockSpec((tt, DC), lambda i: (i, 0))],
        out_shape=[jax.ShapeDtypeStruct((T, DC), BF16), jax.ShapeDtypeStruct((T, DC), F32)],
        scratch_shapes=[pltpu.VMEM((tt + CONV_HALO, DC), F32), pltpu.VMEM((SUBLANES - 1, ph_rows, DC), F32)],
        compiler_params=_cparams(("parallel",)),
    )(proj, proj, conv_w, conv_b, ln_g, ln_b)


def conv_bwd_ln(ds, c, ln_g, ln_b, *, tt=256, name):
    T, DC = c.shape
    tt = min(tt, T)

    def body(ds_ref, c_ref, g_ref, bb_ref, dc_ref, dg_ref, db_ref):
        i = pl.program_id(0)
        c = c_ref[...]
        mu = jnp.mean(c, axis=-1, keepdims=True)
        xc = c - mu
        var = jnp.mean(xc * xc, axis=-1, keepdims=True)
        rstd = lax.rsqrt(var + EPS)
        xh = xc * rstd
        y = xh * g_ref[...] + bb_ref[...]
        sg = jax.nn.sigmoid(y)
        dy = ds_ref[...].astype(F32) * (sg * (1.0 + y * (1.0 - sg)))

        @pl.when(i == 0)
        def _():
            dg_ref[...] = jnp.zeros_like(dg_ref)
            db_ref[...] = jnp.zeros_like(db_ref)

        db_ref[...] += jnp.sum(dy, axis=0, keepdims=True)
        dg_ref[...] += jnp.sum(dy * xh, axis=0, keepdims=True)
        dxh = dy * g_ref[...]
        dc_ref[...] = rstd * (dxh - jnp.mean(dxh, axis=-1, keepdims=True)
                              - xh * jnp.mean(dxh * xh, axis=-1, keepdims=True))

    row = pl.BlockSpec((tt, DC), lambda i: (i, 0))
    vec = pl.BlockSpec((1, DC), lambda i: (0, 0))
    return pl.pallas_call(
        body, name=name, grid=(T // tt,),
        in_specs=[row, row, vec, vec], out_specs=[row, vec, vec],
        out_shape=[jax.ShapeDtypeStruct((T, DC), F32), jax.ShapeDtypeStruct((1, DC), F32),
                   jax.ShapeDtypeStruct((1, DC), F32)],
        compiler_params=_cparams(("arbitrary",)),
    )(ds, c, ln_g, ln_b)


def conv_bwd_taps(dc, proj, conv_w, dproj, *, tt=256, name):
    T, DC = dc.shape
    W = conv_w.shape[0]
    tt = min(tt, T)
    hb = tt // CONV_HALO
    n_t = T // tt
    last_halo = T // CONV_HALO - 1
    ph_rows = tt + CONV_HALO - SUBLANES

    def body(dc_ref, dcn_ref, prev_ref, main_ref, w_ref, _, dp_ref, dw_ref, db_ref, hs_ref, ds_ref, hph_ref,
             dph_ref, acc_ref):
        i = pl.program_id(0)
        _glu_ext(prev_ref, main_ref, hs_ref, i, DC, tt)
        ds_ref[pl.ds(0, tt), :] = dc_ref[...]
        ds_ref[pl.ds(tt, CONV_HALO), :] = jnp.where(i < n_t - 1, dcn_ref[...], 0.0)
        _phase_copies(hs_ref, hph_ref, ph_rows)
        _phase_copies(ds_ref, dph_ref, ph_rows)
        acc_ref[...] = jnp.zeros_like(acc_ref)

        def chunk(cc, carry):
            r0 = pl.multiple_of(cc * CONV_ROWS, CONV_ROWS)
            rows = pl.ds(r0, CONV_ROWS)
            d = ds_ref[rows, :]
            dh = jnp.zeros((CONV_ROWS, DC), F32)
            for k in range(W):
                dh = dh + w_ref[pl.ds(k, 1), :] * _tap_rows(ds_ref, dph_ref, W - 1 - k, r0, CONV_ROWS)
                acc_ref[k] += _fold_rows(d * _tap_rows(hs_ref, hph_ref, CONV_HALO - (W - 1) + k, r0, CONV_ROWS))
            acc_ref[W] += _fold_rows(d)
            um = main_ref[rows, :].astype(F32)
            a, sig = um[:, :DC], jax.nn.sigmoid(um[:, DC:])
            dp_ref[rows, :] = jnp.concatenate([dh * sig, dh * a * sig * (1.0 - sig)], axis=1).astype(dp_ref.dtype)
            return carry

        lax.fori_loop(0, tt // CONV_ROWS, chunk, 0)

        @pl.when(i == 0)
        def _():
            dw_ref[...] = jnp.zeros_like(dw_ref)
            db_ref[...] = jnp.zeros_like(db_ref)

        for k in range(W):
            dw_ref[pl.ds(k, 1), :] += jnp.sum(acc_ref[k], axis=0, keepdims=True)
        db_ref[...] += jnp.sum(acc_ref[W], axis=0, keepdims=True)

    return pl.pallas_call(
        body, name=name, grid=(n_t,),
        in_specs=[pl.BlockSpec((tt, DC), lambda i: (i, 0)),
                  pl.BlockSpec((CONV_HALO, DC), lambda i: (jnp.minimum((i + 1) * hb, last_halo), 0)),
                  pl.BlockSpec((CONV_HALO, 2 * DC), lambda i: (jnp.maximum(i * hb - 1, 0), 0)),
                  pl.BlockSpec((tt, 2 * DC), lambda i: (i, 0)),
                  pl.BlockSpec((W, DC), lambda i: (0, 0)),
                  pl.BlockSpec(memory_space=pl.ANY)],
        out_specs=[pl.BlockSpec((tt, 2 * DC), lambda i: (i, 0)),
                   pl.BlockSpec((W, DC), lambda i: (0, 0)), pl.BlockSpec((1, DC), lambda i: (0, 0))],
        out_shape=[jax.ShapeDtypeStruct(dproj.shape, dproj.dtype), jax.ShapeDtypeStruct((W, DC), F32),
                   jax.ShapeDtypeStruct((1, DC), F32)],
        scratch_shapes=[pltpu.VMEM((tt + CONV_HALO, DC), F32), pltpu.VMEM((tt + CONV_HALO, DC), F32),
                        pltpu.VMEM((SUBLANES - 1, ph_rows, DC), F32), pltpu.VMEM((SUBLANES - 1, ph_rows, DC), F32),
                        pltpu.VMEM((W + 1, SUBLANES, DC), F32)],
        input_output_aliases={5: 0},
        compiler_params=_cparams(("arbitrary",)),
    )(dc, dc, proj, proj, conv_w, dproj)


ATT_TQ = 256
ATT_NB = 1 + (LEFT_CHUNKS * CHUNK) // ATT_TQ
ATT_WIN = ATT_NB * ATT_TQ
ATT_PERIOD = 1024
REL_PAD = 640


def _rel_onehot():
    m = lax.broadcasted_iota(jnp.int32, (REL_PAD, ATT_PERIOD), 1)
    r = lax.broadcasted_iota(jnp.int32, (REL_PAD, ATT_PERIOD), 0)
    qk = jnp.where(m < ATT_PERIOD - ATT_TQ, -m, ATT_PERIOD - m)
    idx = jnp.clip(LEFT_CHUNKS * CHUNK + qk, -MAX_REL, MAX_REL) + MAX_REL
    return (idx == r).astype(F32)


def _band_valid():
    q = lax.broadcasted_iota(jnp.int32, (ATT_TQ, ATT_WIN), 0)
    k = lax.broadcasted_iota(jnp.int32, (ATT_TQ, ATT_WIN), 1)
    j = k // CHUNK - q // CHUNK
    return (j >= 0) & (j <= LEFT_CHUNKS)


def attn_bias(rel_pad, *, name):
    H = rel_pad.shape[0]

    def body(rb_ref, o_ref):
        g = jnp.dot(rb_ref[...], _rel_onehot(), preferred_element_type=F32, precision=lax.Precision.HIGHEST)
        valid = _band_valid()
        for h in range(H):
            row = jnp.broadcast_to(g[h:h + 1, :], (ATT_TQ, ATT_PERIOD))
            t = pltpu.roll(row, 0, 1, stride=1, stride_axis=0)
            o_ref[h] = jnp.where(valid, t[:, :ATT_WIN], NEG_INF)

    return pl.pallas_call(
        body, name=name,
        in_specs=[pl.BlockSpec(memory_space=pltpu.VMEM)], out_specs=pl.BlockSpec(memory_space=pltpu.VMEM),
        out_shape=jax.ShapeDtypeStruct((H, ATT_TQ, ATT_WIN), F32),
        compiler_params=_cparams(),
    )(rel_pad)


def attn_bias_bwd(dbias, *, name):
    H = dbias.shape[0]
    band = (LEFT_CHUNKS + 1) * CHUNK

    def body(d_ref, o_ref, acc_ref):
        r = lax.broadcasted_iota(jnp.int32, (CHUNK, CHUNK), 0)
        c = lax.broadcasted_iota(jnp.int32, (CHUNK, CHUNK), 1)
        rev = (r + c == CHUNK - 1).astype(F32)
        pad = jnp.zeros((CHUNK, ATT_PERIOD - ATT_WIN), F32)
        for h in range(H):
            tab = d_ref[h, pl.ds(0, CHUNK), :]
            for cq in range(1, ATT_TQ // CHUNK):
                tab = tab + pltpu.roll(d_ref[h, pl.ds(cq * CHUNK, CHUNK), :], ATT_WIN - cq * CHUNK, 1)
            t = jnp.dot(rev, tab, preferred_element_type=F32, precision=lax.Precision.HIGHEST)
            u = pltpu.roll(jnp.concatenate([t, pad], axis=1), 0, 1, stride=1, stride_axis=0)
            acc_ref[pl.ds(h, 1), :] = jnp.sum(u, axis=0, keepdims=True)
        j = lax.broadcasted_iota(jnp.int32, (REL_PAD, ATT_PERIOD), 1)
        rr = lax.broadcasted_iota(jnp.int32, (REL_PAD, ATT_PERIOD), 0)
        idx = jnp.clip(LEFT_CHUNKS * CHUNK + CHUNK - 1 - j, -MAX_REL, MAX_REL) + MAX_REL
        onehot = ((idx == rr) & (j < band + CHUNK - 1)).astype(F32)
        o_ref[...] = lax.dot_general(acc_ref[...], onehot, (((1,), (1,)), ((), ())),
                                     preferred_element_type=F32, precision=lax.Precision.HIGHEST)

    return pl.pallas_call(
        body, name=name,
        in_specs=[pl.BlockSpec(memory_space=pltpu.VMEM)], out_specs=pl.BlockSpec(memory_space=pltpu.VMEM),
        out_shape=jax.ShapeDtypeStruct((H, REL_PAD), F32),
        scratch_shapes=[pltpu.VMEM((H, ATT_PERIOD), F32)],
        compiler_params=_cparams(),
    )(dbias)


def _attn_specs(T, qcol, dattn):
    dn = dattn // LANES

    def kv(col0, back):
        return pl.BlockSpec((ATT_TQ, LANES), lambda hp, i: (jnp.maximum(i - back, 0), col0 // LANES + hp))

    q = pl.BlockSpec((ATT_TQ, LANES), lambda hp, i: (i, qcol // LANES + hp))
    ks = [kv(qcol + dattn, ATT_NB - 1 - b) for b in range(ATT_NB)]
    vs = [kv(qcol + 2 * dattn, ATT_NB - 1 - b) for b in range(ATT_NB)]
    return q, ks, vs


def _attn_scores(q, kw, bias, i):
    s = lax.dot_general(q, kw, (((1,), (1,)), ((), ())), preferred_element_type=F32)
    s = s * (ATTN_HEAD_DIM ** -0.5) + bias
    col = lax.broadcasted_iota(jnp.int32, s.shape, 1)
    s = jnp.where(col // ATT_TQ + i >= ATT_NB - 1, s, NEG_INF)
    s = s - jnp.max(s, axis=-1, keepdims=True)
    p = jnp.exp(s)
    return p * (1.0 / jnp.sum(p, axis=-1, keepdims=True))


def attn_fwd(proj, bias, *, qcol, dattn, name):
    T = proj.shape[0]
    HP = dattn // LANES
    q_spec, k_specs, v_specs = _attn_specs(T, qcol, dattn)

    def body(*refs):
        q_ref = refs[0]
        k_refs = refs[1:1 + ATT_NB]
        v_refs = refs[1 + ATT_NB:1 + 2 * ATT_NB]
        b_ref, o_ref = refs[1 + 2 * ATT_NB:]
        i = pl.program_id(1)
        lane = lax.broadcasted_iota(jnp.int32, (ATT_TQ, LANES), 1)
        kw = jnp.concatenate([r[...] for r in k_refs], axis=0)
        vw = jnp.concatenate([r[...] for r in v_refs], axis=0)
        q = q_ref[...]
        out = jnp.zeros((ATT_TQ, LANES), F32)
        for hh in range(2):
            mine = (lane // ATTN_HEAD_DIM) == hh
            p = _attn_scores(jnp.where(mine, q, jnp.zeros_like(q)), kw, b_ref[hh], i)
            o = jnp.dot(p.astype(BF16), vw, preferred_element_type=F32)
            out = jnp.where(mine, o, out)
        o_ref[...] = out.astype(o_ref.dtype)

    return pl.pallas_call(
        body, name=name, grid=(HP, T // ATT_TQ),
        in_specs=[q_spec, *k_specs, *v_specs,
                  pl.BlockSpec((2, ATT_TQ, ATT_WIN), lambda hp, i: (hp, 0, 0))],
        out_specs=pl.BlockSpec((ATT_TQ, LANES), lambda hp, i: (i, hp)),
        out_shape=jax.ShapeDtypeStruct((T, dattn), BF16),
        compiler_params=_cparams(("parallel", "parallel")),
    )(*([proj] * (1 + 2 * ATT_NB)), bias)


def attn_bwd(proj, bias, dout, dproj, *, qcol, dattn, name):
    T = proj.shape[0]
    HP = dattn // LANES
    nq = T // ATT_TQ
    q_spec, k_specs, v_specs = _attn_specs(T, qcol, dattn)

    def body(*refs):
        q_ref = refs[0]
        k_refs = refs[1:1 + ATT_NB]
        v_refs = refs[1 + ATT_NB:1 + 2 * ATT_NB]
        b_ref, do_ref, _, dq_ref, dk_ref, dv_ref, db_ref = refs[1 + 2 * ATT_NB:]
        i = pl.program_id(1)
        lane = lax.broadcasted_iota(jnp.int32, (ATT_TQ, LANES), 1)
        kw = jnp.concatenate([r[...] for r in k_refs], axis=0)
        vw = jnp.concatenate([r[...] for r in v_refs], axis=0)
        q = q_ref[...]
        do = do_ref[...]
        dq = jnp.zeros((ATT_TQ, LANES), F32)
        dk = jnp.zeros((ATT_WIN, LANES), F32)
        dv = jnp.zeros((ATT_WIN, LANES), F32)

        @pl.when(i == 0)
        def _():
            db_ref[...] = jnp.zeros_like(db_ref)

        for hh in range(2):
            mine = (lane // ATTN_HEAD_DIM) == hh
            qh = jnp.where(mine, q, jnp.zeros_like(q))
            doh = jnp.where(mine, do, jnp.zeros_like(do))
            p = _attn_scores(qh, kw, b_ref[hh], i)
            dp = lax.dot_general(doh, vw, (((1,), (1,)), ((), ())), preferred_element_type=F32)
            dv = dv + lax.dot_general(p.astype(BF16), doh, (((0,), (0,)), ((), ())), preferred_element_type=F32)
            ds = p * (dp - jnp.sum(dp * p, axis=-1, keepdims=True))
            db_ref[hh] += ds
            dsb = (ds * (ATTN_HEAD_DIM ** -0.5)).astype(BF16)
            dq = jnp.where(mine, jnp.dot(dsb, kw, preferred_element_type=F32), dq)
            dk = dk + lax.dot_general(dsb, qh, (((0,), (0,)), ((), ())), preferred_element_type=F32)
        dq_ref[...] = dq.astype(dq_ref.dtype)
        dk_ref[...] = dk
        dv_ref[...] = dv

    win = pl.BlockSpec((None, ATT_WIN, LANES), lambda hp, i: (i, 0, hp))
    return pl.pallas_call(
        body, name=name, grid=(HP, nq),
        in_specs=[q_spec, *k_specs, *v_specs,
                  pl.BlockSpec((2, ATT_TQ, ATT_WIN), lambda hp, i: (hp, 0, 0)),
                  pl.BlockSpec((ATT_TQ, LANES), lambda hp, i: (i, hp)),
                  pl.BlockSpec(memory_space=pl.ANY)],
        out_specs=[pl.BlockSpec((ATT_TQ, LANES), lambda hp, i: (i, qcol // LANES + hp)), win, win,
                   pl.BlockSpec((2, ATT_TQ, ATT_WIN), lambda hp, i: (hp, 0, 0))],
        out_shape=[jax.ShapeDtypeStruct(dproj.shape, dproj.dtype),
                   jax.ShapeDtypeStruct((nq, ATT_WIN, dattn), F32), jax.ShapeDtypeStruct((nq, ATT_WIN, dattn), F32),
                   jax.ShapeDtypeStruct((2 * HP, ATT_TQ, ATT_WIN), F32)],
        input_output_aliases={3 + 2 * ATT_NB: 0},
        compiler_params=_cparams(("arbitrary", "arbitrary")),
    )(*([proj] * (1 + 2 * ATT_NB)), bias, dout, dproj)


def window_sum(win, dproj, *, col, name):
    nq, _, C = win.shape
    assert col % C == 0

    def body(*refs):
        w_refs = refs[:ATT_NB]
        o_ref = refs[ATT_NB + 1]
        j = pl.program_id(0)
        acc = w_refs[0][...]
        for b in range(1, ATT_NB):
            acc = acc + jnp.where(j + b < nq, w_refs[b][...], 0.0)
        o_ref[...] = acc.astype(o_ref.dtype)

    def part(b):
        return pl.BlockSpec((None, ATT_TQ, C), lambda j: (jnp.minimum(j + b, nq - 1), ATT_NB - 1 - b, 0))

    return pl.pallas_call(
        body, name=name, grid=(nq,),
        in_specs=[part(b) for b in range(ATT_NB)] + [pl.BlockSpec(memory_space=pl.ANY)],
        out_specs=pl.BlockSpec((ATT_TQ, C), lambda j: (j, col // C)),
        out_shape=jax.ShapeDtypeStruct(dproj.shape, dproj.dtype),
        input_output_aliases={ATT_NB: 0},
        compiler_params=_cparams(("arbitrary",)),
    )(*([win] * ATT_NB), dproj)


def _mem_probs(q, km, scale):
    s = lax.dot_general(q, km, (((1,), (1,)), ((), ())), preferred_element_type=F32) * scale
    s = s - jnp.max(s, axis=-1, keepdims=True)
    p = jnp.exp(s)
    return p * (1.0 / jnp.sum(p, axis=-1, keepdims=True))


def memattn_fwd(proj, kv, *, qcol, tq=512, name):
    T = proj.shape[0]
    M, dm2 = kv.shape
    DM = dm2 // 2
    hd = DM // N_MEM_HEADS
    tq = min(tq, T)

    def body(q_ref, kv_ref, o_ref):
        for h in range(N_MEM_HEADS):
            sl = pl.ds(h * hd, hd)
            p = _mem_probs(q_ref[:, sl], kv_ref[:, sl], hd ** -0.5)
            o = jnp.dot(p.astype(BF16), kv_ref[:, pl.ds(DM + h * hd, hd)], preferred_element_type=F32)
            o_ref[:, sl] = o.astype(o_ref.dtype)

    return pl.pallas_call(
        body, name=name, grid=(T // tq,),
        in_specs=[pl.BlockSpec((tq, DM), lambda i: (i, qcol // DM)),
                  pl.BlockSpec((M, 2 * DM), lambda i: (0, 0))],
        out_specs=pl.BlockSpec((tq, DM), lambda i: (i, 0)),
        out_shape=jax.ShapeDtypeStruct((T, DM), BF16),
        compiler_params=_cparams(("parallel",)),
    )(proj, kv)


def memattn_bwd(proj, kv, dout, dproj, *, qcol, tq=512, name):
    T = proj.shape[0]
    M, dm2 = kv.shape
    DM = dm2 // 2
    hd = DM // N_MEM_HEADS
    tq = min(tq, T)

    def body(q_ref, kv_ref, do_ref, _, dq_ref, dkv_ref):
        i = pl.program_id(0)

        @pl.when(i == 0)
        def _():
            dkv_ref[...] = jnp.zeros_like(dkv_ref)

        for h in range(N_MEM_HEADS):
            sl = pl.ds(h * hd, hd)
            vsl = pl.ds(DM + h * hd, hd)
            q = q_ref[:, sl]
            do = do_ref[:, sl]
            p = _mem_probs(q, kv_ref[:, sl], hd ** -0.5)
            dp = lax.dot_general(do, kv_ref[:, vsl], (((1,), (1,)), ((), ())), preferred_element_type=F32)
            dkv_ref[:, vsl] += lax.dot_general(p.astype(BF16), do, (((0,), (0,)), ((), ())),
                                               preferred_element_type=F32)
            ds = p * (dp - jnp.sum(dp * p, axis=-1, keepdims=True))
            dsb = (ds * (hd ** -0.5)).astype(BF16)
            dq_ref[:, sl] = jnp.dot(dsb, kv_ref[:, sl], preferred_element_type=F32).astype(dq_ref.dtype)
            dkv_ref[:, sl] += lax.dot_general(dsb, q, (((0,), (0,)), ((), ())), preferred_element_type=F32)

    return pl.pallas_call(
        body, name=name, grid=(T // tq,),
        in_specs=[pl.BlockSpec((tq, DM), lambda i: (i, qcol // DM)),
                  pl.BlockSpec((M, 2 * DM), lambda i: (0, 0)),
                  pl.BlockSpec((tq, DM), lambda i: (i, 0)),
                  pl.BlockSpec(memory_space=pl.ANY)],
        out_specs=[pl.BlockSpec((tq, DM), lambda i: (i, qcol // DM)),
                   pl.BlockSpec((M, 2 * DM), lambda i: (0, 0))],
        out_shape=[jax.ShapeDtypeStruct(dproj.shape, dproj.dtype), jax.ShapeDtypeStruct((M, 2 * DM), F32)],
        input_output_aliases={3: 0},
        compiler_params=_cparams(("arbitrary",)),
    )(proj, kv, dout, dproj)


def merge_fwd(proj, gate_b, ys, *, gcol, tr=512, tc=512, name):
    T = proj.shape[0]
    D = ys[0].shape[1]
    tr, tc = min(tr, T), _tile(D, tc)
    nd = D // tc

    def body(g0, g1, g2, b0, b1, b2, y0, y1, y2, o_ref):
        acc = jnp.zeros((tr, tc), F32)
        for g, b, y in ((g0, b0, y0), (g1, b1, y1), (g2, b2, y2)):
            acc = acc + jax.nn.sigmoid(g[...].astype(F32) + b[...]) * y[...].astype(F32)
        o_ref[...] = acc.astype(o_ref.dtype)

    def gate(b):
        return pl.BlockSpec((tr, tc), lambda i, j: (i, gcol // tc + b * nd + j))

    def bias(b):
        return pl.BlockSpec((1, tc), lambda i, j: (0, b * nd + j))

    blk = pl.BlockSpec((tr, tc), lambda i, j: (i, j))
    return pl.pallas_call(
        body, name=name, grid=(T // tr, nd),
        in_specs=[gate(0), gate(1), gate(2), bias(0), bias(1), bias(2), blk, blk, blk],
        out_specs=blk,
        out_shape=jax.ShapeDtypeStruct((T, D), BF16),
        compiler_params=_cparams(("parallel", "parallel")),
    )(proj, proj, proj, gate_b, gate_b, gate_b, *ys)


def merge_bwd(dmerged, proj, gate_b, ys, *, gcol, tr=512, tc=512, name):
    T, D_IN = proj.shape
    D = ys[0].shape[1]
    tr, tc = min(tr, T), _tile(D, tc)
    nd = D // tc

    def body(dm_ref, g_ref, b_ref, y0, y1, y2, dp_ref, dy_ref, db_ref):
        br = pl.program_id(0)
        i = pl.program_id(2)
        y = jnp.where(br == 0, y0[...], jnp.where(br == 1, y1[...], y2[...])).astype(F32)
        dm = dm_ref[...].astype(F32)
        sg = jax.nn.sigmoid(g_ref[...].astype(F32) + b_ref[...])
        dy_ref[...] = (dm * sg).astype(dy_ref.dtype)
        dg = dm * y * sg * (1.0 - sg)
        dp_ref[...] = dg.astype(dp_ref.dtype)

        @pl.when(i == 0)
        def _():
            db_ref[...] = jnp.zeros_like(db_ref)

        db_ref[...] += jnp.sum(dg, axis=0, keepdims=True)

    def ysp(b):
        return pl.BlockSpec((tr, tc), lambda br, j, i: (jnp.where(br == b, i, 0), jnp.where(br == b, j, 0)))

    return pl.pallas_call(
        body, name=name, grid=(3, nd, T // tr),
        in_specs=[pl.BlockSpec((tr, tc), lambda br, j, i: (i, j)),
                  pl.BlockSpec((tr, tc), lambda br, j, i: (i, gcol // tc + br * nd + j)),
                  pl.BlockSpec((1, tc), lambda br, j, i: (0, br * nd + j)),
                  ysp(0), ysp(1), ysp(2)],
        out_specs=[pl.BlockSpec((tr, tc), lambda br, j, i: (i, gcol // tc + br * nd + j)),
                   pl.BlockSpec((None, tr, tc), lambda br, j, i: (br, i, j)),
                   pl.BlockSpec((1, tc), lambda br, j, i: (0, br * nd + j))],
        out_shape=[jax.ShapeDtypeStruct((T, D_IN), BF16), jax.ShapeDtypeStruct((3, T, D), BF16),
                   jax.ShapeDtypeStruct((1, 3 * D), F32)],
        compiler_params=_cparams(("arbitrary", "arbitrary", "arbitrary")),
    )(dmerged, proj, gate_b, *ys)


FFN_CHUNK = 16
FFN_ROWS = 32


def _shift_down(tail, x, s):
    return pltpu.roll(jnp.concatenate([tail, x], axis=0), s, 0)[SUBLANES:]


def _shift_up(x, head, s):
    n = x.shape[0]
    return pltpu.roll(jnp.concatenate([x, head], axis=0), n + SUBLANES - s, 0)[:n]


def _ffn_taps(w_ref, b_ref):
    return [w_ref[pl.ds(k, 1), :] for k in range(FFN_CONV_WIDTH)] + [b_ref[...]]


def _ffn_conv(tail, x, taps):
    w0, w1, w2, b = taps
    x1, x2 = _shift_down(tail, x, 1), _shift_down(tail, x, 2)
    return w0 * x2 + w1 * x1 + w2 * x + b, x1, x2


def _ffn_rows(c):
    return pl.ds(c * FFN_ROWS if isinstance(c, int) else pl.multiple_of(c * FFN_ROWS, FFN_ROWS), FFN_ROWS)


def _ffn_specs(tt, tc, nv, order, with_next):
    hb = tt // FFN_CHUNK

    def mk(shape, f):
        return pl.BlockSpec(shape, (lambda i, j: f(i, j)) if order == "ij" else (lambda j, i: f(i, j)))

    def halo_after(T):
        return lambda i, j, off: (jnp.minimum((i + 1) * hb, T // FFN_CHUNK - 1), off + j)

    out = []
    for off in (0, nv):
        out += [mk((FFN_CHUNK, tc), lambda i, j, off=off: (jnp.maximum(i * hb - 1, 0), off + j)),
                mk((tt, tc), lambda i, j, off=off: (i, off + j))]
        if with_next:
            out.append(mk((FFN_CHUNK, tc), lambda i, j, off=off: halo_after(with_next)(i, j, off)))
    out += [mk((FFN_CONV_WIDTH, tc), lambda i, j, off=off: (0, off + j)) for off in (0, nv)]
    out += [mk((1, tc), lambda i, j, off=off: (0, off + j)) for off in (0, nv)]
    return out


def ffn_act_fwd(up0, w, b, *, tt=1024, tc=256, name):
    T, dff2 = up0.shape
    DFF = dff2 // 2
    tt, tc = min(tt, T), _tile(DFF, tc)
    nv = DFF // tc

    def body(vp, vm, gp, gm, wv, wg, bv, bg, o_ref):
        i = pl.program_id(0)
        taps_v, taps_g = _ffn_taps(wv, bv), _ffn_taps(wg, bg)

        def chunk(c, tails):
            rows = _ffn_rows(c)
            xv, xg = vm[rows, :].astype(F32), gm[rows, :].astype(F32)
            val = _ffn_conv(tails[0], xv, taps_v)[0]
            gt = _ffn_conv(tails[1], xg, taps_g)[0]
            o_ref[rows, :] = (gt * jax.nn.sigmoid(gt) * val).astype(o_ref.dtype)
            return xv[-SUBLANES:], xg[-SUBLANES:]

        before = lambda ref: jnp.where(i > 0, ref[...].astype(F32)[-SUBLANES:], 0.0)
        lax.fori_loop(0, tt // FFN_ROWS, chunk, (before(vp), before(gp)))

    return pl.pallas_call(
        body, name=name, grid=(T // tt, nv),
        in_specs=_ffn_specs(tt, tc, nv, "ij", None),
        out_specs=pl.BlockSpec((tt, tc), lambda i, j: (i, j)),
        out_shape=jax.ShapeDtypeStruct((T, DFF), BF16),
        compiler_params=_cparams(("parallel", "parallel")),
    )(up0, up0, up0, up0, w, w, b, b)


def ffn_bwd(dact, up0, w, b, *, tt=1024, tc=256, name):
    T, dff2 = up0.shape
    DFF = dff2 // 2
    tt, tc = min(tt, T), _tile(DFF, tc)
    nv = DFF // tc
    hb = tt // FFN_CHUNK
    n_t = T // tt
    n = tt // FFN_ROWS

    def body(da, dan, vp, vm, vn, gp, gm, gn, wv, wg, bv, bg, o_ref, dw_ref, db_ref, acc):
        i = pl.program_id(1)
        taps_v, taps_g = _ffn_taps(wv, bv), _ffn_taps(wg, bg)
        f32 = lambda ref, rows=slice(None): ref[rows, :].astype(F32)

        def act_grads(d_act, tail_v, xv, tail_g, xg):
            val, v1, v2 = _ffn_conv(tail_v, xv, taps_v)
            gt, g1, g2 = _ffn_conv(tail_g, xg, taps_g)
            sg = jax.nn.sigmoid(gt)
            return d_act * gt * sg, d_act * val * (sg * (1.0 + gt * (1.0 - sg))), (xv, v1, v2), (xg, g1, g2)

        tail = lambda ref, end: f32(ref, pl.ds(end - FFN_CHUNK, FFN_CHUNK))[-SUBLANES:]
        dnv, dng, _, _ = act_grads(f32(dan), tail(vm, tt), f32(vn), tail(gm, tt), f32(gn))
        heads = (jnp.where(i < n_t - 1, dnv[:SUBLANES], 0.0), jnp.where(i < n_t - 1, dng[:SUBLANES], 0.0))
        acc[...] = jnp.zeros_like(acc)

        def emit(c, tail_v, tail_g, heads):
            rows = _ffn_rows(c)
            dv, dg, xs_v, xs_g = act_grads(f32(da, rows), tail_v, f32(vm, rows), tail_g, f32(gm, rows))
            for half, (d, head, xs, taps) in enumerate(((dv, heads[0], xs_v, taps_v), (dg, heads[1], xs_g, taps_g))):
                w0, w1, w2, _ = taps
                o_ref[half, rows, :] = (w2 * d + w1 * _shift_up(d, head, 1) + w0 * _shift_up(d, head, 2)
                                        ).astype(o_ref.dtype)
                for k in range(FFN_CONV_WIDTH):
                    acc[4 * half + k] += d * xs[FFN_CONV_WIDTH - 1 - k]
                acc[4 * half + 3] += d
            return dv[:SUBLANES], dg[:SUBLANES]

        def chunk(k, heads):
            c = n - 1 - k
            end = pl.multiple_of(c * FFN_ROWS, FFN_ROWS)
            return emit(c, tail(vm, end), tail(gm, end), heads)

        heads = lax.fori_loop(0, n - 1, chunk, heads)
        before = lambda ref: jnp.where(i > 0, f32(ref)[-SUBLANES:], 0.0)
        emit(0, before(vp), before(gp), heads)

        @pl.when(i == 0)
        def _():
            dw_ref[...] = jnp.zeros_like(dw_ref)
            db_ref[...] = jnp.zeros_like(db_ref)

        for half in range(2):
            for k in range(FFN_CONV_WIDTH):
                dw_ref[half, pl.ds(k, 1), :] += jnp.sum(acc[4 * half + k], axis=0, keepdims=True)
            db_ref[half] += jnp.sum(acc[4 * half + 3], axis=0, keepdims=True)

    last_halo = T // FFN_CHUNK - 1
    return pl.pallas_call(
        body, name=name, grid=(nv, n_t),
        in_specs=[pl.BlockSpec((tt, tc), lambda j, i: (i, j)),
                  pl.BlockSpec((FFN_CHUNK, tc), lambda j, i: (jnp.minimum((i + 1) * hb, last_halo), j))]
                 + _ffn_specs(tt, tc, nv, "ji", T),
        out_specs=[pl.BlockSpec((2, tt, tc), lambda j, i: (0, i, j)),
                   pl.BlockSpec((2, FFN_CONV_WIDTH, tc), lambda j, i: (0, 0, j)),
                   pl.BlockSpec((2, 1, tc), lambda j, i: (0, 0, j))],
        out_shape=[jax.ShapeDtypeStruct((2, T, DFF), BF16), jax.ShapeDtypeStruct((2, FFN_CONV_WIDTH, DFF), F32),
                   jax.ShapeDtypeStruct((2, 1, DFF), F32)],
        scratch_shapes=[pltpu.VMEM((8, FFN_ROWS, tc), F32)],
        compiler_params=_cparams(("arbitrary", "arbitrary")),
    )(dact, dact, up0, up0, up0, up0, up0, up0, w, w, b, b)


def adamw(contribs, w, m, v, *, name):
    L, R, C = w.shape
    assert len(contribs) == L and all(c.shape == (N_DEV, R, C) for c in contribs)
    tr = R
    if R * C > 256 * 1024 and R % 8 == 0:
        tr = 8
        while R % (tr * 2) == 0 and tr * 2 * C <= 256 * 1024:
            tr *= 2
    c1 = 1.0 - ADAM_B1 ** ADAM_STEP
    c2 = 1.0 - ADAM_B2 ** ADAM_STEP

    def body(*refs):
        c_refs = refs[:L]
        w_ref, m_ref, v_ref, g_ref, d_ref, nm_ref, nv_ref = refs[L:]
        layer = pl.program_id(0)
        for lp in range(L):
            @pl.when(layer == lp)
            def _(c_ref=c_refs[lp]):
                g = c_ref[0].astype(F32)
                for s in range(1, N_DEV):
                    g = g + c_ref[s].astype(F32)
                g_ref[...] = g

        g = g_ref[...]
        nm = ADAM_B1 * m_ref[...] + (1.0 - ADAM_B1) * g
        nv = ADAM_B2 * v_ref[...] + (1.0 - ADAM_B2) * (g * g)
        nm_ref[...] = nm
        nv_ref[...] = nv
        d_ref[...] = -ADAM_LR * ((nm / c1) / (jnp.sqrt(nv / c2) + ADAM_EPS) + ADAM_WD * w_ref[...])

    def contrib_spec(lp):
        return pl.BlockSpec((N_DEV, tr, C), lambda l, i: (0, jnp.where(l == lp, i, 0), 0))

    blk = pl.BlockSpec((None, tr, C), lambda l, i: (l, i, 0))
    out = jax.ShapeDtypeStruct((L, R, C), F32)
    return pl.pallas_call(
        body, name=name, grid=(L, R // tr),
        in_specs=[contrib_spec(lp) for lp in range(L)] + [blk, blk, blk],
        out_specs=[blk, blk, blk, blk], out_shape=[out, out, out, out],
        compiler_params=_cparams(("arbitrary", "arbitrary")),
    )(*contribs, w, m, v)


def _my_position():
    x, y, c = lax.axis_index("x"), lax.axis_index("y"), lax.axis_index("c")
    return x, y, c, 4 * x + 2 * y + c


def _peers(x, y, c):
    out = []
    for r in range(1, N_DEV):
        px = 1 - x if r & 4 else x
        py = 1 - y if r & 2 else y
        pc = 1 - c if r & 1 else c
        out.append(((px, py, pc), 4 * px + 2 * py + pc))
    return out


def _run_exchange(plan, n, send_sems, recv_sems, local_sems):
    x, y, c, me = _my_position()
    copies = []
    for t in range(n):
        src, dst = plan(t, None, me)
        own = pltpu.make_async_copy(src, dst, local_sems.at[t])
        own.start()
        copies.append(own)
    remote = []
    for r, (peer, peer_index) in enumerate(_peers(x, y, c)):
        for t in range(n):
            src, dst = plan(t, peer_index, me)
            cp = pltpu.make_async_remote_copy(src_ref=src, dst_ref=dst, send_sem=send_sems.at[t, r],
                                              recv_sem=recv_sems.at[t, r], device_id=peer,
                                              device_id_type=pl.DeviceIdType.MESH)
            cp.start()
            remote.append(cp)
    for cp in remote:
        cp.wait_send()
    for cp in remote:
        cp.wait_recv()
    for cp in copies:
        cp.wait()


def all_gather(blocks, *, name):
    n = len(blocks)

    def body(*refs):
        srcs, outs = refs[:n], refs[n:2 * n]
        send_sems, recv_sems, local_sems = refs[2 * n:]
        _run_exchange(lambda t, peer_index, me: (srcs[t], outs[t].at[me]), n, send_sems, recv_sems, local_sems)

    hbm = pl.BlockSpec(memory_space=pl.ANY)
    return pl.pallas_call(
        body, name=name, in_specs=[hbm] * n, out_specs=[hbm] * n,
        out_shape=[jax.ShapeDtypeStruct((N_DEV, *b.shape), b.dtype) for b in blocks],
        scratch_shapes=[pltpu.SemaphoreType.DMA((n, N_DEV - 1)), pltpu.SemaphoreType.DMA((n, N_DEV - 1)),
                        pltpu.SemaphoreType.DMA((n,))],
    )(*blocks)


_RELATIONS = {"scatter": (1, 2, 3, 4, 5, 6, 7), "gather": (1, 2, 4, 6), "forward": (2, 4, 6)}

_HBM = pl.BlockSpec(memory_space=pltpu.HBM)
_SEM = pl.BlockSpec(memory_space=pltpu.SEMAPHORE)


def _push_copies(kind, src_refs, land_refs, send_sems, recv_sems):
    x, y, c, me = _my_position()
    peers = _peers(x, y, c)
    relations = _RELATIONS[kind]
    copies = []
    for k, r in enumerate(relations):
        peer, peer_index = peers[r - 1]
        for t, land in enumerate(land_refs):
            if kind == "forward":
                src, dst, peer = land.at[peer_index], land.at[peer_index], peers[0][0]
            else:
                src = src_refs[t] if kind == "gather" else src_refs[t].at[peer_index]
                dst = land.at[me]
            copies.append(pltpu.make_async_remote_copy(
                src_ref=src, dst_ref=dst, send_sem=send_sems.at[t * len(relations) + k],
                recv_sem=recv_sems.at[t * len(relations) + k], device_id=peer,
                device_id_type=pl.DeviceIdType.MESH))
    return copies


def push_start(srcs, lands, *, kind, name):
    ns, n = len(srcs), len(lands)

    def body(*refs):
        send_sems, recv_sems = refs[ns + n:ns + n + 2]
        token = refs[-1]
        for cp in _push_copies(kind, refs[:ns], refs[ns:ns + n], send_sems, recv_sems):
            cp.start()
        token[...] = jnp.zeros_like(token)

    sems = pltpu.SemaphoreType.DMA((n * len(_RELATIONS[kind]),))
    arrays = [*srcs, *lands]
    out = pl.pallas_call(
        body, name=name,
        out_shape=(sems, sems, *[pltpu.HBM(a.shape, a.dtype) for a in arrays],
                   jax.ShapeDtypeStruct((8, LANES), F32)),
        in_specs=[_HBM] * (ns + n),
        out_specs=(_SEM, _SEM, *[_HBM] * (ns + n), pl.BlockSpec(memory_space=pltpu.VMEM)),
        input_output_aliases={i: 2 + i for i in range(ns + n)},
        compiler_params=pltpu.CompilerParams(has_side_effects=pltpu.SideEffectType.DATAFLOW_SIDE_EFFECTING),
    )(*[pltpu.with_memory_space_constraint(a, pltpu.HBM) for a in arrays])
    return out[0], out[1], list(out[2:2 + ns]), list(out[2 + ns:2 + ns + n]), out[-1]


def push_wait(send_sems, recv_sems, srcs, lands, after, *, kind, name):
    ns, n = len(srcs), len(lands)

    def body(*refs):
        for cp in _push_copies(kind, refs[:ns], refs[ns:ns + n], refs[ns + n], refs[ns + n + 1]):
            cp.wait_send()
            cp.wait_recv()

    arrays = [*srcs, *lands]
    out = pl.pallas_call(
        body, name=name,
        out_shape=tuple(pltpu.HBM(a.shape, a.dtype) for a in arrays),
        in_specs=[_HBM] * (ns + n) + [_SEM, _SEM, pl.BlockSpec(memory_space=pl.ANY)],
        out_specs=tuple([_HBM] * (ns + n)),
        input_output_aliases={i: i for i in range(ns + n)},
        compiler_params=pltpu.CompilerParams(has_side_effects=pltpu.SideEffectType.DATAFLOW_SIDE_EFFECTING),
    )(*arrays, send_sems, recv_sems, after)
    return list(out[ns:])


def _own_slot(block, me):
    zone = lax.empty((N_DEV, *block.shape), block.dtype)
    return lax.dynamic_update_slice(zone, block[None], (me,) + (0,) * block.ndim)


WEIGHT_NAMES = ('mix_norm_g', 'mem_norm_g', 'w_in', 'gate_b', 'conv_w', 'conv_b', 'conv_ln_g', 'conv_ln_b',
                'w_conv_out', 'rel_bias', 'w_attn_out', 'w_mem_kv', 'w_mem_out', 'w_o', 'ffn_norm_g', 'w_up',
                'ffn_conv_w', 'ffn_conv_b', 'w_down', 'final_norm_g')
BIG = (('w_in', False), ('w_conv_out', False), ('w_attn_out', False), ('w_mem_out', False), ('w_up', False),
       ('w_mem_kv', True), ('w_o', True), ('w_down', True))
BY_ROWS = dict(BIG)
GATHER_GROUPS = (('w_in',), ('w_mem_kv', 'w_conv_out', 'w_attn_out', 'w_mem_out', 'w_o'), ('w_up', 'w_down'))


MIN_SLOT_COLS = 512


def _as_matrix(gathered, by_rows):
    n, r, c = gathered.shape
    if by_rows:
        return gathered.reshape(1, n * r, c)
    if c >= MIN_SLOT_COLS:
        return gathered
    return jnp.transpose(gathered, (1, 0, 2)).reshape(1, r, n * c)


def _as_shards(grad, by_rows):
    s, r, c = grad.shape
    if by_rows:
        return grad.reshape(N_DEV, r // N_DEV, c)
    if s == N_DEV:
        return grad
    return jnp.transpose(grad.reshape(r, N_DEV, c // N_DEV), (1, 0, 2))


class _LayerWeights:
    def __init__(self, pending):
        self.pending = pending
        self.ready = {}

    def advance(self, after, name=None):
        for item in self.pending:
            if item['trip'] == 1 and (name is None or name in item['names']):
                lands = push_wait(*item['started'][:4], after, kind="gather", name=item['tag'] + "_wait1")
                item['started'] = push_start([], lands, kind="forward", name=item['tag'] + "_start2")
                item['trip'] = 2

    def get(self, name, after):
        self.advance(after, name)
        for item in self.pending:
            if name in item['names'] and item['trip'] == 2:
                got = push_wait(*item['started'][:4], after, kind="forward", name=item['tag'] + "_wait2")
                self.ready.update({n: _as_matrix(a, BY_ROWS[n]) for n, a in zip(item['names'], got)})
                item['trip'] = None
        return self.ready[name]

    def __getitem__(self, name):
        return self.ready[name]


def _forward_layer(h, mem2, p, l, dims, prefetch):
    W = p['gathered'][l]
    row = lambda name: p[name][l:l + 1]
    sv = {'h': h}
    sv['xn'] = rms_fwd(h, row('mix_norm_g'), name="mix_norm")
    w_in = W.get('w_in', sv['xn'])
    sv['proj'] = mm_nn(sv['xn'], w_in, out_dtype=BF16, after=prefetch(w_in), name="w_in")
    sv['s'], sv['c'] = conv_fwd(sv['proj'], p['conv_w_full'][l], row('conv_b'), row('conv_ln_g'),
                                row('conv_ln_b'), name="conv_module")
    sv['bias'] = attn_bias(p['rel_pad'][l], name="attn_bias")
    sv['att'] = attn_fwd(sv['proj'], sv['bias'], qcol=dims['qcol'], dattn=dims['DA'], name="chunk_attn")
    sv['mn'] = rms_fwd(mem2, row('mem_norm_g'), name="mem_norm")
    sv['kv'] = mm_nn(sv['mn'], W.get('w_mem_kv', sv['att']), out_dtype=BF16, name="w_mem_kv")
    sv['mo'] = memattn_fwd(sv['proj'], sv['kv'], qcol=dims['mcol'], name="mem_attn")
    sv['ys'] = (mm_nn(sv['s'], W['w_conv_out'], out_dtype=BF16, name="w_conv_out"),
                mm_nn(sv['att'], W['w_attn_out'], out_dtype=BF16, name="w_attn_out"),
                mm_nn(sv['mo'], W['w_mem_out'], out_dtype=BF16, name="w_mem_out"))
    sv['merged'] = merge_fwd(sv['proj'], row('gate_b'), sv['ys'], gcol=dims['gcol'], name="merge")
    sv['h1'] = mm_nn(sv['merged'], W['w_o'], out_dtype=F32, residual=h, name="w_o")
    sv['hn'] = rms_fwd(sv['h1'], row('ffn_norm_g'), name="ffn_norm")
    sv['up0'] = mm_nn(sv['hn'], W.get('w_up', sv['hn']), out_dtype=BF16, name="w_up")
    sv['act'] = ffn_act_fwd(sv['up0'], p['ffn_conv_w_full'][l], row('ffn_conv_b'), name="ffn_act")
    if l + 1 < len(p['gathered']):
        p['gathered'][l + 1].advance(sv['act'])
    h2 = mm_nn(sv['act'], W['w_down'], out_dtype=F32, residual=sv['h1'], name="w_down")
    return h2, sv


SCATTER_GROUPS = (('w_down', 'w_up'), ('w_o', 'w_conv_out', 'w_attn_out', 'w_mem_out', 'w_mem_kv'), ('w_in',))


def _backward_layer(dh, dhb, mem2, sv, p, l, dims, scatter, after=None):
    W = p['gathered'][l]
    row = lambda name: p[name][l:l + 1]
    g, small = {}, {}
    dact = mm_nt(dhb, W['w_down'], out_dtype=BF16, after=after, name="d_act")
    g['w_down'] = mm_tn(sv['act'], dhb, slots=1, out_dtype=BF16, name="g_w_down")
    dup0, dtaps, dbias = ffn_bwd(dact, sv['up0'], p['ffn_conv_w_full'][l], row('ffn_conv_b'), name="ffn_bwd")
    small['ffn_conv_w'] = jnp.transpose(dtaps, (1, 0, 2)).reshape(FFN_CONV_WIDTH, -1)
    small['ffn_conv_b'] = jnp.transpose(dbias, (1, 0, 2)).reshape(1, -1)
    dhn = mm_nt(dup0, W['w_up'], lead="planes", out_dtype=BF16, name="d_hn")
    g['w_up'] = mm_tn(sv['hn'], dup0, lead="planes", slots=W['w_up'].shape[0], out_dtype=BF16, name="g_w_up")
    dh1, dh1b, small['ffn_norm_g'] = rms_bwd(sv['h1'], row('ffn_norm_g'), dhn, dh, name="ffn_norm_bwd")
    dmerged = mm_nt(dh1b, W['w_o'], out_dtype=BF16, after=scatter(SCATTER_GROUPS[0], g), name="d_merged")
    g['w_o'] = mm_tn(sv['merged'], dh1b, slots=1, out_dtype=BF16, name="g_w_o")
    dproj, dy, small['gate_b'] = merge_bwd(dmerged, sv['proj'], row('gate_b'), sv['ys'], gcol=dims['gcol'],
                                           name="merge_bwd")
    ds = mm_nt(dy, W['w_conv_out'], lead=0, out_dtype=BF16, name="d_conv_out")
    g['w_conv_out'] = mm_tn(sv['s'], dy, lead=0, slots=W['w_conv_out'].shape[0], out_dtype=BF16,
                            name="g_w_conv_out")
    dc, small['conv_ln_g'], small['conv_ln_b'] = conv_bwd_ln(ds, sv['c'], row('conv_ln_g'), row('conv_ln_b'),
                                                             name="conv_ln_bwd")
    dproj, small['conv_w'], small['conv_b'] = conv_bwd_taps(dc, sv['proj'], p['conv_w_full'][l], dproj,
                                                            name="conv_taps_bwd")
    datt = mm_nt(dy, W['w_attn_out'], lead=1, out_dtype=BF16, name="d_attn_out")
    g['w_attn_out'] = mm_tn(sv['att'], dy, lead=1, slots=W['w_attn_out'].shape[0], out_dtype=BF16,
                            name="g_w_attn_out")
    dproj, dkw, dvw, dbias = attn_bwd(sv['proj'], sv['bias'], datt, dproj, qcol=dims['qcol'], dattn=dims['DA'],
                                      name="chunk_attn_bwd")
    dproj = window_sum(dkw, dproj, col=dims['qcol'] + dims['DA'], name="dk_windows")
    dproj = window_sum(dvw, dproj, col=dims['qcol'] + 2 * dims['DA'], name="dv_windows")
    small['rel_bias'] = attn_bias_bwd(dbias, name="attn_bias_bwd")[:, :2 * MAX_REL + 1]
    dmo = mm_nt(dy, W['w_mem_out'], lead=2, out_dtype=BF16, name="d_mem_out")
    g['w_mem_out'] = mm_tn(sv['mo'], dy, lead=2, slots=W['w_mem_out'].shape[0], out_dtype=BF16,
                           name="g_w_mem_out")
    dproj, dkv = memattn_bwd(sv['proj'], sv['kv'], dmo, dproj, qcol=dims['mcol'], name="mem_attn_bwd")
    dkvb = dkv.astype(BF16)
    g['w_mem_kv'] = mm_tn(sv['mn'], dkvb, slots=1, out_dtype=BF16, name="g_w_mem_kv")
    dmn = mm_nt(dkvb, W['w_mem_kv'], out_dtype=BF16, name="d_mem_norm")
    _, _, small['mem_norm_g'] = rms_bwd(mem2, row('mem_norm_g'), dmn, jnp.zeros(mem2.shape, F32),
                                        name="mem_norm_bwd")
    dxn = mm_nt(dproj, W['w_in'], out_dtype=BF16, after=scatter(SCATTER_GROUPS[1], g), name="d_xn")
    g['w_in'] = mm_tn(sv['xn'], dproj, slots=N_DEV, out_dtype=BF16, name="g_w_in")
    dh0, dh0b, small['mix_norm_g'] = rms_bwd(sv['h'], row('mix_norm_g'), dxn, dh1, name="mix_norm_bwd")
    return dh0, dh0b, g['w_in'], small


def kernel(x, mem, mix_norm_g, mem_norm_g, w_in, gate_b, conv_w, conv_b, conv_ln_g, conv_ln_b, w_conv_out, rel_bias, w_attn_out, w_mem_kv, w_mem_out, w_o, ffn_norm_g, w_up, ffn_conv_w, ffn_conv_b, w_down, final_norm_g, loss_target, m_mix_norm_g, m_mem_norm_g, m_w_in, m_gate_b, m_conv_w, m_conv_b, m_conv_ln_g, m_conv_ln_b, m_w_conv_out, m_rel_bias, m_w_attn_out, m_w_mem_kv, m_w_mem_out, m_w_o, m_ffn_norm_g, m_w_up, m_ffn_conv_w, m_ffn_conv_b, m_w_down, m_final_norm_g, v_mix_norm_g, v_mem_norm_g, v_w_in, v_gate_b, v_conv_w, v_conv_b, v_conv_ln_g, v_conv_ln_b, v_w_conv_out, v_rel_bias, v_w_attn_out, v_w_mem_kv, v_w_mem_out, v_w_o, v_ffn_norm_g, v_w_up, v_ffn_conv_w, v_ffn_conv_b, v_w_down, v_final_norm_g):
    env = locals()
    w = {n: env[n] for n in WEIGHT_NAMES}
    mom = {n: env['m_' + n] for n in WEIGHT_NAMES}
    var = {n: env['v_' + n] for n in WEIGHT_NAMES}
    T, D = x.shape[-2:]
    L = w_in.shape[0]
    DC = conv_b.shape[-1]
    DA = N_DEV * w_attn_out.shape[-1] // 2
    dims = {'DA': DA, 'qcol': 2 * DC, 'mcol': 2 * DC + 3 * DA, 'gcol': 2 * DC + 3 * DA + D // 2}
    x2, mem2, target = x.reshape(T, D), mem.reshape(-1, D), loss_target.reshape(T, D)
    me = 4 * lax.axis_index("x") + 2 * lax.axis_index("y") + lax.axis_index("c")

    p = dict(w)

    def start_gather(l, names, first, tag):
        blocks = [w[n][l].astype(BF16) for n in names]
        first, blocks = lax.optimization_barrier((first, blocks))
        tag = "gather_%d%s" % (l, tag)
        started = push_start(blocks, [_own_slot(b, me) for b in blocks], kind="gather", name=tag + "_start1")
        return {'names': names, 'started': started, 'trip': 1, 'tag': tag}

    taps, ffn_taps = all_gather([conv_w, ffn_conv_w], name="gather_taps")
    p['conv_w_full'] = jnp.moveaxis(taps, 0, 2).reshape(L, conv_w.shape[1], -1)
    p['ffn_conv_w_full'] = jnp.moveaxis(ffn_taps, 0, 2).reshape(L, ffn_conv_w.shape[1], -1)
    p['rel_pad'] = jnp.pad(rel_bias, ((0, 0), (0, 0), (0, REL_PAD - rel_bias.shape[-1])))
    pending, first = [], taps
    for names, tag in zip(GATHER_GROUPS, "abc"):
        pending.append(start_gather(0, names, first, tag))
        first = pending[-1]['started'][4]
    p['gathered'] = [_LayerWeights(pending)] + [None] * (L - 1)

    h = x2
    saved = []
    for l in range(L):
        def prefetch(first, l=l):
            if l + 1 == L:
                return None
            nxt = start_gather(l + 1, [n for n, _ in BIG], first, "")
            p['gathered'][l + 1] = _LayerWeights([nxt])
            return nxt['started'][4]

        h, sv = _forward_layer(h, mem2, p, l, dims, prefetch)
        saved.append(sv)
    loss_part, dh, dhb, d_final = loss_head(h, final_norm_g.reshape(1, D), target, name="loss_head")
    loss = lax.psum(loss_part[0, 0], ("x", "y", "c"))

    small, landed, inflight = [None] * L, [{} for _ in range(L)], []

    def scatter_start(l, names, g):
        tag = "abc"[SCATTER_GROUPS.index(names)]
        grads = [_as_shards(g[n], BY_ROWS[n]) for n in names]
        zones = [_own_slot(lax.dynamic_index_in_dim(a, me, 0, keepdims=False), me) for a in grads]
        started = push_start(grads, zones, kind="scatter", name="scatter_start_%d%s" % (l, tag))
        inflight.append((l, names, started, "scatter_wait_%d%s" % (l, tag)))
        return started[4]

    def scatter_finish(after, groups):
        for item in [i for i in inflight if (i[0], i[1]) in groups]:
            l, names, started, wait_name = item
            landed[l].update(zip(names, push_wait(*started[:4], after, kind="scatter", name=wait_name)))
            inflight.remove(item)

    token = None
    for l in reversed(range(L)):
        dh, dhb, g_w_in, small[l] = _backward_layer(dh, dhb, mem2, saved[l], p, l, dims,
                                                    functools.partial(scatter_start, l), after=token)
        scatter_finish(dh, [(l + 1, names) for names in SCATTER_GROUPS])
        if l > 0:
            token = scatter_start(l, SCATTER_GROUPS[2], {'w_in': g_w_in})

    big_names = [n for n, _ in BIG]
    small_names = [n for n in WEIGHT_NAMES if n not in big_names and n != 'final_norm_g']
    stacked = [jnp.stack([small[l][n] for l in range(L)]) for n in small_names] + [d_final]
    got = dict(zip(small_names + ['final_norm_g'], all_gather(stacked, name="gather_small_grads")))
    g_w_in, _ = lax.optimization_barrier((g_w_in, got['final_norm_g']))
    token = scatter_start(0, SCATTER_GROUPS[2], {'w_in': g_w_in})
    scatter_finish(token, [(0, names) for names in SCATTER_GROUPS[:2]])
    for n in ('conv_w', 'ffn_conv_w'):
        cs = w[n].shape[-1]
        got[n] = lax.dynamic_slice_in_dim(got[n], me * cs, cs, axis=3)
    contribs = {n: [got[n].reshape(N_DEV, -1, w[n].shape[-1])] for n in small_names + ['final_norm_g']}

    outs = {}
    for n in [n for n in WEIGHT_NAMES if n != 'w_in'] + ['w_in']:
        if n == 'w_in':
            done = lax.optimization_barrier(tuple(o[0] for o in outs.values()))
            scatter_finish(done[0], [(0, SCATTER_GROUPS[2])])
        if n in big_names:
            contribs[n] = [landed[l][n] for l in range(L)]
        shp = (len(contribs[n]),) + contribs[n][0].shape[1:]
        res = adamw(contribs[n], w[n].reshape(shp), mom[n].reshape(shp), var[n].reshape(shp), name="adamw_" + n)
        outs[n] = [r.reshape(w[n].shape) for r in res]
    return (loss, dh.reshape(x.shape),
            *[outs[n][0] for n in WEIGHT_NAMES], *[outs[n][1] for n in WEIGHT_NAMES],
            *[outs[n][2] for n in WEIGHT_NAMES], *[outs[n][3] for n in WEIGHT_NAMES])
```

```python
import functools

import numpy as np
import jax
import jax.numpy as jnp
from jax import lax
from jax.experimental import pallas as pl
from jax.experimental.pallas import tpu as pltpu

F32 = jnp.float32
BF16 = jnp.bfloat16

CHUNK = 64
LEFT_CHUNKS = 8
MAX_REL = 256
N_MEM_HEADS = 4
ATTN_HEAD_DIM = 64
CONV_WIDTH = 31
FFN_CONV_WIDTH = 3
EPS = 1e-6
NEG_INF = -1e30
ADAM_LR = 0.001
ADAM_B1 = 0.9
ADAM_B2 = 0.999
ADAM_EPS = 1e-08
ADAM_WD = 0.01
ADAM_STEP = 10
N_DEV = 8

VMEM_LIMIT_BYTES = 56 * 1024 * 1024
LANES = 128
SUBLANES = 8


def _cparams(sem=None):
    return pltpu.CompilerParams(dimension_semantics=sem, vmem_limit_bytes=VMEM_LIMIT_BYTES)


def _tile(n, want):
    if n <= want:
        return n
    t = (want // LANES) * LANES
    while t >= LANES:
        if n % t == 0:
            return t if 4 * t >= want or n > 2 * want else n
        t -= LANES
    return n


MM_TM, MM_TN, MM_TK = 1024, 1536, 2048


def _matmul(name, a, b, *, dims, grid, nk, a_spec, b_spec, out_spec, out_shape, acc_shape, residual=None,
            after=None, slots_per_step=1):
    n_in = 2 + (residual is not None) + (after is not None)

    def body(*refs):
        a_ref, b_ref = refs[:2]
        o_ref = refs[n_in]

        def finish(r):
            if residual is not None:
                r = r + refs[2][...]
            o_ref[...] = r.astype(o_ref.dtype)

        if slots_per_step == 1:
            part = lax.dot_general(a_ref[...], b_ref[...], (dims, ((), ())), preferred_element_type=F32)
        else:
            tk = b_ref.shape[-1]
            part = functools.reduce(jnp.add, [
                lax.dot_general(a_ref[:, pl.ds(g * tk, tk)], b_ref[g], (dims, ((), ())), preferred_element_type=F32)
                for g in range(slots_per_step)])
        if nk == 1:
            finish(part)
            return
        acc = refs[-1]
        k = pl.program_id(2)

        @pl.when(k == 0)
        def _():
            acc[...] = part

        @pl.when((k > 0) & (k < nk - 1))
        def _():
            acc[...] += part

        @pl.when(k == nk - 1)
        def _():
            finish(acc[...] + part)

    in_specs, args = [a_spec, b_spec], [a, b]
    if residual is not None:
        in_specs.append(out_spec)
        args.append(residual)
    if after is not None:
        in_specs.append(pl.BlockSpec(memory_space=pl.ANY))
        args.append(after)
    return pl.pallas_call(
        body, name=name, grid=grid, in_specs=in_specs, out_specs=out_spec, out_shape=out_shape,
        scratch_shapes=[pltpu.VMEM(acc_shape, F32)] if nk > 1 else [],
        compiler_params=_cparams(("parallel", "parallel", "arbitrary")),
    )(*args)


def _lead_spec(arr, lead, block, index):
    if lead is None:
        assert arr.ndim == 2
        return pl.BlockSpec(block, index)
    assert arr.ndim == 3
    if lead == "planes":
        per_plane = arr.shape[2] // block[1]
        assert arr.shape[2] % block[1] == 0

        def planes_index(i, j, k):
            r, c = index(i, j, k)
            return c // per_plane, r, c % per_plane

        return pl.BlockSpec((None, *block), planes_index)
    return pl.BlockSpec((None, *block), lambda i, j, k: (lead, *index(i, j, k)))


def _matrix_shape(arr, lead):
    return (arr.shape[1], arr.shape[0] * arr.shape[2]) if lead == "planes" else arr.shape[-2:]


def mm_nn(a, w, *, out_dtype, residual=None, after=None, tm=MM_TM, tn=MM_TN, tk=MM_TK, name):
    M, K = a.shape
    S, K2, Ns = w.shape
    assert K == K2
    tm, tn, tk = _tile(M, tm), _tile(Ns, tn), _tile(K, tk)
    nb = Ns // tn
    return _matmul(
        name, a, w, dims=((1,), (0,)), grid=(M // tm, S * nb, K // tk), nk=K // tk,
        a_spec=pl.BlockSpec((tm, tk), lambda i, j, k: (i, k)),
        b_spec=pl.BlockSpec((None, tk, tn), lambda i, j, k: (j // nb, k, j % nb)),
        out_spec=pl.BlockSpec((tm, tn), lambda i, j, k: (i, j)),
        out_shape=jax.ShapeDtypeStruct((M, S * Ns), out_dtype), acc_shape=(tm, tn), residual=residual,
        after=after)


def mm_nt(a, w, *, out_dtype, lead=None, after=None, tm=MM_TM, tn=MM_TN, tk=MM_TK, name):
    M, N = _matrix_shape(a, lead)
    S, K, Ns = w.shape
    assert N == S * Ns
    tm, tn, tk = _tile(M, tm), _tile(K, tn), _tile(Ns, tk)
    nb = Ns // tk
    if nb == 1 and S % 2 == 0 and S > 2:
        return _matmul(
            name, a, w, dims=((1,), (1,)), grid=(M // tm, K // tn, S // 2), nk=S // 2,
            a_spec=_lead_spec(a, lead, (tm, 2 * tk), lambda i, j, k: (i, k)),
            b_spec=pl.BlockSpec((2, tn, tk), lambda i, j, k: (k, j, 0)),
            out_spec=pl.BlockSpec((tm, tn), lambda i, j, k: (i, j)),
            out_shape=jax.ShapeDtypeStruct((M, K), out_dtype), acc_shape=(tm, tn), after=after, slots_per_step=2)
    return _matmul(
        name, a, w, dims=((1,), (1,)), grid=(M // tm, K // tn, S * nb), nk=S * nb,
        a_spec=_lead_spec(a, lead, (tm, tk), lambda i, j, k: (i, k)),
        b_spec=pl.BlockSpec((None, tn, tk), lambda i, j, k: (k // nb, j, k % nb)),
        out_spec=pl.BlockSpec((tm, tn), lambda i, j, k: (i, j)),
        out_shape=jax.ShapeDtypeStruct((M, K), out_dtype), acc_shape=(tm, tn), after=after)


def mm_tn(a, b, *, slots, out_dtype, lead=None, tm=MM_TM, tn=MM_TN, tk=MM_TK, name):
    T, K = a.shape
    T2, N = _matrix_shape(b, lead)
    assert T == T2 and N % slots == 0
    Ns = N // slots
    tm, tn, tk = _tile(K, tm), _tile(Ns, tn), _tile(T, tk)
    nb = Ns // tn
    return _matmul(
        name, a, b, dims=((0,), (0,)), grid=(K // tm, slots * nb, T // tk), nk=T // tk,
        a_spec=pl.BlockSpec((tk, tm), lambda i, j, k: (k, i)),
        b_spec=_lead_spec(b, lead, (tk, tn), lambda i, j, k: (k, j)),
        out_spec=pl.BlockSpec((None, tm, tn), lambda i, j, k: (j // nb, i, j % nb)),
        out_shape=jax.ShapeDtypeStruct((slots, K, Ns), out_dtype), acc_shape=(tm, tn))


def rms_fwd(h, g, *, tr=256, name):
    T, D = h.shape
    tr = min(tr, T)

    def body(h_ref, g_ref, o_ref):
        x = h_ref[...]
        r = lax.rsqrt(jnp.mean(x * x, axis=-1, keepdims=True) + EPS)
        o_ref[...] = (x * r * g_ref[...]).astype(o_ref.dtype)

    return pl.pallas_call(
        body, name=name, grid=(T // tr,),
        in_specs=[pl.BlockSpec((tr, D), lambda i: (i, 0)), pl.BlockSpec((1, D), lambda i: (0, 0))],
        out_specs=pl.BlockSpec((tr, D), lambda i: (i, 0)),
        out_shape=jax.ShapeDtypeStruct((T, D), BF16),
        compiler_params=_cparams(("parallel",)),
    )(h, g)


def rms_bwd(h, g, dxn, dres, *, tr=256, name):
    T, D = h.shape
    tr = min(tr, T)

    def body(h_ref, g_ref, d_ref, r_ref, dh_ref, dhb_ref, dg_ref):
        i = pl.program_id(0)
        x = h_ref[...]
        r = lax.rsqrt(jnp.mean(x * x, axis=-1, keepdims=True) + EPS)
        xh = x * r
        d = d_ref[...].astype(F32)

        @pl.when(i == 0)
        def _():
            dg_ref[...] = jnp.zeros_like(dg_ref)

        dg_ref[...] += jnp.sum(d * xh, axis=0, keepdims=True)
        dxh = d * g_ref[...]
        dh = r * (dxh - xh * jnp.mean(dxh * xh, axis=-1, keepdims=True)) + r_ref[...]
        dh_ref[...] = dh
        dhb_ref[...] = dh.astype(BF16)

    row = pl.BlockSpec((tr, D), lambda i: (i, 0))
    vec = pl.BlockSpec((1, D), lambda i: (0, 0))
    return pl.pallas_call(
        body, name=name, grid=(T // tr,),
        in_specs=[row, vec, row, row],
        out_specs=[row, row, vec],
        out_shape=[jax.ShapeDtypeStruct((T, D), F32), jax.ShapeDtypeStruct((T, D), BF16),
                   jax.ShapeDtypeStruct((1, D), F32)],
        compiler_params=_cparams(("arbitrary",)),
    )(h, g, dxn, dres)


def loss_head(h, g, target, *, tr=256, name):
    T, D = h.shape
    tr = min(tr, T)

    def body(h_ref, g_ref, t_ref, loss_ref, dh_ref, dhb_ref, dg_ref):
        i = pl.program_id(0)
        x = h_ref[...]
        r = lax.rsqrt(jnp.mean(x * x, axis=-1, keepdims=True) + EPS)
        xh = x * r
        gg = g_ref[...]
        err = xh * gg - t_ref[...]

        @pl.when(i == 0)
        def _():
            dg_ref[...] = jnp.zeros_like(dg_ref)
            loss_ref[...] = jnp.zeros_like(loss_ref)

        loss_ref[...] += 0.5 * jnp.sum(jnp.mean(err * err, axis=-1, keepdims=True))
        dy = err * (1.0 / D)
        dg_ref[...] += jnp.sum(dy * xh, axis=0, keepdims=True)
        dxh = dy * gg
        dh = r * (dxh - xh * jnp.mean(dxh * xh, axis=-1, keepdims=True))
        dh_ref[...] = dh
        dhb_ref[...] = dh.astype(BF16)

    row = pl.BlockSpec((tr, D), lambda i: (i, 0))
    vec = pl.BlockSpec((1, D), lambda i: (0, 0))
    return pl.pallas_call(
        body, name=name, grid=(T // tr,),
        in_specs=[row, vec, row],
        out_specs=[pl.BlockSpec((8, LANES), lambda i: (0, 0)), row, row, vec],
        out_shape=[jax.ShapeDtypeStruct((8, LANES), F32), jax.ShapeDtypeStruct((T, D), F32),
                   jax.ShapeDtypeStruct((T, D), BF16), jax.ShapeDtypeStruct((1, D), F32)],
        compiler_params=_cparams(("arbitrary",)),
    )(h, g, target)


CONV_HALO = 32


def _glu_ext(prev_ref, main_ref, hs_ref, i, dc, tt):
    up = prev_ref[...].astype(F32)
    hp = up[:, :dc] * jax.nn.sigmoid(up[:, dc:])
    hs_ref[pl.ds(0, CONV_HALO), :] = jnp.where(i > 0, hp, 0.0)
    um = main_ref[...].astype(F32)
    sig = jax.nn.sigmoid(um[:, dc:])
    hs_ref[pl.ds(CONV_HALO, tt), :] = um[:, :dc] * sig


CONV_ROWS = 32


def _phase_copies(src_ref, ph_ref, rows):
    for b in range(1, SUBLANES):
        ph_ref[b - 1, pl.ds(0, rows), :] = src_ref[pl.ds(b, rows), :]


def _tap_rows(src_ref, ph_ref, offset, r0, n):
    a, b = divmod(offset, SUBLANES)
    ref = src_ref if b == 0 else ph_ref.at[b - 1]
    return ref[pl.ds(r0 + a * SUBLANES, n), :]


def _fold_rows(x):
    out = x[0:SUBLANES]
    for r in range(SUBLANES, x.shape[0], SUBLANES):
        out = out + x[r:r + SUBLANES]
    return out


def conv_fwd(proj, conv_w, conv_b, ln_g, ln_b, *, tt=256, name):
    T = proj.shape[0]
    W, DC = conv_w.shape
    tt = min(tt, T)
    hb = tt // CONV_HALO
    ph_rows = tt + CONV_HALO - SUBLANES

    def body(prev_ref, main_ref, w_ref, b_ref, g_ref, bb_ref, s_ref, c_ref, hs_ref, ph_ref):
        i = pl.program_id(0)
        _glu_ext(prev_ref, main_ref, hs_ref, i, DC, tt)
        _phase_copies(hs_ref, ph_ref, ph_rows)

        def chunk(cc, carry):
            r0 = pl.multiple_of(cc * CONV_ROWS, CONV_ROWS)
            c = jnp.zeros((CONV_ROWS, DC), F32) + b_ref[...]
            for k in range(W):
                c = c + w_ref[pl.ds(k, 1), :] * _tap_rows(hs_ref, ph_ref, CONV_HALO - (W - 1) + k, r0, CONV_ROWS)
            rows = pl.ds(r0, CONV_ROWS)
            c_ref[rows, :] = c
            mu = jnp.mean(c, axis=-1, keepdims=True)
            xc = c - mu
            var = jnp.mean(xc * xc, axis=-1, keepdims=True)
            y = xc * lax.rsqrt(var + EPS) * g_ref[...] + bb_ref[...]
            s_ref[rows, :] = (y * jax.nn.sigmoid(y)).astype(s_ref.dtype)
            return carry

        lax.fori_loop(0, tt // CONV_ROWS, chunk, 0)

    vec = pl.BlockSpec((1, DC), lambda i: (0, 0))
    return pl.pallas_call(
        body, name=name, grid=(T // tt,),
        in_specs=[pl.BlockSpec((CONV_HALO, 2 * DC), lambda i: (jnp.maximum(i * hb - 1, 0), 0)),
                  pl.BlockSpec((tt, 2 * DC), lambda i: (i, 0)),
                  pl.BlockSpec((W, DC), lambda i: (0, 0)), vec, vec, vec],
        out_specs=[pl.BlockSpec((tt, DC), lambda i: (i, 0)), pl.BlockSpec((tt, DC), lambda i: (i, 0))],
        out_shape=[jax.ShapeDtypeStruct((T, DC), BF16), jax.ShapeDtypeStruct((T, DC), F32)],
        scratch_shapes=[pltpu.VMEM((tt + CONV_HALO, DC), F32), pltpu.VMEM((SUBLANES - 1, ph_rows, DC), F32)],
        compiler_params=_cparams(("parallel",)),
    )(proj, proj, conv_w, conv_b, ln_g, ln_b)


def conv_bwd_ln(ds, c, ln_g, ln_b, *, tt=256, name):
    T, DC = c.shape
    tt = min(tt, T)

    def body(ds_ref, c_ref, g_ref, bb_ref, dc_ref, dg_ref, db_ref):
        i = pl.program_id(0)
        c = c_ref[...]
        mu = jnp.mean(c, axis=-1, keepdims=True)
        xc = c - mu
        var = jnp.mean(xc * xc, axis=-1, keepdims=True)
        rstd = lax.rsqrt(var + EPS)
        xh = xc * rstd
        y = xh * g_ref[...] + bb_ref[...]
        sg = jax.nn.sigmoid(y)
        dy = ds_ref[...].astype(F32) * (sg * (1.0 + y * (1.0 - sg)))

        @pl.when(i == 0)
        def _():
            dg_ref[...] = jnp.zeros_like(dg_ref)
            db_ref[...] = jnp.zeros_like(db_ref)

        db_ref[...] += jnp.sum(dy, axis=0, keepdims=True)
        dg_ref[...] += jnp.sum(dy * xh, axis=0, keepdims=True)
        dxh = dy * g_ref[...]
        dc_ref[...] = rstd * (dxh - jnp.mean(dxh, axis=-1, keepdims=True)
                              - xh * jnp.mean(dxh * xh, axis=-1, keepdims=True))

    row = pl.BlockSpec((tt, DC), lambda i: (i, 0))
    vec = pl.BlockSpec((1, DC), lambda i: (0, 0))
    return pl.pallas_call(
        body, name=name, grid=(T // tt,),
        in_specs=[row, row, vec, vec], out_specs=[row, vec, vec],
        out_shape=[jax.ShapeDtypeStruct((T, DC), F32), jax.ShapeDtypeStruct((1, DC), F32),
                   jax.ShapeDtypeStruct((1, DC), F32)],
        compiler_params=_cparams(("arbitrary",)),
    )(ds, c, ln_g, ln_b)


def conv_bwd_taps(dc, proj, conv_w, dproj, *, tt=256, name):
    T, DC = dc.shape
    W = conv_w.shape[0]
    tt = min(tt, T)
    hb = tt // CONV_HALO
    n_t = T // tt
    last_halo = T // CONV_HALO - 1
    ph_rows = tt + CONV_HALO - SUBLANES

    def body(dc_ref, dcn_ref, prev_ref, main_ref, w_ref, _, dp_ref, dw_ref, db_ref, hs_ref, ds_ref, hph_ref,
             dph_ref, acc_ref):
        i = pl.program_id(0)
        _glu_ext(prev_ref, main_ref, hs_ref, i, DC, tt)
        ds_ref[pl.ds(0, tt), :] = dc_ref[...]
        ds_ref[pl.ds(tt, CONV_HALO), :] = jnp.where(i < n_t - 1, dcn_ref[...], 0.0)
        _phase_copies(hs_ref, hph_ref, ph_rows)
        _phase_copies(ds_ref, dph_ref, ph_rows)
        acc_ref[...] = jnp.zeros_like(acc_ref)

        def chunk(cc, carry):
            r0 = pl.multiple_of(cc * CONV_ROWS, CONV_ROWS)
            rows = pl.ds(r0, CONV_ROWS)
            d = ds_ref[rows, :]
            dh = jnp.zeros((CONV_ROWS, DC), F32)
            for k in range(W):
                dh = dh + w_ref[pl.ds(k, 1), :] * _tap_rows(ds_ref, dph_ref, W - 1 - k, r0, CONV_ROWS)
                acc_ref[k] += _fold_rows(d * _tap_rows(hs_ref, hph_ref, CONV_HALO - (W - 1) + k, r0, CONV_ROWS))
            acc_ref[W] += _fold_rows(d)
            um = main_ref[rows, :].astype(F32)
            a, sig = um[:, :DC], jax.nn.sigmoid(um[:, DC:])
            dp_ref[rows, :] = jnp.concatenate([dh * sig, dh * a * sig * (1.0 - sig)], axis=1).astype(dp_ref.dtype)
            return carry

        lax.fori_loop(0, tt // CONV_ROWS, chunk, 0)

        @pl.when(i == 0)
        def _():
            dw_ref[...] = jnp.zeros_like(dw_ref)
            db_ref[...] = jnp.zeros_like(db_ref)

        for k in range(W):
            dw_ref[pl.ds(k, 1), :] += jnp.sum(acc_ref[k], axis=0, keepdims=True)
        db_ref[...] += jnp.sum(acc_ref[W], axis=0, keepdims=True)

    return pl.pallas_call(
        body, name=name, grid=(n_t,),
        in_specs=[pl.BlockSpec((tt, DC), lambda i: (i, 0)),
                  pl.BlockSpec((CONV_HALO, DC), lambda i: (jnp.minimum((i + 1) * hb, last_halo), 0)),
                  pl.BlockSpec((CONV_HALO, 2 * DC), lambda i: (jnp.maximum(i * hb - 1, 0), 0)),
                  pl.BlockSpec((tt, 2 * DC), lambda i: (i, 0)),
                  pl.BlockSpec((W, DC), lambda i: (0, 0)),
                  pl.BlockSpec(memory_space=pl.ANY)],
        out_specs=[pl.BlockSpec((tt, 2 * DC), lambda i: (i, 0)),
                   pl.BlockSpec((W, DC), lambda i: (0, 0)), pl.BlockSpec((1, DC), lambda i: (0, 0))],
        out_shape=[jax.ShapeDtypeStruct(dproj.shape, dproj.dtype), jax.ShapeDtypeStruct((W, DC), F32),
                   jax.ShapeDtypeStruct((1, DC), F32)],
        scratch_shapes=[pltpu.VMEM((tt + CONV_HALO, DC), F32), pltpu.VMEM((tt + CONV_HALO, DC), F32),
                        pltpu.VMEM((SUBLANES - 1, ph_rows, DC), F32), pltpu.VMEM((SUBLANES - 1, ph_rows, DC), F32),
                        pltpu.VMEM((W + 1, SUBLANES, DC), F32)],
        input_output_aliases={5: 0},
        compiler_params=_cparams(("arbitrary",)),
    )(dc, dc, proj, proj, conv_w, dproj)


ATT_TQ = 256
ATT_NB = 1 + (LEFT_CHUNKS * CHUNK) // ATT_TQ
ATT_WIN = ATT_NB * ATT_TQ
ATT_PERIOD = 1024
REL_PAD = 640


def _rel_onehot():
    m = lax.broadcasted_iota(jnp.int32, (REL_PAD, ATT_PERIOD), 1)
    r = lax.broadcasted_iota(jnp.int32, (REL_PAD, ATT_PERIOD), 0)
    qk = jnp.where(m < ATT_PERIOD - ATT_TQ, -m, ATT_PERIOD - m)
    idx = jnp.clip(LEFT_CHUNKS * CHUNK + qk, -MAX_REL, MAX_REL) + MAX_REL
    return (idx == r).astype(F32)


def _band_valid():
    q = lax.broadcasted_iota(jnp.int32, (ATT_TQ, ATT_WIN), 0)
    k = lax.broadcasted_iota(jnp.int32, (ATT_TQ, ATT_WIN), 1)
    j = k // CHUNK - q // CHUNK
    return (j >= 0) & (j <= LEFT_CHUNKS)


def attn_bias(rel_pad, *, name):
    H = rel_pad.shape[0]

    def body(rb_ref, o_ref):
        g = jnp.dot(rb_ref[...], _rel_onehot(), preferred_element_type=F32, precision=lax.Precision.HIGHEST)
        valid = _band_valid()
        for h in range(H):
            row = jnp.broadcast_to(g[h:h + 1, :], (ATT_TQ, ATT_PERIOD))
            t = pltpu.roll(row, 0, 1, stride=1, stride_axis=0)
            o_ref[h] = jnp.where(valid, t[:, :ATT_WIN], NEG_INF)

    return pl.pallas_call(
        body, name=name,
        in_specs=[pl.BlockSpec(memory_space=pltpu.VMEM)], out_specs=pl.BlockSpec(memory_space=pltpu.VMEM),
        out_shape=jax.ShapeDtypeStruct((H, ATT_TQ, ATT_WIN), F32),
        compiler_params=_cparams(),
    )(rel_pad)


def attn_bias_bwd(dbias, *, name):
    H = dbias.shape[0]
    band = (LEFT_CHUNKS + 1) * CHUNK

    def body(d_ref, o_ref, acc_ref):
        r = lax.broadcasted_iota(jnp.int32, (CHUNK, CHUNK), 0)
        c = lax.broadcasted_iota(jnp.int32, (CHUNK, CHUNK), 1)
        rev = (r + c == CHUNK - 1).astype(F32)
        pad = jnp.zeros((CHUNK, ATT_PERIOD - ATT_WIN), F32)
        for h in range(H):
            tab = d_ref[h, pl.ds(0, CHUNK), :]
            for cq in range(1, ATT_TQ // CHUNK):
                tab = tab + pltpu.roll(d_ref[h, pl.ds(cq * CHUNK, CHUNK), :], ATT_WIN - cq * CHUNK, 1)
            t = jnp.dot(rev, tab, preferred_element_type=F32, precision=lax.Precision.HIGHEST)
            u = pltpu.roll(jnp.concatenate([t, pad], axis=1), 0, 1, stride=1, stride_axis=0)
            acc_ref[pl.ds(h, 1), :] = jnp.sum(u, axis=0, keepdims=True)
        j = lax.broadcasted_iota(jnp.int32, (REL_PAD, ATT_PERIOD), 1)
        rr = lax.broadcasted_iota(jnp.int32, (REL_PAD, ATT_PERIOD), 0)
        idx = jnp.clip(LEFT_CHUNKS * CHUNK + CHUNK - 1 - j, -MAX_REL, MAX_REL) + MAX_REL
        onehot = ((idx == rr) & (j < band + CHUNK - 1)).astype(F32)
        o_ref[...] = lax.dot_general(acc_ref[...], onehot, (((1,), (1,)), ((), ())),
                                     preferred_element_type=F32, precision=lax.Precision.HIGHEST)

    return pl.pallas_call(
        body, name=name,
        in_specs=[pl.BlockSpec(memory_space=pltpu.VMEM)], out_specs=pl.BlockSpec(memory_space=pltpu.VMEM),
        out_shape=jax.ShapeDtypeStruct((H, REL_PAD), F32),
        scratch_shapes=[pltpu.VMEM((H, ATT_PERIOD), F32)],
        compiler_params=_cparams(),
    )(dbias)


def _attn_specs(T, qcol, dattn):
    dn = dattn // LANES

    def kv(col0, back):
        return pl.BlockSpec((ATT_TQ, LANES), lambda hp, i: (jnp.maximum(i - back, 0), col0 // LANES + hp))

    q = pl.BlockSpec((ATT_TQ, LANES), lambda hp, i: (i, qcol // LANES + hp))
    ks = [kv(qcol + dattn, ATT_NB - 1 - b) for b in range(ATT_NB)]
    vs = [kv(qcol + 2 * dattn, ATT_NB - 1 - b) for b in range(ATT_NB)]
    return q, ks, vs


def _attn_scores(q, kw, bias, i):
    s = lax.dot_general(q, kw, (((1,), (1,)), ((), ())), preferred_element_type=F32)
    s = s * (ATTN_HEAD_DIM ** -0.5) + bias
    col = lax.broadcasted_iota(jnp.int32, s.shape, 1)
    s = jnp.where(col // ATT_TQ + i >= ATT_NB - 1, s, NEG_INF)
    s = s - jnp.max(s, axis=-1, keepdims=True)
    p = jnp.exp(s)
    return p * (1.0 / jnp.sum(p, axis=-1, keepdims=True))


def attn_fwd(proj, bias, *, qcol, dattn, name):
    T = proj.shape[0]
    HP = dattn // LANES
    q_spec, k_specs, v_specs = _attn_specs(T, qcol, dattn)

    def body(*refs):
        q_ref = refs[0]
        k_refs = refs[1:1 + ATT_NB]
        v_refs = refs[1 + ATT_NB:1 + 2 * ATT_NB]
        b_ref, o_ref = refs[1 + 2 * ATT_NB:]
        i = pl.program_id(1)
        lane = lax.broadcasted_iota(jnp.int32, (ATT_TQ, LANES), 1)
        kw = jnp.concatenate([r[...] for r in k_refs], axis=0)
        vw = jnp.concatenate([r[...] for r in v_refs], axis=0)
        q = q_ref[...]
        out = jnp.zeros((ATT_TQ, LANES), F32)
        for hh in range(2):
            mine = (lane // ATTN_HEAD_DIM) == hh
            p = _attn_scores(jnp.where(mine, q, jnp.zeros_like(q)), kw, b_ref[hh], i)
            o = jnp.dot(p.astype(BF16), vw, preferred_element_type=F32)
            out = jnp.where(mine, o, out)
        o_ref[...] = out.astype(o_ref.dtype)

    return pl.pallas_call(
        body, name=name, grid=(HP, T // ATT_TQ),
        in_specs=[q_spec, *k_specs, *v_specs,
                  pl.BlockSpec((2, ATT_TQ, ATT_WIN), lambda hp, i: (hp, 0, 0))],
        out_specs=pl.BlockSpec((ATT_TQ, LANES), lambda hp, i: (i, hp)),
        out_shape=jax.ShapeDtypeStruct((T, dattn), BF16),
        compiler_params=_cparams(("parallel", "parallel")),
    )(*([proj] * (1 + 2 * ATT_NB)), bias)


def attn_bwd(proj, bias, dout, dproj, *, qcol, dattn, name):
    T = proj.shape[0]
    HP = dattn // LANES
    nq = T // ATT_TQ
    q_spec, k_specs, v_specs = _attn_specs(T, qcol, dattn)

    def body(*refs):
        q_ref = refs[0]
        k_refs = refs[1:1 + ATT_NB]
        v_refs = refs[1 + ATT_NB:1 + 2 * ATT_NB]
        b_ref, do_ref, _, dq_ref, dk_ref, dv_ref, db_ref = refs[1 + 2 * ATT_NB:]
        i = pl.program_id(1)
        lane = lax.broadcasted_iota(jnp.int32, (ATT_TQ, LANES), 1)
        kw = jnp.concatenate([r[...] for r in k_refs], axis=0)
        vw = jnp.concatenate([r[...] for r in v_refs], axis=0)
        q = q_ref[...]
        do = do_ref[...]
        dq = jnp.zeros((ATT_TQ, LANES), F32)
        dk = jnp.zeros((ATT_WIN, LANES), F32)
        dv = jnp.zeros((ATT_WIN, LANES), F32)

        @pl.when(i == 0)
        def _():
            db_ref[...] = jnp.zeros_like(db_ref)

        for hh in range(2):
            mine = (lane // ATTN_HEAD_DIM) == hh
            qh = jnp.where(mine, q, jnp.zeros_like(q))
            doh = jnp.where(mine, do, jnp.zeros_like(do))
            p = _attn_scores(qh, kw, b_ref[hh], i)
            dp = lax.dot_general(doh, vw, (((1,), (1,)), ((), ())), preferred_element_type=F32)
            dv = dv + lax.dot_general(p.astype(BF16), doh, (((0,), (0,)), ((), ())), preferred_element_type=F32)
            ds = p * (dp - jnp.sum(dp * p, axis=-1, keepdims=True))
            db_ref[hh] += ds
            dsb = (ds * (ATTN_HEAD_DIM ** -0.5)).astype(BF16)
            dq = jnp.where(mine, jnp.dot(dsb, kw, preferred_element_type=F32), dq)
            dk = dk + lax.dot_general(dsb, qh, (((0,), (0,)), ((), ())), preferred_element_type=F32)
        dq_ref[...] = dq.astype(dq_ref.dtype)
        dk_ref[...] = dk
        dv_ref[...] = dv

    win = pl.BlockSpec((None, ATT_WIN, LANES), lambda hp, i: (i, 0, hp))
    return pl.pallas_call(
        body, name=name, grid=(HP, nq),
        in_specs=[q_spec, *k_specs, *v_specs,
                  pl.BlockSpec((2, ATT_TQ, ATT_WIN), lambda hp, i: (hp, 0, 0)),
                  pl.BlockSpec((ATT_TQ, LANES), lambda hp, i: (i, hp)),
                  pl.BlockSpec(memory_space=pl.ANY)],
        out_specs=[pl.BlockSpec((ATT_TQ, LANES), lambda hp, i: (i, qcol // LANES + hp)), win, win,
                   pl.BlockSpec((2, ATT_TQ, ATT_WIN), lambda hp, i: (hp, 0, 0))],
        out_shape=[jax.ShapeDtypeStruct(dproj.shape, dproj.dtype),
                   jax.ShapeDtypeStruct((nq, ATT_WIN, dattn), F32), jax.ShapeDtypeStruct((nq, ATT_WIN, dattn), F32),
                   jax.ShapeDtypeStruct((2 * HP, ATT_TQ, ATT_WIN), F32)],
        input_output_aliases={3 + 2 * ATT_NB: 0},
        compiler_params=_cparams(("arbitrary", "arbitrary")),
    )(*([proj] * (1 + 2 * ATT_NB)), bias, dout, dproj)


def window_sum(win, dproj, *, col, name):
    nq, _, C = win.shape
    assert col % C == 0

    def body(*refs):
        w_refs = refs[:ATT_NB]
        o_ref = refs[ATT_NB + 1]
        j = pl.program_id(0)
        acc = w_refs[0][...]
        for b in range(1, ATT_NB):
            acc = acc + jnp.where(j + b < nq, w_refs[b][...], 0.0)
        o_ref[...] = acc.astype(o_ref.dtype)

    def part(b):
        return pl.BlockSpec((None, ATT_TQ, C), lambda j: (jnp.minimum(j + b, nq - 1), ATT_NB - 1 - b, 0))

    return pl.pallas_call(
        body, name=name, grid=(nq,),
        in_specs=[part(b) for b in range(ATT_NB)] + [pl.BlockSpec(memory_space=pl.ANY)],
        out_specs=pl.BlockSpec((ATT_TQ, C), lambda j: (j, col // C)),
        out_shape=jax.ShapeDtypeStruct(dproj.shape, dproj.dtype),
        input_output_aliases={ATT_NB: 0},
        compiler_params=_cparams(("arbitrary",)),
    )(*([win] * ATT_NB), dproj)


def _mem_probs(q, km, scale):
    s = lax.dot_general(q, km, (((1,), (1,)), ((), ())), preferred_element_type=F32) * scale
    s = s - jnp.max(s, axis=-1, keepdims=True)
    p = jnp.exp(s)
    return p * (1.0 / jnp.sum(p, axis=-1, keepdims=True))


def memattn_fwd(proj, kv, *, qcol, tq=512, name):
    T = proj.shape[0]
    M, dm2 = kv.shape
    DM = dm2 // 2
    hd = DM // N_MEM_HEADS
    tq = min(tq, T)

    def body(q_ref, kv_ref, o_ref):
        for h in range(N_MEM_HEADS):
            sl = pl.ds(h * hd, hd)
            p = _mem_probs(q_ref[:, sl], kv_ref[:, sl], hd ** -0.5)
            o = jnp.dot(p.astype(BF16), kv_ref[:, pl.ds(DM + h * hd, hd)], preferred_element_type=F32)
            o_ref[:, sl] = o.astype(o_ref.dtype)

    return pl.pallas_call(
        body, name=name, grid=(T // tq,),
        in_specs=[pl.BlockSpec((tq, DM), lambda i: (i, qcol // DM)),
                  pl.BlockSpec((M, 2 * DM), lambda i: (0, 0))],
        out_specs=pl.BlockSpec((tq, DM), lambda i: (i, 0)),
        out_shape=jax.ShapeDtypeStruct((T, DM), BF16),
        compiler_params=_cparams(("parallel",)),
    )(proj, kv)


def memattn_bwd(proj, kv, dout, dproj, *, qcol, tq=512, name):
    T = proj.shape[0]
    M, dm2 = kv.shape
    DM = dm2 // 2
    hd = DM // N_MEM_HEADS
    tq = min(tq, T)

    def body(q_ref, kv_ref, do_ref, _, dq_ref, dkv_ref):
        i = pl.program_id(0)

        @pl.when(i == 0)
        def _():
            dkv_ref[...] = jnp.zeros_like(dkv_ref)

        for h in range(N_MEM_HEADS):
            sl = pl.ds(h * hd, hd)
            vsl = pl.ds(DM + h * hd, hd)
            q = q_ref[:, sl]
            do = do_ref[:, sl]
            p = _mem_probs(q, kv_ref[:, sl], hd ** -0.5)
            dp = lax.dot_general(do, kv_ref[:, vsl], (((1,), (1,)), ((), ())), preferred_element_type=F32)
            dkv_ref[:, vsl] += lax.dot_general(p.astype(BF16), do, (((0,), (0,)), ((), ())),
                                               preferred_element_type=F32)
            ds = p * (dp - jnp.sum(dp * p, axis=-1, keepdims=True))
            dsb = (ds * (hd ** -0.5)).astype(BF16)
            dq_ref[:, sl] = jnp.dot(dsb, kv_ref[:, sl], preferred_element_type=F32).astype(dq_ref.dtype)
            dkv_ref[:, sl] += lax.dot_general(dsb, q, (((0,), (0,)), ((), ())), preferred_element_type=F32)

    return pl.pallas_call(
        body, name=name, grid=(T // tq,),
        in_specs=[pl.BlockSpec((tq, DM), lambda i: (i, qcol // DM)),
                  pl.BlockSpec((M, 2 * DM), lambda i: (0, 0)),
                  pl.BlockSpec((tq, DM), lambda i: (i, 0)),
                  pl.BlockSpec(memory_space=pl.ANY)],
        out_specs=[pl.BlockSpec((tq, DM), lambda i: (i, qcol // DM)),
                   pl.BlockSpec((M, 2 * DM), lambda i: (0, 0))],
        out_shape=[jax.ShapeDtypeStruct(dproj.shape, dproj.dtype), jax.ShapeDtypeStruct((M, 2 * DM), F32)],
        input_output_aliases={3: 0},
        compiler_params=_cparams(("arbitrary",)),
    )(proj, kv, dout, dproj)


def merge_fwd(proj, gate_b, ys, *, gcol, tr=512, tc=512, name):
    T = proj.shape[0]
    D = ys[0].shape[1]
    tr, tc = min(tr, T), _tile(D, tc)
    nd = D // tc

    def body(g0, g1, g2, b0, b1, b2, y0, y1, y2, o_ref):
        acc = jnp.zeros((tr, tc), F32)
        for g, b, y in ((g0, b0, y0), (g1, b1, y1), (g2, b2, y2)):
            acc = acc + jax.nn.sigmoid(g[...].astype(F32) + b[...]) * y[...].astype(F32)
        o_ref[...] = acc.astype(o_ref.dtype)

    def gate(b):
        return pl.BlockSpec((tr, tc), lambda i, j: (i, gcol // tc + b * nd + j))

    def bias(b):
        return pl.BlockSpec((1, tc), lambda i, j: (0, b * nd + j))

    blk = pl.BlockSpec((tr, tc), lambda i, j: (i, j))
    return pl.pallas_call(
        body, name=name, grid=(T // tr, nd),
        in_specs=[gate(0), gate(1), gate(2), bias(0), bias(1), bias(2), blk, blk, blk],
        out_specs=blk,
        out_shape=jax.ShapeDtypeStruct((T, D), BF16),
        compiler_params=_cparams(("parallel", "parallel")),
    )(proj, proj, proj, gate_b, gate_b, gate_b, *ys)


def merge_bwd(dmerged, proj, gate_b, ys, *, gcol, tr=512, tc=512, name):
    T, D_IN = proj.shape
    D = ys[0].shape[1]
    tr, tc = min(tr, T), _tile(D, tc)
    nd = D // tc

    def body(dm_ref, g_ref, b_ref, y0, y1, y2, dp_ref, dy_ref, db_ref):
        br = pl.program_id(0)
        i = pl.program_id(2)

        y = jnp.where(br == 0, y0[...], jnp.where(br == 1, y1[...], y2[...])).astype(F32)
        dm = dm_ref[...].astype(F32)
        sg = jax.nn.sigmoid(g_ref[...].astype(F32) + b_ref[...])
        dy_ref[...] = (dm * sg).astype(dy_ref.dtype)
        dg = dm * y * sg * (1.0 - sg)
        dp_ref[...] = dg.astype(dp_ref.dtype)

        @pl.when(i == 0)
        def _():
            db_ref[...] = jnp.zeros_like(db_ref)

        db_ref[...] += jnp.sum(dg, axis=0, keepdims=True)

    def ysp(b):
        return pl.BlockSpec((tr, tc), lambda br, j, i: (jnp.where(br == b, i, 0), jnp.where(br == b, j, 0)))

    return pl.pallas_call(
        body, name=name, grid=(3, nd, T // tr),
        in_specs=[pl.BlockSpec((tr, tc), lambda br, j, i: (i, j)),
                  pl.BlockSpec((tr, tc), lambda br, j, i: (i, gcol // tc + br * nd + j)),
                  pl.BlockSpec((1, tc), lambda br, j, i: (0, br * nd + j)),
                  ysp(0), ysp(1), ysp(2)],
        out_specs=[pl.BlockSpec((tr, tc), lambda br, j, i: (i, gcol // tc + br * nd + j)),
                   pl.BlockSpec((None, tr, tc), lambda br, j, i: (br, i, j)),
                   pl.BlockSpec((1, tc), lambda br, j, i: (0, br * nd + j))],
        out_shape=[jax.ShapeDtypeStruct((T, D_IN), BF16), jax.ShapeDtypeStruct((3, T, D), BF16),
                   jax.ShapeDtypeStruct((1, 3 * D), F32)],
        compiler_params=_cparams(("arbitrary", "arbitrary", "arbitrary")),
    )(dmerged, proj, gate_b, *ys)


FFN_CHUNK = 16
FFN_ROWS = 32


def _shift_down(tail, x, s):
    return pltpu.roll(jnp.concatenate([tail, x], axis=0), s, 0)[SUBLANES:]


def _shift_up(x, head, s):
    n = x.shape[0]
    return pltpu.roll(jnp.concatenate([x, head], axis=0), n + SUBLANES - s, 0)[:n]


def _ffn_taps(w_ref, b_ref):
    return [w_ref[pl.ds(k, 1), :] for k in range(FFN_CONV_WIDTH)] + [b_ref[...]]


def _ffn_conv(tail, x, taps):
    w0, w1, w2, b = taps
    x1, x2 = _shift_down(tail, x, 1), _shift_down(tail, x, 2)
    return w0 * x2 + w1 * x1 + w2 * x + b, x1, x2


def _ffn_rows(c):
    return pl.ds(c * FFN_ROWS if isinstance(c, int) else pl.multiple_of(c * FFN_ROWS, FFN_ROWS), FFN_ROWS)


def _ffn_specs(tt, tc, nv, order, with_next):
    hb = tt // FFN_CHUNK

    def mk(shape, f):
        return pl.BlockSpec(shape, (lambda i, j: f(i, j)) if order == "ij" else (lambda j, i: f(i, j)))

    def halo_after(T):
        return lambda i, j, off: (jnp.minimum((i + 1) * hb, T // FFN_CHUNK - 1), off + j)

    out = []
    for off in (0, nv):
        out += [mk((FFN_CHUNK, tc), lambda i, j, off=off: (jnp.maximum(i * hb - 1, 0), off + j)),
                mk((tt, tc), lambda i, j, off=off: (i, off + j))]
        if with_next:
            out.append(mk((FFN_CHUNK, tc), lambda i, j, off=off: halo_after(with_next)(i, j, off)))
    out += [mk((FFN_CONV_WIDTH, tc), lambda i, j, off=off: (0, off + j)) for off in (0, nv)]
    out += [mk((1, tc), lambda i, j, off=off: (0, off + j)) for off in (0, nv)]
    return out


def ffn_act_fwd(up0, w, b, *, tt=1024, tc=256, name):
    T, dff2 = up0.shape
    DFF = dff2 // 2
    tt, tc = min(tt, T), _tile(DFF, tc)
    nv = DFF // tc

    def body(vp, vm, gp, gm, wv, wg, bv, bg, o_ref):
        i = pl.program_id(0)
        taps_v, taps_g = _ffn_taps(wv, bv), _ffn_taps(wg, bg)

        def chunk(c, tails):
            rows = _ffn_rows(c)
            xv, xg = vm[rows, :].astype(F32), gm[rows, :].astype(F32)
            val = _ffn_conv(tails[0], xv, taps_v)[0]
            gt = _ffn_conv(tails[1], xg, taps_g)[0]
            o_ref[rows, :] = (gt * jax.nn.sigmoid(gt) * val).astype(o_ref.dtype)
            return xv[-SUBLANES:], xg[-SUBLANES:]

        before = lambda ref: jnp.where(i > 0, ref[...].astype(F32)[-SUBLANES:], 0.0)
        lax.fori_loop(0, tt // FFN_ROWS, chunk, (before(vp), before(gp)))

    return pl.pallas_call(
        body, name=name, grid=(T // tt, nv),
        in_specs=_ffn_specs(tt, tc, nv, "ij", None),
        out_specs=pl.BlockSpec((tt, tc), lambda i, j: (i, j)),
        out_shape=jax.ShapeDtypeStruct((T, DFF), BF16),
        compiler_params=_cparams(("parallel", "parallel")),
    )(up0, up0, up0, up0, w, w, b, b)


def ffn_bwd(dact, up0, w, b, *, tt=1024, tc=256, name):
    T, dff2 = up0.shape
    DFF = dff2 // 2
    tt, tc = min(tt, T), _tile(DFF, tc)
    nv = DFF // tc
    hb = tt // FFN_CHUNK
    n_t = T // tt
    n = tt // FFN_ROWS

    def body(da, dan, vp, vm, vn, gp, gm, gn, wv, wg, bv, bg, o_ref, dw_ref, db_ref, acc):
        i = pl.program_id(1)
        taps_v, taps_g = _ffn_taps(wv, bv), _ffn_taps(wg, bg)
        f32 = lambda ref, rows=slice(None): ref[rows, :].astype(F32)

        def act_grads(d_act, tail_v, xv, tail_g, xg):
            val, v1, v2 = _ffn_conv(tail_v, xv, taps_v)
            gt, g1, g2 = _ffn_conv(tail_g, xg, taps_g)
            sg = jax.nn.sigmoid(gt)
            return d_act * gt * sg, d_act * val * (sg * (1.0 + gt * (1.0 - sg))), (xv, v1, v2), (xg, g1, g2)

        tail = lambda ref, end: f32(ref, pl.ds(end - FFN_CHUNK, FFN_CHUNK))[-SUBLANES:]
        dnv, dng, _, _ = act_grads(f32(dan), tail(vm, tt), f32(vn), tail(gm, tt), f32(gn))
        heads = (jnp.where(i < n_t - 1, dnv[:SUBLANES], 0.0), jnp.where(i < n_t - 1, dng[:SUBLANES], 0.0))
        acc[...] = jnp.zeros_like(acc)

        def emit(c, tail_v, tail_g, heads):
            rows = _ffn_rows(c)
            dv, dg, xs_v, xs_g = act_grads(f32(da, rows), tail_v, f32(vm, rows), tail_g, f32(gm, rows))
            for half, (d, head, xs, taps) in enumerate(((dv, heads[0], xs_v, taps_v), (dg, heads[1], xs_g, taps_g))):
                w0, w1, w2, _ = taps
                o_ref[half, rows, :] = (w2 * d + w1 * _shift_up(d, head, 1) + w0 * _shift_up(d, head, 2)
                                        ).astype(o_ref.dtype)
                for k in range(FFN_CONV_WIDTH):
                    acc[4 * half + k] += d * xs[FFN_CONV_WIDTH - 1 - k]
                acc[4 * half + 3] += d
            return dv[:SUBLANES], dg[:SUBLANES]

        def chunk(k, heads):
            c = n - 1 - k
            end = pl.multiple_of(c * FFN_ROWS, FFN_ROWS)
            return emit(c, tail(vm, end), tail(gm, end), heads)

        heads = lax.fori_loop(0, n - 1, chunk, heads)
        before = lambda ref: jnp.where(i > 0, f32(ref)[-SUBLANES:], 0.0)
        emit(0, before(vp), before(gp), heads)

        @pl.when(i == 0)
        def _():
            dw_ref[...] = jnp.zeros_like(dw_ref)
            db_ref[...] = jnp.zeros_like(db_ref)

        for half in range(2):
            for k in range(FFN_CONV_WIDTH):
                dw_ref[half, pl.ds(k, 1), :] += jnp.sum(acc[4 * half + k], axis=0, keepdims=True)
            db_ref[half] += jnp.sum(acc[4 * half + 3], axis=0, keepdims=True)

    last_halo = T // FFN_CHUNK - 1
    return pl.pallas_call(
        body, name=name, grid=(nv, n_t),
        in_specs=[pl.BlockSpec((tt, tc), lambda j, i: (i, j)),
                  pl.BlockSpec((FFN_CHUNK, tc), lambda j, i: (jnp.minimum((i + 1) * hb, last_halo), j))]
                 + _ffn_specs(tt, tc, nv, "ji", T),
        out_specs=[pl.BlockSpec((2, tt, tc), lambda j, i: (0, i, j)),
                   pl.BlockSpec((2, FFN_CONV_WIDTH, tc), lambda j, i: (0, 0, j)),
                   pl.BlockSpec((2, 1, tc), lambda j, i: (0, 0, j))],
        out_shape=[jax.ShapeDtypeStruct((2, T, DFF), BF16), jax.ShapeDtypeStruct((2, FFN_CONV_WIDTH, DFF), F32),
                   jax.ShapeDtypeStruct((2, 1, DFF), F32)],
        scratch_shapes=[pltpu.VMEM((8, FFN_ROWS, tc), F32)],
        compiler_params=_cparams(("arbitrary", "arbitrary")),
    )(dact, dact, up0, up0, up0, up0, up0, up0, w, w, b, b)


def adamw(contribs, w, m, v, *, name):
    L, R, C = w.shape
    assert len(contribs) == L and all(c.shape == (N_DEV, R, C) for c in contribs)
    tr = R
    if R * C > 256 * 1024 and R % 8 == 0:
        tr = 8
        while R % (tr * 2) == 0 and tr * 2 * C <= 256 * 1024:
            tr *= 2
    c1 = 1.0 - ADAM_B1 ** ADAM_STEP
    c2 = 1.0 - ADAM_B2 ** ADAM_STEP

    def body(*refs):
        c_refs = refs[:L]
        w_ref, m_ref, v_ref, g_ref, d_ref, nm_ref, nv_ref = refs[L:]
        layer = pl.program_id(0)
        for lp in range(L):
            @pl.when(layer == lp)
            def _(c_ref=c_refs[lp]):
                g = c_ref[0].astype(F32)
                for s in range(1, N_DEV):
                    g = g + c_ref[s].astype(F32)
                g_ref[...] = g

        g = g_ref[...]
        nm = ADAM_B1 * m_ref[...] + (1.0 - ADAM_B1) * g
        nv = ADAM_B2 * v_ref[...] + (1.0 - ADAM_B2) * (g * g)
        nm_ref[...] = nm
        nv_ref[...] = nv
        d_ref[...] = -ADAM_LR * ((nm / c1) / (jnp.sqrt(nv / c2) + ADAM_EPS) + ADAM_WD * w_ref[...])

    def contrib_spec(lp):
        return pl.BlockSpec((N_DEV, tr, C), lambda l, i: (0, jnp.where(l == lp, i, 0), 0))

    blk = pl.BlockSpec((None, tr, C), lambda l, i: (l, i, 0))
    out = jax.ShapeDtypeStruct((L, R, C), F32)
    return pl.pallas_call(
        body, name=name, grid=(L, R // tr),
        in_specs=[contrib_spec(lp) for lp in range(L)] + [blk, blk, blk],
        out_specs=[blk, blk, blk, blk], out_shape=[out, out, out, out],
        compiler_params=_cparams(("arbitrary", "arbitrary")),
    )(*contribs, w, m, v)


def _my_position():
    x, y, c = lax.axis_index("x"), lax.axis_index("y"), lax.axis_index("c")
    return x, y, c, 4 * x + 2 * y + c


def _peers(x, y, c):
    out = []
    for r in range(1, N_DEV):
        px = 1 - x if r & 4 else x
        py = 1 - y if r & 2 else y
        pc = 1 - c if r & 1 else c
        out.append(((px, py, pc), 4 * px + 2 * py + pc))
    return out


def _run_exchange(plan, n, send_sems, recv_sems, local_sems):
    x, y, c, me = _my_position()
    copies = []
    for t in range(n):
        src, dst = plan(t, None, me)
        own = pltpu.make_async_copy(src, dst, local_sems.at[t])
        own.start()
        copies.append(own)
    remote = []
    for r, (peer, peer_index) in enumerate(_peers(x, y, c)):
        for t in range(n):
            src, dst = plan(t, peer_index, me)
            cp = pltpu.make_async_remote_copy(src_ref=src, dst_ref=dst, send_sem=send_sems.at[t, r],
                                              recv_sem=recv_sems.at[t, r], device_id=peer,
                                              device_id_type=pl.DeviceIdType.MESH)
            cp.start()
            remote.append(cp)
    for cp in remote:
        cp.wait_send()
    for cp in remote:
        cp.wait_recv()
    for cp in copies:
        cp.wait()


def all_gather(blocks, *, name):
    n = len(blocks)

    def body(*refs):
        srcs, outs = refs[:n], refs[n:2 * n]
        send_sems, recv_sems, local_sems = refs[2 * n:]
        _run_exchange(lambda t, peer_index, me: (srcs[t], outs[t].at[me]), n, send_sems, recv_sems, local_sems)

    hbm = pl.BlockSpec(memory_space=pl.ANY)
    return pl.pallas_call(
        body, name=name, in_specs=[hbm] * n, out_specs=[hbm] * n,
        out_shape=[jax.ShapeDtypeStruct((N_DEV, *b.shape), b.dtype) for b in blocks],
        scratch_shapes=[pltpu.SemaphoreType.DMA((n, N_DEV - 1)), pltpu.SemaphoreType.DMA((n, N_DEV - 1)),
                        pltpu.SemaphoreType.DMA((n,))],
    )(*blocks)


_RELATIONS = {"scatter": (1, 2, 3, 4, 5, 6, 7), "gather": (1, 2, 4, 6), "forward": (2, 4, 6)}

_HBM = pl.BlockSpec(memory_space=pltpu.HBM)
_SEM = pl.BlockSpec(memory_space=pltpu.SEMAPHORE)


def _push_copies(kind, src_refs, land_refs, send_sems, recv_sems):
    x, y, c, me = _my_position()
    peers = _peers(x, y, c)
    relations = _RELATIONS[kind]
    copies = []
    for k, r in enumerate(relations):
        peer, peer_index = peers[r - 1]
        for t, land in enumerate(land_refs):
            if kind == "forward":
                src, dst, peer = land.at[peer_index], land.at[peer_index], peers[0][0]
            else:
                src = src_refs[t] if kind == "gather" else src_refs[t].at[peer_index]
                dst = land.at[me]
            copies.append(pltpu.make_async_remote_copy(
                src_ref=src, dst_ref=dst, send_sem=send_sems.at[t * len(relations) + k],
                recv_sem=recv_sems.at[t * len(relations) + k], device_id=peer,
                device_id_type=pl.DeviceIdType.MESH))
    return copies


def push_start(srcs, lands, *, kind, name):
    ns, n = len(srcs), len(lands)

    def body(*refs):
        send_sems, recv_sems = refs[ns + n:ns + n + 2]
        token = refs[-1]
        for cp in _push_copies(kind, refs[:ns], refs[ns:ns + n], send_sems, recv_sems):
            cp.start()
        token[...] = jnp.zeros_like(token)

    sems = pltpu.SemaphoreType.DMA((n * len(_RELATIONS[kind]),))
    arrays = [*srcs, *lands]
    out = pl.pallas_call(
        body, name=name,
        out_shape=(sems, sems, *[pltpu.HBM(a.shape, a.dtype) for a in arrays],
                   jax.ShapeDtypeStruct((8, LANES), F32)),
        in_specs=[_HBM] * (ns + n),
        out_specs=(_SEM, _SEM, *[_HBM] * (ns + n), pl.BlockSpec(memory_space=pltpu.VMEM)),
        input_output_aliases={i: 2 + i for i in range(ns + n)},
        compiler_params=pltpu.CompilerParams(has_side_effects=pltpu.SideEffectType.DATAFLOW_SIDE_EFFECTING),
    )(*[pltpu.with_memory_space_constraint(a, pltpu.HBM) for a in arrays])
    return out[0], out[1], list(out[2:2 + ns]), list(out[2 + ns:2 + ns + n]), out[-1]


def push_wait(send_sems, recv_sems, srcs, lands, after, *, kind, name):
    ns, n = len(srcs), len(lands)

    def body(*refs):
        for cp in _push_copies(kind, refs[:ns], refs[ns:ns + n], refs[ns + n], refs[ns + n + 1]):
            cp.wait_send()
            cp.wait_recv()

    arrays = [*srcs, *lands]
    out = pl.pallas_call(
        body, name=name,
        out_shape=tuple(pltpu.HBM(a.shape, a.dtype) for a in arrays),
        in_specs=[_HBM] * (ns + n) + [_SEM, _SEM, pl.BlockSpec(memory_space=pl.ANY)],
        out_specs=tuple([_HBM] * (ns + n)),
        input_output_aliases={i: i for i in range(ns + n)},
        compiler_params=pltpu.CompilerParams(has_side_effects=pltpu.SideEffectType.DATAFLOW_SIDE_EFFECTING),
    )(*arrays, send_sems, recv_sems, after)
    return list(out[ns:])


def _own_slot(block, me):
    zone = lax.empty((N_DEV, *block.shape), block.dtype)
    return lax.dynamic_update_slice(zone, block[None], (me,) + (0,) * block.ndim)


WEIGHT_NAMES = ('mix_norm_g', 'mem_norm_g', 'w_in', 'gate_b', 'conv_w', 'conv_b', 'conv_ln_g', 'conv_ln_b',
                'w_conv_out', 'rel_bias', 'w_attn_out', 'w_mem_kv', 'w_mem_out', 'w_o', 'ffn_norm_g', 'w_up',
                'ffn_conv_w', 'ffn_conv_b', 'w_down', 'final_norm_g')
BIG = (('w_in', False), ('w_conv_out', False), ('w_attn_out', False), ('w_mem_out', False), ('w_up', False),
       ('w_mem_kv', True), ('w_o', True), ('w_down', True))
BY_ROWS = dict(BIG)
GATHER_GROUPS = (('w_in',), ('w_mem_kv', 'w_conv_out', 'w_attn_out', 'w_mem_out', 'w_o'), ('w_up', 'w_down'))


MIN_SLOT_COLS = 512


def _as_matrix(gathered, by_rows):
    n, r, c = gathered.shape
    if by_rows:
        return gathered.reshape(1, n * r, c)
    if c >= MIN_SLOT_COLS:
        return gathered
    return jnp.transpose(gathered, (1, 0, 2)).reshape(1, r, n * c)


def _as_shards(grad, by_rows):
    s, r, c = grad.shape
    if by_rows:
        return grad.reshape(N_DEV, r // N_DEV, c)
    if s == N_DEV:
        return grad
    return jnp.transpose(grad.reshape(r, N_DEV, c // N_DEV), (1, 0, 2))


class _LayerWeights:
    def __init__(self, pending):
        self.pending = pending
        self.ready = {}

    def advance(self, after, name=None):
        for item in self.pending:
            if item['trip'] == 1 and (name is None or name in item['names']):
                lands = push_wait(*item['started'][:4], after, kind="gather", name=item['tag'] + "_wait1")
                item['started'] = push_start([], lands, kind="forward", name=item['tag'] + "_start2")
                item['trip'] = 2

    def get(self, name, after):
        self.advance(after, name)
        for item in self.pending:
            if name in item['names'] and item['trip'] == 2:
                got = push_wait(*item['started'][:4], after, kind="forward", name=item['tag'] + "_wait2")
                self.ready.update({n: _as_matrix(a, BY_ROWS[n]) for n, a in zip(item['names'], got)})
                item['trip'] = None
        return self.ready[name]

    def __getitem__(self, name):
        return self.ready[name]


def _forward_layer(h, mem2, p, l, dims, prefetch):
    W = p['gathered'][l]
    row = lambda name: p[name][l:l + 1]
    sv = {'h': h}
    sv['xn'] = rms_fwd(h, row('mix_norm_g'), name="mix_norm")
    w_in = W.get('w_in', sv['xn'])
    sv['proj'] = mm_nn(sv['xn'], w_in, out_dtype=BF16, after=prefetch(w_in), name="w_in")
    sv['s'], sv['c'] = conv_fwd(sv['proj'], p['conv_w_full'][l], row('conv_b'), row('conv_ln_g'),
                                row('conv_ln_b'), name="conv_module")
    sv['bias'] = attn_bias(p['rel_pad'][l], name="attn_bias")
    sv['att'] = attn_fwd(sv['proj'], sv['bias'], qcol=dims['qcol'], dattn=dims['DA'], name="chunk_attn")
    sv['mn'] = rms_fwd(mem2, row('mem_norm_g'), name="mem_norm")
    sv['kv'] = mm_nn(sv['mn'], W.get('w_mem_kv', sv['att']), out_dtype=BF16, name="w_mem_kv")
    sv['mo'] = memattn_fwd(sv['proj'], sv['kv'], qcol=dims['mcol'], name="mem_attn")
    sv['ys'] = (mm_nn(sv['s'], W['w_conv_out'], out_dtype=BF16, name="w_conv_out"),
                mm_nn(sv['att'], W['w_attn_out'], out_dtype=BF16, name="w_attn_out"),
                mm_nn(sv['mo'], W['w_mem_out'], out_dtype=BF16, name="w_mem_out"))
    sv['merged'] = merge_fwd(sv['proj'], row('gate_b'), sv['ys'], gcol=dims['gcol'], name="merge")
    sv['h1'] = mm_nn(sv['merged'], W['w_o'], out_dtype=F32, residual=h, name="w_o")
    sv['hn'] = rms_fwd(sv['h1'], row('ffn_norm_g'), name="ffn_norm")
    sv['up0'] = mm_nn(sv['hn'], W.get('w_up', sv['hn']), out_dtype=BF16, name="w_up")
    sv['act'] = ffn_act_fwd(sv['up0'], p['ffn_conv_w_full'][l], row('ffn_conv_b'), name="ffn_act")
    if l + 1 < len(p['gathered']):
        p['gathered'][l + 1].advance(sv['act'])
    h2 = mm_nn(sv['act'], W['w_down'], out_dtype=F32, residual=sv['h1'], name="w_down")
    return h2, sv


SCATTER_GROUPS = (('w_down', 'w_up'), ('w_o', 'w_conv_out', 'w_attn_out', 'w_mem_out', 'w_mem_kv'), ('w_in',))


def _backward_layer(dh, dhb, mem2, sv, p, l, dims, scatter, after=None):
    W = p['gathered'][l]
    row = lambda name: p[name][l:l + 1]
    g, small = {}, {}
    dact = mm_nt(dhb, W['w_down'], out_dtype=BF16, after=after, name="d_act")
    g['w_down'] = mm_tn(sv['act'], dhb, slots=1, out_dtype=BF16, name="g_w_down")
    dup0, dtaps, dbias = ffn_bwd(dact, sv['up0'], p['ffn_conv_w_full'][l], row('ffn_conv_b'), name="ffn_bwd")
    small['ffn_conv_w'] = jnp.transpose(dtaps, (1, 0, 2)).reshape(FFN_CONV_WIDTH, -1)
    small['ffn_conv_b'] = jnp.transpose(dbias, (1, 0, 2)).reshape(1, -1)
    dhn = mm_nt(dup0, W['w_up'], lead="planes", out_dtype=BF16, name="d_hn")
    g['w_up'] = mm_tn(sv['hn'], dup0, lead="planes", slots=W['w_up'].shape[0], out_dtype=BF16, name="g_w_up")
    dh1, dh1b, small['ffn_norm_g'] = rms_bwd(sv['h1'], row('ffn_norm_g'), dhn, dh, name="ffn_norm_bwd")
    dmerged = mm_nt(dh1b, W['w_o'], out_dtype=BF16, after=scatter(SCATTER_GROUPS[0], g), name="d_merged")
    g['w_o'] = mm_tn(sv['merged'], dh1b, slots=1, out_dtype=BF16, name="g_w_o")
    dproj, dy, small['gate_b'] = merge_bwd(dmerged, sv['proj'], row('gate_b'), sv['ys'], gcol=dims['gcol'],
                                           name="merge_bwd")
    ds = mm_nt(dy, W['w_conv_out'], lead=0, out_dtype=BF16, name="d_conv_out")
    g['w_conv_out'] = mm_tn(sv['s'], dy, lead=0, slots=W['w_conv_out'].shape[0], out_dtype=BF16,
                            name="g_w_conv_out")
    dc, small['conv_ln_g'], small['conv_ln_b'] = conv_bwd_ln(ds, sv['c'], row('conv_ln_g'), row('conv_ln_b'),
                                                             name="conv_ln_bwd")
    dproj, small['conv_w'], small['conv_b'] = conv_bwd_taps(dc, sv['proj'], p['conv_w_full'][l], dproj,
                                                            name="conv_taps_bwd")
    datt = mm_nt(dy, W['w_attn_out'], lead=1, out_dtype=BF16, name="d_attn_out")
    g['w_attn_out'] = mm_tn(sv['att'], dy, lead=1, slots=W['w_attn_out'].shape[0], out_dtype=BF16,
                            name="g_w_attn_out")
    dproj, dkw, dvw, dbias = attn_bwd(sv['proj'], sv['bias'], datt, dproj, qcol=dims['qcol'], dattn=dims['DA'],
                                      name="chunk_attn_bwd")
    dproj = window_sum(dkw, dproj, col=dims['qcol'] + dims['DA'], name="dk_windows")
    dproj = window_sum(dvw, dproj, col=dims['qcol'] + 2 * dims['DA'], name="dv_windows")
    small['rel_bias'] = attn_bias_bwd(dbias, name="attn_bias_bwd")[:, :2 * MAX_REL + 1]
    dmo = mm_nt(dy, W['w_mem_out'], lead=2, out_dtype=BF16, name="d_mem_out")
    g['w_mem_out'] = mm_tn(sv['mo'], dy, lead=2, slots=W['w_mem_out'].shape[0], out_dtype=BF16,
                           name="g_w_mem_out")
    dproj, dkv = memattn_bwd(sv['proj'], sv['kv'], dmo, dproj, qcol=dims['mcol'], name="mem_attn_bwd")
    dkvb = dkv.astype(BF16)
    g['w_mem_kv'] = mm_tn(sv['mn'], dkvb, slots=1, out_dtype=BF16, name="g_w_mem_kv")
    dmn = mm_nt(dkvb, W['w_mem_kv'], out_dtype=BF16, name="d_mem_norm")
    _, _, small['mem_norm_g'] = rms_bwd(mem2, row('mem_norm_g'), dmn, jnp.zeros(mem2.shape, F32),
                                        name="mem_norm_bwd")
    dxn = mm_nt(dproj, W['w_in'], out_dtype=BF16, after=scatter(SCATTER_GROUPS[1], g), name="d_xn")
    g['w_in'] = mm_tn(sv['xn'], dproj, slots=N_DEV, out_dtype=BF16, name="g_w_in")
    dh0, dh0b, small['mix_norm_g'] = rms_bwd(sv['h'], row('mix_norm_g'), dxn, dh1, name="mix_norm_bwd")
    return dh0, dh0b, g['w_in'], small


def kernel(x, mem, mix_norm_g, mem_norm_g, w_in, gate_b, conv_w, conv_b, conv_ln_g, conv_ln_b, w_conv_out, rel_bias, w_attn_out, w_mem_kv, w_mem_out, w_o, ffn_norm_g, w_up, ffn_conv_w, ffn_conv_b, w_down, final_norm_g, loss_target, m_mix_norm_g, m_mem_norm_g, m_w_in, m_gate_b, m_conv_w, m_conv_b, m_conv_ln_g, m_conv_ln_b, m_w_conv_out, m_rel_bias, m_w_attn_out, m_w_mem_kv, m_w_mem_out, m_w_o, m_ffn_norm_g, m_w_up, m_ffn_conv_w, m_ffn_conv_b, m_w_down, m_final_norm_g, v_mix_norm_g, v_mem_norm_g, v_w_in, v_gate_b, v_conv_w, v_conv_b, v_conv_ln_g, v_conv_ln_b, v_w_conv_out, v_rel_bias, v_w_attn_out, v_w_mem_kv, v_w_mem_out, v_w_o, v_ffn_norm_g, v_w_up, v_ffn_conv_w, v_ffn_conv_b, v_w_down, v_final_norm_g):
    env = locals()
    w = {n: env[n] for n in WEIGHT_NAMES}
    mom = {n: env['m_' + n] for n in WEIGHT_NAMES}
    var = {n: env['v_' + n] for n in WEIGHT_NAMES}
    T, D = x.shape[-2:]
    L = w_in.shape[0]
    DC = conv_b.shape[-1]
    DA = N_DEV * w_attn_out.shape[-1] // 2
    dims = {'DA': DA, 'qcol': 2 * DC, 'mcol': 2 * DC + 3 * DA, 'gcol': 2 * DC + 3 * DA + D // 2}
    x2, mem2, target = x.reshape(T, D), mem.reshape(-1, D), loss_target.reshape(T, D)
    me = 4 * lax.axis_index("x") + 2 * lax.axis_index("y") + lax.axis_index("c")

    p = dict(w)

    def start_gather(l, names, first, tag):
        blocks = [w[n][l].astype(BF16) for n in names]
        first, blocks = lax.optimization_barrier((first, blocks))
        tag = "gather_%d%s" % (l, tag)
        started = push_start(blocks, [_own_slot(b, me) for b in blocks], kind="gather", name=tag + "_start1")
        return {'names': names, 'started': started, 'trip': 1, 'tag': tag}

    taps, ffn_taps = all_gather([conv_w, ffn_conv_w], name="gather_taps")
    p['conv_w_full'] = jnp.moveaxis(taps, 0, 2).reshape(L, conv_w.shape[1], -1)
    p['ffn_conv_w_full'] = jnp.moveaxis(ffn_taps, 0, 2).reshape(L, ffn_conv_w.shape[1], -1)
    p['rel_pad'] = jnp.pad(rel_bias, ((0, 0), (0, 0), (0, REL_PAD - rel_bias.shape[-1])))
    pending, first = [], taps
    for names, tag in zip(GATHER_GROUPS, "abc"):
        pending.append(start_gather(0, names, first, tag))
        first = pending[-1]['started'][4]
    p['gathered'] = [_LayerWeights(pending)] + [None] * (L - 1)

    h = x2
    saved = []
    for l in range(L):
        def prefetch(first, l=l):
            if l + 1 == L:
                return None
            nxt = start_gather(l + 1, [n for n, _ in BIG], first, "")
            p['gathered'][l + 1] = _LayerWeights([nxt])
            return nxt['started'][4]

        h, sv = _forward_layer(h, mem2, p, l, dims, prefetch)
        saved.append(sv)
    loss_part, dh, dhb, d_final = loss_head(h, final_norm_g.reshape(1, D), target, name="loss_head")
    loss = lax.psum(loss_part[0, 0], ("x", "y", "c"))

    small, landed, inflight = [None] * L, [{} for _ in range(L)], []

    def scatter_start(l, names, g):
        tag = "abc"[SCATTER_GROUPS.index(names)]
        grads = [_as_shards(g[n], BY_ROWS[n]) for n in names]
        zones = [_own_slot(lax.dynamic_index_in_dim(a, me, 0, keepdims=False), me) for a in grads]
        started = push_start(grads, zones, kind="scatter", name="scatter_start_%d%s" % (l, tag))
        inflight.append((l, names, started, "scatter_wait_%d%s" % (l, tag)))
        return started[4]

    def scatter_finish(after, groups):
        for item in [i for i in inflight if (i[0], i[1]) in groups]:
            l, names, started, wait_name = item
            landed[l].update(zip(names, push_wait(*started[:4], after, kind="scatter", name=wait_name)))
            inflight.remove(item)

    token = None
    for l in reversed(range(L)):
        dh, dhb, g_w_in, small[l] = _backward_layer(dh, dhb, mem2, saved[l], p, l, dims,
                                                    functools.partial(scatter_start, l), after=token)
        scatter_finish(dh, [(l + 1, names) for names in SCATTER_GROUPS])
        if l > 0:
            token = scatter_start(l, SCATTER_GROUPS[2], {'w_in': g_w_in})

    big_names = [n for n, _ in BIG]
    small_names = [n for n in WEIGHT_NAMES if n not in big_names and n != 'final_norm_g']
    stacked = [jnp.stack([small[l][n] for l in range(L)]) for n in small_names] + [d_final]
    got = dict(zip(small_names + ['final_norm_g'], all_gather(stacked, name="gather_small_grads")))
    g_w_in, _ = lax.optimization_barrier((g_w_in, got['final_norm_g']))
    token = scatter_start(0, SCATTER_GROUPS[2], {'w_in': g_w_in})
    scatter_finish(token, [(0, names) for names in SCATTER_GROUPS[:2]])
    for n in ('conv_w', 'ffn_conv_w'):
        cs = w[n].shape[-1]
        got[n] = lax.dynamic_slice_in_dim(got[n], me * cs, cs, axis=3)
    contribs = {n: [got[n].reshape(N_DEV, -1, w[n].shape[-1])] for n in small_names + ['final_norm_g']}

    outs = {}
    for n in [n for n in WEIGHT_NAMES if n != 'w_in'] + ['w_in']:
        if n == 'w_in':
            done = lax.optimization_barrier(tuple(o[0] for o in outs.values()))
            scatter_finish(done[0], [(0, SCATTER_GROUPS[2])])
        if n in big_names:
            contribs[n] = [landed[l][n] for l in range(L)]
        shp = (len(contribs[n]),) + contribs[n][0].shape[1:]
        res = adamw(contribs[n], w[n].reshape(shp), mom[n].reshape(shp), var[n].reshape(shp), name="adamw_" + n)
        outs[n] = [r.reshape(w[n].shape) for r in res]
    return (loss, dh.reshape(x.shape),
            *[outs[n][0] for n in WEIGHT_NAMES], *[outs[n][1] for n in WEIGHT_NAMES],
            *[outs[n][2] for n in WEIGHT_NAMES], *[outs[n][3] for n in WEIGHT_NAMES])
```

```python
import functools

import numpy as np
import jax
import jax.numpy as jnp
from jax import lax
from jax.experimental import pallas as pl
from jax.experimental.pallas import tpu as pltpu

F32 = jnp.float32
BF16 = jnp.bfloat16

CHUNK = 64
LEFT_CHUNKS = 8
MAX_REL = 256
N_MEM_HEADS = 4
ATTN_HEAD_DIM = 64
CONV_WIDTH = 31
FFN_CONV_WIDTH = 3
EPS = 1e-6
NEG_INF = -1e30
ADAM_LR = 0.001
ADAM_B1 = 0.9
ADAM_B2 = 0.999
ADAM_EPS = 1e-08
ADAM_WD = 0.01
ADAM_STEP = 10
N_DEV = 8

VMEM_LIMIT_BYTES = 56 * 1024 * 1024
LANES = 128
SUBLANES = 8


def _cparams(sem=None):
    return pltpu.CompilerParams(dimension_semantics=sem, vmem_limit_bytes=VMEM_LIMIT_BYTES)


def _tile(n, want):
    if n <= want:
        return n
    t = (want // LANES) * LANES
    while t >= LANES:
        if n % t == 0:
            return t if 4 * t >= want or n > 2 * want else n
        t -= LANES
    return n


MM_TM, MM_TN, MM_TK = 1024, 1536, 2048
W_DOWN_TK, W_DOWN_TM = 2816, 1408


def _matmul(name, a, b, *, dims, grid, nk, a_spec, b_spec, out_spec, out_shape, acc_shape, residual=None,
            after=None, slots_per_step=1):
    n_in = 2 + (residual is not None) + (after is not None)

    def body(*refs):
        a_ref, b_ref = refs[:2]
        o_ref = refs[n_in]

        def finish(r):
            if residual is not None:
                r = r + refs[2][...]
            o_ref[...] = r.astype(o_ref.dtype)

        if slots_per_step == 1:
            part = lax.dot_general(a_ref[...], b_ref[...], (dims, ((), ())), preferred_element_type=F32)
        else:
            tk = b_ref.shape[-1]
            part = functools.reduce(jnp.add, [
                lax.dot_general(a_ref[:, pl.ds(g * tk, tk)], b_ref[g], (dims, ((), ())), preferred_element_type=F32)
                for g in range(slots_per_step)])
        if nk == 1:
            finish(part)
            return
        acc = refs[-1]
        k = pl.program_id(2)

        @pl.when(k == 0)
        def _():
            acc[...] = part

        @pl.when((k > 0) & (k < nk - 1))
        def _():
            acc[...] += part

        @pl.when(k == nk - 1)
        def _():
            finish(acc[...] + part)

    in_specs, args = [a_spec, b_spec], [a, b]
    if residual is not None:
        in_specs.append(out_spec)
        args.append(residual)
    if after is not None:
        in_specs.append(pl.BlockSpec(memory_space=pl.ANY))
        args.append(after)
    return pl.pallas_call(
        body, name=name, grid=grid, in_specs=in_specs, out_specs=out_spec, out_shape=out_shape,
        scratch_shapes=[pltpu.VMEM(acc_shape, F32)] if nk > 1 else [],
        compiler_params=_cparams(("parallel", "parallel", "arbitrary")),
    )(*args)


def _lead_spec(arr, lead, block, index):
    if lead is None:
        assert arr.ndim == 2
        return pl.BlockSpec(block, index)
    assert arr.ndim == 3
    if lead == "planes":
        per_plane = arr.shape[2] // block[1]
        assert arr.shape[2] % block[1] == 0

        def planes_index(i, j, k):
            r, c = index(i, j, k)
            return c // per_plane, r, c % per_plane

        return pl.BlockSpec((None, *block), planes_index)
    return pl.BlockSpec((None, *block), lambda i, j, k: (lead, *index(i, j, k)))


def _matrix_shape(arr, lead):
    return (arr.shape[1], arr.shape[0] * arr.shape[2]) if lead == "planes" else arr.shape[-2:]


def mm_nn(a, w, *, out_dtype, residual=None, after=None, tm=MM_TM, tn=MM_TN, tk=MM_TK, name):
    M, K = a.shape
    S, K2, Ns = w.shape
    assert K == K2
    tm, tn, tk = _tile(M, tm), _tile(Ns, tn), _tile(K, tk)
    nb = Ns // tn
    return _matmul(
        name, a, w, dims=((1,), (0,)), grid=(M // tm, S * nb, K // tk), nk=K // tk,
        a_spec=pl.BlockSpec((tm, tk), lambda i, j, k: (i, k)),
        b_spec=pl.BlockSpec((None, tk, tn), lambda i, j, k: (j // nb, k, j % nb)),
        out_spec=pl.BlockSpec((tm, tn), lambda i, j, k: (i, j)),
        out_shape=jax.ShapeDtypeStruct((M, S * Ns), out_dtype), acc_shape=(tm, tn), residual=residual,
        after=after)


def mm_nt(a, w, *, out_dtype, lead=None, after=None, tm=MM_TM, tn=MM_TN, tk=MM_TK, name):
    M, N = _matrix_shape(a, lead)
    S, K, Ns = w.shape
    assert N == S * Ns
    tm, tn, tk = _tile(M, tm), _tile(K, tn), _tile(Ns, tk)
    nb = Ns // tk
    if nb == 1 and S % 2 == 0 and S > 2:
        return _matmul(
            name, a, w, dims=((1,), (1,)), grid=(M // tm, K // tn, S // 2), nk=S // 2,
            a_spec=_lead_spec(a, lead, (tm, 2 * tk), lambda i, j, k: (i, k)),
            b_spec=pl.BlockSpec((2, tn, tk), lambda i, j, k: (k, j, 0)),
            out_spec=pl.BlockSpec((tm, tn), lambda i, j, k: (i, j)),
            out_shape=jax.ShapeDtypeStruct((M, K), out_dtype), acc_shape=(tm, tn), after=after, slots_per_step=2)
    return _matmul(
        name, a, w, dims=((1,), (1,)), grid=(M // tm, K // tn, S * nb), nk=S * nb,
        a_spec=_lead_spec(a, lead, (tm, tk), lambda i, j, k: (i, k)),
        b_spec=pl.BlockSpec((None, tn, tk), lambda i, j, k: (k // nb, j, k % nb)),
        out_spec=pl.BlockSpec((tm, tn), lambda i, j, k: (i, j)),
        out_shape=jax.ShapeDtypeStruct((M, K), out_dtype), acc_shape=(tm, tn), after=after)


def mm_tn(a, b, *, slots, out_dtype, lead=None, tm=MM_TM, tn=MM_TN, tk=MM_TK, name):
    T, K = a.shape
    T2, N = _matrix_shape(b, lead)
    assert T == T2 and N % slots == 0
    Ns = N // slots
    tm, tn, tk = _tile(K, tm), _tile(Ns, tn), _tile(T, tk)
    nb = Ns // tn
    return _matmul(
        name, a, b, dims=((0,), (0,)), grid=(K // tm, slots * nb, T // tk), nk=T // tk,
        a_spec=pl.BlockSpec((tk, tm), lambda i, j, k: (k, i)),
        b_spec=_lead_spec(b, lead, (tk, tn), lambda i, j, k: (k, j)),
        out_spec=pl.BlockSpec((None, tm, tn), lambda i, j, k: (j // nb, i, j % nb)),
        out_shape=jax.ShapeDtypeStruct((slots, K, Ns), out_dtype), acc_shape=(tm, tn))


def rms_fwd(h, g, *, tr=256, name):
    T, D = h.shape
    tr = min(tr, T)

    def body(h_ref, g_ref, o_ref):
        x = h_ref[...]
        r = lax.rsqrt(jnp.mean(x * x, axis=-1, keepdims=True) + EPS)
        o_ref[...] = (x * r * g_ref[...]).astype(o_ref.dtype)

    return pl.pallas_call(
        body, name=name, grid=(T // tr,),
        in_specs=[pl.BlockSpec((tr, D), lambda i: (i, 0)), pl.BlockSpec((1, D), lambda i: (0, 0))],
        out_specs=pl.BlockSpec((tr, D), lambda i: (i, 0)),
        out_shape=jax.ShapeDtypeStruct((T, D), BF16),
        compiler_params=_cparams(("parallel",)),
    )(h, g)


def rms_bwd(h, g, dxn, dres, *, tr=256, name):
    T, D = h.shape
    tr = min(tr, T)

    def body(h_ref, g_ref, d_ref, r_ref, dh_ref, dhb_ref, dg_ref):
        i = pl.program_id(0)
        x = h_ref[...]
        r = lax.rsqrt(jnp.mean(x * x, axis=-1, keepdims=True) + EPS)
        xh = x * r
        d = d_ref[...].astype(F32)

        @pl.when(i == 0)
        def _():
            dg_ref[...] = jnp.zeros_like(dg_ref)

        dg_ref[...] += jnp.sum(d * xh, axis=0, keepdims=True)
        dxh = d * g_ref[...]
        dh = r * (dxh - xh * jnp.mean(dxh * xh, axis=-1, keepdims=True)) + r_ref[...]
        dh_ref[...] = dh
        dhb_ref[...] = dh.astype(BF16)

    row = pl.BlockSpec((tr, D), lambda i: (i, 0))
    vec = pl.BlockSpec((1, D), lambda i: (0, 0))
    return pl.pallas_call(
        body, name=name, grid=(T // tr,),
        in_specs=[row, vec, row, row],
        out_specs=[row, row, vec],
        out_shape=[jax.ShapeDtypeStruct((T, D), F32), jax.ShapeDtypeStruct((T, D), BF16),
                   jax.ShapeDtypeStruct((1, D), F32)],
        compiler_params=_cparams(("arbitrary",)),
    )(h, g, dxn, dres)


def loss_head(h, g, target, *, tr=256, name):
    T, D = h.shape
    tr = min(tr, T)

    def body(h_ref, g_ref, t_ref, loss_ref, dh_ref, dhb_ref, dg_ref):
        i = pl.program_id(0)
        x = h_ref[...]
        r = lax.rsqrt(jnp.mean(x * x, axis=-1, keepdims=True) + EPS)
        xh = x * r
        gg = g_ref[...]
        err = xh * gg - t_ref[...]

        @pl.when(i == 0)
        def _():
            dg_ref[...] = jnp.zeros_like(dg_ref)
            loss_ref[...] = jnp.zeros_like(loss_ref)

        loss_ref[...] += 0.5 * jnp.sum(jnp.mean(err * err, axis=-1, keepdims=True))
        dy = err * (1.0 / D)
        dg_ref[...] += jnp.sum(dy * xh, axis=0, keepdims=True)
        dxh = dy * gg
        dh = r * (dxh - xh * jnp.mean(dxh * xh, axis=-1, keepdims=True))
        dh_ref[...] = dh
        dhb_ref[...] = dh.astype(BF16)

    row = pl.BlockSpec((tr, D), lambda i: (i, 0))
    vec = pl.BlockSpec((1, D), lambda i: (0, 0))
    return pl.pallas_call(
        body, name=name, grid=(T // tr,),
        in_specs=[row, vec, row],
        out_specs=[pl.BlockSpec((8, LANES), lambda i: (0, 0)), row, row, vec],
        out_shape=[jax.ShapeDtypeStruct((8, LANES), F32), jax.ShapeDtypeStruct((T, D), F32),
                   jax.ShapeDtypeStruct((T, D), BF16), jax.ShapeDtypeStruct((1, D), F32)],
        compiler_params=_cparams(("arbitrary",)),
    )(h, g, target)


CONV_HALO = 32


def _glu_ext(prev_ref, main_ref, hs_ref, i, dc, tt):
    up = prev_ref[...].astype(F32)
    hp = up[:, :dc] * jax.nn.sigmoid(up[:, dc:])
    hs_ref[pl.ds(0, CONV_HALO), :] = jnp.where(i > 0, hp, 0.0)
    um = main_ref[...].astype(F32)
    sig = jax.nn.sigmoid(um[:, dc:])
    hs_ref[pl.ds(CONV_HALO, tt), :] = um[:, :dc] * sig


CONV_ROWS = 32


def _phase_copies(src_ref, ph_ref, rows):
    for b in range(1, SUBLANES):
        ph_ref[b - 1, pl.ds(0, rows), :] = src_ref[pl.ds(b, rows), :]


def _tap_rows(src_ref, ph_ref, offset, r0, n):
    a, b = divmod(offset, SUBLANES)
    ref = src_ref if b == 0 else ph_ref.at[b - 1]
    return ref[pl.ds(r0 + a * SUBLANES, n), :]


def _fold_rows(x):
    out = x[0:SUBLANES]
    for r in range(SUBLANES, x.shape[0], SUBLANES):
        out = out + x[r:r + SUBLANES]
    return out


def conv_fwd(proj, conv_w, conv_b, ln_g, ln_b, *, tt=256, name):
    T = proj.shape[0]
    W, DC = conv_w.shape
    tt = min(tt, T)
    hb = tt // CONV_HALO
    ph_rows = tt + CONV_HALO - SUBLANES

    def body(prev_ref, main_ref, w_ref, b_ref, g_ref, bb_ref, s_ref, c_ref, hs_ref, ph_ref):
        i = pl.program_id(0)
        _glu_ext(prev_ref, main_ref, hs_ref, i, DC, tt)
        _phase_copies(hs_ref, ph_ref, ph_rows)

        def chunk(cc, carry):
            r0 = pl.multiple_of(cc * CONV_ROWS, CONV_ROWS)
            c = jnp.zeros((CONV_ROWS, DC), F32) + b_ref[...]
            for k in range(W):
                c = c + w_ref[pl.ds(k, 1), :] * _tap_rows(hs_ref, ph_ref, CONV_HALO - (W - 1) + k, r0, CONV_ROWS)
            rows = pl.ds(r0, CONV_ROWS)
            c_ref[rows, :] = c
            mu = jnp.mean(c, axis=-1, keepdims=True)
            xc = c - mu
            var = jnp.mean(xc * xc, axis=-1, keepdims=True)
            y = xc * lax.rsqrt(var + EPS) * g_ref[...] + bb_ref[...]
            s_ref[rows, :] = (y * jax.nn.sigmoid(y)).astype(s_ref.dtype)
            return carry

        lax.fori_loop(0, tt // CONV_ROWS, chunk, 0)

    vec = pl.BlockSpec((1, DC), lambda i: (0, 0))
    return pl.pallas_call(
        body, name=name, grid=(T // tt,),
        in_specs=[pl.BlockSpec((CONV_HALO, 2 * DC), lambda i: (jnp.maximum(i * hb - 1, 0), 0)),
                  pl.BlockSpec((tt, 2 * DC), lambda i: (i, 0)),
                  pl.BlockSpec((W, DC), lambda i: (0, 0)), vec, vec, vec],
        out_specs=[pl.BlockSpec((tt, DC), lambda i: (i, 0)), pl.BlockSpec((tt, DC), lambda i: (i, 0))],
        out_shape=[jax.ShapeDtypeStruct((T, DC), BF16), jax.ShapeDtypeStruct((T, DC), F32)],
        scratch_shapes=[pltpu.VMEM((tt + CONV_HALO, DC), F32), pltpu.VMEM((SUBLANES - 1, ph_rows, DC), F32)],
        compiler_params=_cparams(("parallel",)),
    )(proj, proj, conv_w, conv_b, ln_g, ln_b)


def conv_bwd_ln(ds, c, ln_g, ln_b, *, tt=256, name):
    T, DC = c.shape
    tt = min(tt, T)

    def body(ds_ref, c_ref, g_ref, bb_ref, dc_ref, dg_ref, db_ref):
        i = pl.program_id(0)
        c = c_ref[...]
        mu = jnp.mean(c, axis=-1, keepdims=True)
        xc = c - mu
        var = jnp.mean(xc * xc, axis=-1, keepdims=True)
        rstd = lax.rsqrt(var + EPS)
        xh = xc * rstd
        y = xh * g_ref[...] + bb_ref[...]
        sg = jax.nn.sigmoid(y)
        dy = ds_ref[...].astype(F32) * (sg * (1.0 + y * (1.0 - sg)))

        @pl.when(i == 0)
        def _():
            dg_ref[...] = jnp.zeros_like(dg_ref)
            db_ref[...] = jnp.zeros_like(db_ref)

        db_ref[...] += jnp.sum(dy, axis=0, keepdims=True)
        dg_ref[...] += jnp.sum(dy * xh, axis=0, keepdims=True)
        dxh = dy * g_ref[...]
        dc_ref[...] = rstd * (dxh - jnp.mean(dxh, axis=-1, keepdims=True)
                              - xh * jnp.mean(dxh * xh, axis=-1, keepdims=True))

    row = pl.BlockSpec((tt, DC), lambda i: (i, 0))
    vec = pl.BlockSpec((1, DC), lambda i: (0, 0))
    return pl.pallas_call(
        body, name=name, grid=(T // tt,),
        in_specs=[row, row, vec, vec], out_specs=[row, vec, vec],
        out_shape=[jax.ShapeDtypeStruct((T, DC), F32), jax.ShapeDtypeStruct((1, DC), F32),
                   jax.ShapeDtypeStruct((1, DC), F32)],
        compiler_params=_cparams(("arbitrary",)),
    )(ds, c, ln_g, ln_b)


def conv_bwd_taps(dc, proj, conv_w, dproj, *, tt=256, name):
    T, DC = dc.shape
    W = conv_w.shape[0]
    tt = min(tt, T)
    hb = tt // CONV_HALO
    n_t = T // tt
    last_halo = T // CONV_HALO - 1
    ph_rows = tt + CONV_HALO - SUBLANES

    def body(dc_ref, dcn_ref, prev_ref, main_ref, w_ref, _, dp_ref, dw_ref, db_ref, hs_ref, ds_ref, hph_ref,
             dph_ref, acc_ref):
        i = pl.program_id(0)
        _glu_ext(prev_ref, main_ref, hs_ref, i, DC, tt)
        ds_ref[pl.ds(0, tt), :] = dc_ref[...]
        ds_ref[pl.ds(tt, CONV_HALO), :] = jnp.where(i < n_t - 1, dcn_ref[...], 0.0)
        _phase_copies(hs_ref, hph_ref, ph_rows)
        _phase_copies(ds_ref, dph_ref, ph_rows)
        acc_ref[...] = jnp.zeros_like(acc_ref)

        def chunk(cc, carry):
            r0 = pl.multiple_of(cc * CONV_ROWS, CONV_ROWS)
            rows = pl.ds(r0, CONV_ROWS)
            d = ds_ref[rows, :]
            dh = jnp.zeros((CONV_ROWS, DC), F32)
            for k in range(W):
                dh = dh + w_ref[pl.ds(k, 1), :] * _tap_rows(ds_ref, dph_ref, W - 1 - k, r0, CONV_ROWS)
                acc_ref[k] += _fold_rows(d * _tap_rows(hs_ref, hph_ref, CONV_HALO - (W - 1) + k, r0, CONV_ROWS))
            acc_ref[W] += _fold_rows(d)
            um = main_ref[rows, :].astype(F32)
            a, sig = um[:, :DC], jax.nn.sigmoid(um[:, DC:])
            dp_ref[rows, :] = jnp.concatenate([dh * sig, dh * a * sig * (1.0 - sig)], axis=1).astype(dp_ref.dtype)
            return carry

        lax.fori_loop(0, tt // CONV_ROWS, chunk, 0)

        @pl.when(i == 0)
        def _():
            dw_ref[...] = jnp.zeros_like(dw_ref)
            db_ref[...] = jnp.zeros_like(db_ref)

        for k in range(W):
            dw_ref[pl.ds(k, 1), :] += jnp.sum(acc_ref[k], axis=0, keepdims=True)
        db_ref[...] += jnp.sum(acc_ref[W], axis=0, keepdims=True)

    return pl.pallas_call(
        body, name=name, grid=(n_t,),
        in_specs=[pl.BlockSpec((tt, DC), lambda i: (i, 0)),
                  pl.BlockSpec((CONV_HALO, DC), lambda i: (jnp.minimum((i + 1) * hb, last_halo), 0)),
                  pl.BlockSpec((CONV_HALO, 2 * DC), lambda i: (jnp.maximum(i * hb - 1, 0), 0)),
                  pl.BlockSpec((tt, 2 * DC), lambda i: (i, 0)),
                  pl.BlockSpec((W, DC), lambda i: (0, 0)),
                  pl.BlockSpec(memory_space=pl.ANY)],
        out_specs=[pl.BlockSpec((tt, 2 * DC), lambda i: (i, 0)),
                   pl.BlockSpec((W, DC), lambda i: (0, 0)), pl.BlockSpec((1, DC), lambda i: (0, 0))],
        out_shape=[jax.ShapeDtypeStruct(dproj.shape, dproj.dtype), jax.ShapeDtypeStruct((W, DC), F32),
                   jax.ShapeDtypeStruct((1, DC), F32)],
        scratch_shapes=[pltpu.VMEM((tt + CONV_HALO, DC), F32), pltpu.VMEM((tt + CONV_HALO, DC), F32),
                        pltpu.VMEM((SUBLANES - 1, ph_rows, DC), F32), pltpu.VMEM((SUBLANES - 1, ph_rows, DC), F32),
                        pltpu.VMEM((W + 1, SUBLANES, DC), F32)],
        input_output_aliases={5: 0},
        compiler_params=_cparams(("arbitrary",)),
    )(dc, dc, proj, proj, conv_w, dproj)


ATT_TQ = 256
ATT_NB = 1 + (LEFT_CHUNKS * CHUNK) // ATT_TQ
ATT_WIN = ATT_NB * ATT_TQ
ATT_PERIOD = 1024
REL_PAD = 640


def _rel_onehot():
    m = lax.broadcasted_iota(jnp.int32, (REL_PAD, ATT_PERIOD), 1)
    r = lax.broadcasted_iota(jnp.int32, (REL_PAD, ATT_PERIOD), 0)
    qk = jnp.where(m < ATT_PERIOD - ATT_TQ, -m, ATT_PERIOD - m)
    idx = jnp.clip(LEFT_CHUNKS * CHUNK + qk, -MAX_REL, MAX_REL) + MAX_REL
    return (idx == r).astype(F32)


def _band_valid():
    q = lax.broadcasted_iota(jnp.int32, (ATT_TQ, ATT_WIN), 0)
    k = lax.broadcasted_iota(jnp.int32, (ATT_TQ, ATT_WIN), 1)
    j = k // CHUNK - q // CHUNK
    return (j >= 0) & (j <= LEFT_CHUNKS)


def attn_bias(rel_pad, *, name):
    H = rel_pad.shape[0]

    def body(rb_ref, o_ref):
        g = jnp.dot(rb_ref[...], _rel_onehot(), preferred_element_type=F32, precision=lax.Precision.HIGHEST)
        valid = _band_valid()
        for h in range(H):
            row = jnp.broadcast_to(g[h:h + 1, :], (ATT_TQ, ATT_PERIOD))
            t = pltpu.roll(row, 0, 1, stride=1, stride_axis=0)
            o_ref[h] = jnp.where(valid, t[:, :ATT_WIN], NEG_INF)

    return pl.pallas_call(
        body, name=name,
        in_specs=[pl.BlockSpec(memory_space=pltpu.VMEM)], out_specs=pl.BlockSpec(memory_space=pltpu.VMEM),
        out_shape=jax.ShapeDtypeStruct((H, ATT_TQ, ATT_WIN), F32),
        compiler_params=_cparams(),
    )(rel_pad)


def attn_bias_bwd(dbias, *, name):
    H = dbias.shape[0]
    band = (LEFT_CHUNKS + 1) * CHUNK

    def body(d_ref, o_ref, acc_ref):
        r = lax.broadcasted_iota(jnp.int32, (CHUNK, CHUNK), 0)
        c = lax.broadcasted_iota(jnp.int32, (CHUNK, CHUNK), 1)
        rev = (r + c == CHUNK - 1).astype(F32)
        pad = jnp.zeros((CHUNK, ATT_PERIOD - ATT_WIN), F32)
        for h in range(H):
            tab = d_ref[h, pl.ds(0, CHUNK), :]
            for cq in range(1, ATT_TQ // CHUNK):
                tab = tab + pltpu.roll(d_ref[h, pl.ds(cq * CHUNK, CHUNK), :], ATT_WIN - cq * CHUNK, 1)
            t = jnp.dot(rev, tab, preferred_element_type=F32, precision=lax.Precision.HIGHEST)
            u = pltpu.roll(jnp.concatenate([t, pad], axis=1), 0, 1, stride=1, stride_axis=0)
            acc_ref[pl.ds(h, 1), :] = jnp.sum(u, axis=0, keepdims=True)
        j = lax.broadcasted_iota(jnp.int32, (REL_PAD, ATT_PERIOD), 1)
        rr = lax.broadcasted_iota(jnp.int32, (REL_PAD, ATT_PERIOD), 0)
        idx = jnp.clip(LEFT_CHUNKS * CHUNK + CHUNK - 1 - j, -MAX_REL, MAX_REL) + MAX_REL
        onehot = ((idx == rr) & (j < band + CHUNK - 1)).astype(F32)
        o_ref[...] = lax.dot_general(acc_ref[...], onehot, (((1,), (1,)), ((), ())),
                                     preferred_element_type=F32, precision=lax.Precision.HIGHEST)

    return pl.pallas_call(
        body, name=name,
        in_specs=[pl.BlockSpec(memory_space=pltpu.VMEM)], out_specs=pl.BlockSpec(memory_space=pltpu.VMEM),
        out_shape=jax.ShapeDtypeStruct((H, REL_PAD), F32),
        scratch_shapes=[pltpu.VMEM((H, ATT_PERIOD), F32)],
        compiler_params=_cparams(),
    )(dbias)


def _attn_specs(T, qcol, dattn):
    dn = dattn // LANES

    def kv(col0, back):
        return pl.BlockSpec((ATT_TQ, LANES), lambda hp, i: (jnp.maximum(i - back, 0), col0 // LANES + hp))

    q = pl.BlockSpec((ATT_TQ, LANES), lambda hp, i: (i, qcol // LANES + hp))
    ks = [kv(qcol + dattn, ATT_NB - 1 - b) for b in range(ATT_NB)]
    vs = [kv(qcol + 2 * dattn, ATT_NB - 1 - b) for b in range(ATT_NB)]
    return q, ks, vs


def _attn_scores(q, kw, bias, i):
    s = lax.dot_general(q, kw, (((1,), (1,)), ((), ())), preferred_element_type=F32)
    s = s * (ATTN_HEAD_DIM ** -0.5) + bias
    col = lax.broadcasted_iota(jnp.int32, s.shape, 1)
    s = jnp.where(col // ATT_TQ + i >= ATT_NB - 1, s, NEG_INF)
    s = s - jnp.max(s, axis=-1, keepdims=True)
    p = jnp.exp(s)
    return p * (1.0 / jnp.sum(p, axis=-1, keepdims=True))


def attn_fwd(proj, bias, *, qcol, dattn, name):
    T = proj.shape[0]
    HP = dattn // LANES
    q_spec, k_specs, v_specs = _attn_specs(T, qcol, dattn)

    def body(*refs):
        q_ref = refs[0]
        k_refs = refs[1:1 + ATT_NB]
        v_refs = refs[1 + ATT_NB:1 + 2 * ATT_NB]
        b_ref, o_ref = refs[1 + 2 * ATT_NB:]
        i = pl.program_id(1)
        lane = lax.broadcasted_iota(jnp.int32, (ATT_TQ, LANES), 1)
        kw = jnp.concatenate([r[...] for r in k_refs], axis=0)
        vw = jnp.concatenate([r[...] for r in v_refs], axis=0)
        q = q_ref[...]
        out = jnp.zeros((ATT_TQ, LANES), F32)
        for hh in range(2):
            mine = (lane // ATTN_HEAD_DIM) == hh
            p = _attn_scores(jnp.where(mine, q, jnp.zeros_like(q)), kw, b_ref[hh], i)
            o = jnp.dot(p.astype(BF16), vw, preferred_element_type=F32)
            out = jnp.where(mine, o, out)
        o_ref[...] = out.astype(o_ref.dtype)

    return pl.pallas_call(
        body, name=name, grid=(HP, T // ATT_TQ),
        in_specs=[q_spec, *k_specs, *v_specs,
                  pl.BlockSpec((2, ATT_TQ, ATT_WIN), lambda hp, i: (hp, 0, 0))],
        out_specs=pl.BlockSpec((ATT_TQ, LANES), lambda hp, i: (i, hp)),
        out_shape=jax.ShapeDtypeStruct((T, dattn), BF16),
        compiler_params=_cparams(("parallel", "parallel")),
    )(*([proj] * (1 + 2 * ATT_NB)), bias)


def attn_bwd(proj, bias, dout, dproj, *, qcol, dattn, name):
    T = proj.shape[0]
    HP = dattn // LANES
    nq = T // ATT_TQ
    q_spec, k_specs, v_specs = _attn_specs(T, qcol, dattn)

    def body(*refs):
        q_ref = refs[0]
        k_refs = refs[1:1 + ATT_NB]
        v_refs = refs[1 + ATT_NB:1 + 2 * ATT_NB]
        b_ref, do_ref, _, dq_ref, dk_ref, dv_ref, db_ref = refs[1 + 2 * ATT_NB:]
        i = pl.program_id(1)
        lane = lax.broadcasted_iota(jnp.int32, (ATT_TQ, LANES), 1)
        kw = jnp.concatenate([r[...] for r in k_refs], axis=0)
        vw = jnp.concatenate([r[...] for r in v_refs], axis=0)
        q = q_ref[...]
        do = do_ref[...]
        dq = jnp.zeros((ATT_TQ, LANES), F32)
        dk = jnp.zeros((ATT_WIN, LANES), F32)
        dv = jnp.zeros((ATT_WIN, LANES), F32)

        @pl.when(i == 0)
        def _():
            db_ref[...] = jnp.zeros_like(db_ref)

        for hh in range(2):
            mine = (lane // ATTN_HEAD_DIM) == hh
            qh = jnp.where(mine, q, jnp.zeros_like(q))
            doh = jnp.where(mine, do, jnp.zeros_like(do))
            p = _attn_scores(qh, kw, b_ref[hh], i)
            dp = lax.dot_general(doh, vw, (((1,), (1,)), ((), ())), preferred_element_type=F32)
            dv = dv + lax.dot_general(p.astype(BF16), doh, (((0,), (0,)), ((), ())), preferred_element_type=F32)
            ds = p * (dp - jnp.sum(dp * p, axis=-1, keepdims=True))
            db_ref[hh] += ds
            dsb = (ds * (ATTN_HEAD_DIM ** -0.5)).astype(BF16)
            dq = jnp.where(mine, jnp.dot(dsb, kw, preferred_element_type=F32), dq)
            dk = dk + lax.dot_general(dsb, qh, (((0,), (0,)), ((), ())), preferred_element_type=F32)
        dq_ref[...] = dq.astype(dq_ref.dtype)
        dk_ref[...] = dk
        dv_ref[...] = dv

    win = pl.BlockSpec((None, ATT_WIN, LANES), lambda hp, i: (i, 0, hp))
    return pl.pallas_call(
        body, name=name, grid=(HP, nq),
        in_specs=[q_spec, *k_specs, *v_specs,
                  pl.BlockSpec((2, ATT_TQ, ATT_WIN), lambda hp, i: (hp, 0, 0)),
                  pl.BlockSpec((ATT_TQ, LANES), lambda hp, i: (i, hp)),
                  pl.BlockSpec(memory_space=pl.ANY)],
        out_specs=[pl.BlockSpec((ATT_TQ, LANES), lambda hp, i: (i, qcol // LANES + hp)), win, win,
                   pl.BlockSpec((2, ATT_TQ, ATT_WIN), lambda hp, i: (hp, 0, 0))],
        out_shape=[jax.ShapeDtypeStruct(dproj.shape, dproj.dtype),
                   jax.ShapeDtypeStruct((nq, ATT_WIN, dattn), F32), jax.ShapeDtypeStruct((nq, ATT_WIN, dattn), F32),
                   jax.ShapeDtypeStruct((2 * HP, ATT_TQ, ATT_WIN), F32)],
        input_output_aliases={3 + 2 * ATT_NB: 0},
        compiler_params=_cparams(("arbitrary", "arbitrary")),
    )(*([proj] * (1 + 2 * ATT_NB)), bias, dout, dproj)


def window_sum(win, dproj, *, col, name):
    nq, _, C = win.shape
    assert col % C == 0

    def body(*refs):
        w_refs = refs[:ATT_NB]
        o_ref = refs[ATT_NB + 1]
        j = pl.program_id(0)
        acc = w_refs[0][...]
        for b in range(1, ATT_NB):
            acc = acc + jnp.where(j + b < nq, w_refs[b][...], 0.0)
        o_ref[...] = acc.astype(o_ref.dtype)

    def part(b):
        return pl.BlockSpec((None, ATT_TQ, C), lambda j: (jnp.minimum(j + b, nq - 1), ATT_NB - 1 - b, 0))

    return pl.pallas_call(
        body, name=name, grid=(nq,),
        in_specs=[part(b) for b in range(ATT_NB)] + [pl.BlockSpec(memory_space=pl.ANY)],
        out_specs=pl.BlockSpec((ATT_TQ, C), lambda j: (j, col // C)),
        out_shape=jax.ShapeDtypeStruct(dproj.shape, dproj.dtype),
        input_output_aliases={ATT_NB: 0},
        compiler_params=_cparams(("arbitrary",)),
    )(*([win] * ATT_NB), dproj)


def _mem_probs(q, km, scale):
    s = lax.dot_general(q, km, (((1,), (1,)), ((), ())), preferred_element_type=F32) * scale
    s = s - jnp.max(s, axis=-1, keepdims=True)
    p = jnp.exp(s)
    return p * (1.0 / jnp.sum(p, axis=-1, keepdims=True))


def memattn_fwd(proj, kv, *, qcol, tq=512, name):
    T = proj.shape[0]
    M, dm2 = kv.shape
    DM = dm2 // 2
    hd = DM // N_MEM_HEADS
    tq = min(tq, T)

    def body(q_ref, kv_ref, o_ref):
        for h in range(N_MEM_HEADS):
            sl = pl.ds(h * hd, hd)
            p = _mem_probs(q_ref[:, sl], kv_ref[:, sl], hd ** -0.5)
            o = jnp.dot(p.astype(BF16), kv_ref[:, pl.ds(DM + h * hd, hd)], preferred_element_type=F32)
            o_ref[:, sl] = o.astype(o_ref.dtype)

    return pl.pallas_call(
        body, name=name, grid=(T // tq,),
        in_specs=[pl.BlockSpec((tq, DM), lambda i: (i, qcol // DM)),
                  pl.BlockSpec((M, 2 * DM), lambda i: (0, 0))],
        out_specs=pl.BlockSpec((tq, DM), lambda i: (i, 0)),
        out_shape=jax.ShapeDtypeStruct((T, DM), BF16),
        compiler_params=_cparams(("parallel",)),
    )(proj, kv)


def memattn_bwd(proj, kv, dout, dproj, *, qcol, tq=512, name):
    T = proj.shape[0]
    M, dm2 = kv.shape
    DM = dm2 // 2
    hd = DM // N_MEM_HEADS
    tq = min(tq, T)

    def body(q_ref, kv_ref, do_ref, _, dq_ref, dkv_ref):
        i = pl.program_id(0)

        @pl.when(i == 0)
        def _():
            dkv_ref[...] = jnp.zeros_like(dkv_ref)

        for h in range(N_MEM_HEADS):
            sl = pl.ds(h * hd, hd)
            vsl = pl.ds(DM + h * hd, hd)
            q = q_ref[:, sl]
            do = do_ref[:, sl]
            p = _mem_probs(q, kv_ref[:, sl], hd ** -0.5)
            dp = lax.dot_general(do, kv_ref[:, vsl], (((1,), (1,)), ((), ())), preferred_element_type=F32)
            dkv_ref[:, vsl] += lax.dot_general(p.astype(BF16), do, (((0,), (0,)), ((), ())),
                                               preferred_element_type=F32)
            ds = p * (dp - jnp.sum(dp * p, axis=-1, keepdims=True))
            dsb = (ds * (hd ** -0.5)).astype(BF16)
            dq_ref[:, sl] = jnp.dot(dsb, kv_ref[:, sl], preferred_element_type=F32).astype(dq_ref.dtype)
            dkv_ref[:, sl] += lax.dot_general(dsb, q, (((0,), (0,)), ((), ())), preferred_element_type=F32)

    return pl.pallas_call(
        body, name=name, grid=(T // tq,),
        in_specs=[pl.BlockSpec((tq, DM), lambda i: (i, qcol // DM)),
                  pl.BlockSpec((M, 2 * DM), lambda i: (0, 0)),
                  pl.BlockSpec((tq, DM), lambda i: (i, 0)),
                  pl.BlockSpec(memory_space=pl.ANY)],
        out_specs=[pl.BlockSpec((tq, DM), lambda i: (i, qcol // DM)),
                   pl.BlockSpec((M, 2 * DM), lambda i: (0, 0))],
        out_shape=[jax.ShapeDtypeStruct(dproj.shape, dproj.dtype), jax.ShapeDtypeStruct((M, 2 * DM), F32)],
        input_output_aliases={3: 0},
        compiler_params=_cparams(("arbitrary",)),
    )(proj, kv, dout, dproj)


def merge_fwd(proj, gate_b, ys, *, gcol, tr=512, tc=512, name):
    T = proj.shape[0]
    D = ys[0].shape[1]
    tr, tc = min(tr, T), _tile(D, tc)
    nd = D // tc

    def body(g0, g1, g2, b0, b1, b2, y0, y1, y2, o_ref):
        acc = jnp.zeros((tr, tc), F32)
        for g, b, y in ((g0, b0, y0), (g1, b1, y1), (g2, b2, y2)):
            acc = acc + jax.nn.sigmoid(g[...].astype(F32) + b[...]) * y[...].astype(F32)
        o_ref[...] = acc.astype(o_ref.dtype)

    def gate(b):
        return pl.BlockSpec((tr, tc), lambda i, j: (i, gcol // tc + b * nd + j))

    def bias(b):
        return pl.BlockSpec((1, tc), lambda i, j: (0, b * nd + j))

    blk = pl.BlockSpec((tr, tc), lambda i, j: (i, j))
    return pl.pallas_call(
        body, name=name, grid=(T // tr, nd),
        in_specs=[gate(0), gate(1), gate(2), bias(0), bias(1), bias(2), blk, blk, blk],
        out_specs=blk,
        out_shape=jax.ShapeDtypeStruct((T, D), BF16),
        compiler_params=_cparams(("parallel", "parallel")),
    )(proj, proj, proj, gate_b, gate_b, gate_b, *ys)


def merge_bwd(dmerged, proj, gate_b, ys, *, gcol, tr=512, tc=512, name):
    T, D_IN = proj.shape
    D = ys[0].shape[1]
    tr, tc = min(tr, T), _tile(D, tc)
    nd = D // tc

    def body(dm_ref, g_ref, b_ref, y0, y1, y2, dp_ref, dy_ref, db_ref):
        br = pl.program_id(0)
        i = pl.program_id(2)

        y = jnp.where(br == 0, y0[...], jnp.where(br == 1, y1[...], y2[...])).astype(F32)
        dm = dm_ref[...].astype(F32)
        sg = jax.nn.sigmoid(g_ref[...].astype(F32) + b_ref[...])
        dy_ref[...] = (dm * sg).astype(dy_ref.dtype)
        dg = dm * y * sg * (1.0 - sg)
        dp_ref[...] = dg.astype(dp_ref.dtype)

        @pl.when(i == 0)
        def _():
            db_ref[...] = jnp.zeros_like(db_ref)

        db_ref[...] += jnp.sum(dg, axis=0, keepdims=True)

    def ysp(b):
        return pl.BlockSpec((tr, tc), lambda br, j, i: (jnp.where(br == b, i, 0), jnp.where(br == b, j, 0)))

    return pl.pallas_call(
        body, name=name, grid=(3, nd, T // tr),
        in_specs=[pl.BlockSpec((tr, tc), lambda br, j, i: (i, j)),
                  pl.BlockSpec((tr, tc), lambda br, j, i: (i, gcol // tc + br * nd + j)),
                  pl.BlockSpec((1, tc), lambda br, j, i: (0, br * nd + j)),
                  ysp(0), ysp(1), ysp(2)],
        out_specs=[pl.BlockSpec((tr, tc), lambda br, j, i: (i, gcol // tc + br * nd + j)),
                   pl.BlockSpec((None, tr, tc), lambda br, j, i: (br, i, j)),
                   pl.BlockSpec((1, tc), lambda br, j, i: (0, br * nd + j))],
        out_shape=[jax.ShapeDtypeStruct((T, D_IN), BF16), jax.ShapeDtypeStruct((3, T, D), BF16),
                   jax.ShapeDtypeStruct((1, 3 * D), F32)],
        compiler_params=_cparams(("arbitrary", "arbitrary", "arbitrary")),
    )(dmerged, proj, gate_b, *ys)


FFN_CHUNK = 16
FFN_ROWS = 32


def _shift_down(tail, x, s):
    return pltpu.roll(jnp.concatenate([tail, x], axis=0), s, 0)[SUBLANES:]


def _shift_up(x, head, s):
    n = x.shape[0]
    return pltpu.roll(jnp.concatenate([x, head], axis=0), n + SUBLANES - s, 0)[:n]


def _ffn_taps(w_ref, b_ref):
    return [w_ref[pl.ds(k, 1), :] for k in range(FFN_CONV_WIDTH)] + [b_ref[...]]


def _ffn_conv(tail, x, taps):
    w0, w1, w2, b = taps
    x1, x2 = _shift_down(tail, x, 1), _shift_down(tail, x, 2)
    return w0 * x2 + w1 * x1 + w2 * x + b, x1, x2


def _ffn_rows(c):
    return pl.ds(c * FFN_ROWS if isinstance(c, int) else pl.multiple_of(c * FFN_ROWS, FFN_ROWS), FFN_ROWS)


def _ffn_specs(tt, tc, nv, order, with_next):
    hb = tt // FFN_CHUNK

    def mk(shape, f):
        return pl.BlockSpec(shape, (lambda i, j: f(i, j)) if order == "ij" else (lambda j, i: f(i, j)))

    def halo_after(T):
        return lambda i, j, off: (jnp.minimum((i + 1) * hb, T // FFN_CHUNK - 1), off + j)

    out = []
    for off in (0, nv):
        out += [mk((FFN_CHUNK, tc), lambda i, j, off=off: (jnp.maximum(i * hb - 1, 0), off + j)),
                mk((tt, tc), lambda i, j, off=off: (i, off + j))]
        if with_next:
            out.append(mk((FFN_CHUNK, tc), lambda i, j, off=off: halo_after(with_next)(i, j, off)))
    out += [mk((FFN_CONV_WIDTH, tc), lambda i, j, off=off: (0, off + j)) for off in (0, nv)]
    out += [mk((1, tc), lambda i, j, off=off: (0, off + j)) for off in (0, nv)]
    return out


def ffn_act_fwd(up0, w, b, *, tt=1024, tc=256, name):
    T, dff2 = up0.shape
    DFF = dff2 // 2
    tt, tc = min(tt, T), _tile(DFF, tc)
    nv = DFF // tc

    def body(vp, vm, gp, gm, wv, wg, bv, bg, o_ref):
        i = pl.program_id(0)
        taps_v, taps_g = _ffn_taps(wv, bv), _ffn_taps(wg, bg)

        def chunk(c, tails):
            rows = _ffn_rows(c)
            xv, xg = vm[rows, :].astype(F32), gm[rows, :].astype(F32)
            val = _ffn_conv(tails[0], xv, taps_v)[0]
            gt = _ffn_conv(tails[1], xg, taps_g)[0]
            o_ref[rows, :] = (gt * jax.nn.sigmoid(gt) * val).astype(o_ref.dtype)
            return xv[-SUBLANES:], xg[-SUBLANES:]

        before = lambda ref: jnp.where(i > 0, ref[...].astype(F32)[-SUBLANES:], 0.0)
        lax.fori_loop(0, tt // FFN_ROWS, chunk, (before(vp), before(gp)))

    return pl.pallas_call(
        body, name=name, grid=(T // tt, nv),
        in_specs=_ffn_specs(tt, tc, nv, "ij", None),
        out_specs=pl.BlockSpec((tt, tc), lambda i, j: (i, j)),
        out_shape=jax.ShapeDtypeStruct((T, DFF), BF16),
        compiler_params=_cparams(("parallel", "parallel")),
    )(up0, up0, up0, up0, w, w, b, b)


def ffn_bwd(dact, up0, w, b, *, tt=1024, tc=256, name):
    T, dff2 = up0.shape
    DFF = dff2 // 2
    tt, tc = min(tt, T), _tile(DFF, tc)
    nv = DFF // tc
    hb = tt // FFN_CHUNK
    n_t = T // tt
    n = tt // FFN_ROWS

    def body(da, dan, vp, vm, vn, gp, gm, gn, wv, wg, bv, bg, o_ref, dw_ref, db_ref, acc):
        i = pl.program_id(1)
        taps_v, taps_g = _ffn_taps(wv, bv), _ffn_taps(wg, bg)
        f32 = lambda ref, rows=slice(None): ref[rows, :].astype(F32)

        def act_grads(d_act, tail_v, xv, tail_g, xg):
            val, v1, v2 = _ffn_conv(tail_v, xv, taps_v)
            gt, g1, g2 = _ffn_conv(tail_g, xg, taps_g)
            sg = jax.nn.sigmoid(gt)
            return d_act * gt * sg, d_act * val * (sg * (1.0 + gt * (1.0 - sg))), (xv, v1, v2), (xg, g1, g2)

        tail = lambda ref, end: f32(ref, pl.ds(end - FFN_CHUNK, FFN_CHUNK))[-SUBLANES:]
        dnv, dng, _, _ = act_grads(f32(dan), tail(vm, tt), f32(vn), tail(gm, tt), f32(gn))
        heads = (jnp.where(i < n_t - 1, dnv[:SUBLANES], 0.0), jnp.where(i < n_t - 1, dng[:SUBLANES], 0.0))
        acc[...] = jnp.zeros_like(acc)

        def emit(c, tail_v, tail_g, heads):
            rows = _ffn_rows(c)
            dv, dg, xs_v, xs_g = act_grads(f32(da, rows), tail_v, f32(vm, rows), tail_g, f32(gm, rows))
            for half, (d, head, xs, taps) in enumerate(((dv, heads[0], xs_v, taps_v), (dg, heads[1], xs_g, taps_g))):
                w0, w1, w2, _ = taps
                o_ref[half, rows, :] = (w2 * d + w1 * _shift_up(d, head, 1) + w0 * _shift_up(d, head, 2)
                                        ).astype(o_ref.dtype)
                for k in range(FFN_CONV_WIDTH):
                    acc[4 * half + k] += d * xs[FFN_CONV_WIDTH - 1 - k]
                acc[4 * half + 3] += d
            return dv[:SUBLANES], dg[:SUBLANES]

        def chunk(k, heads):
            c = n - 1 - k
            end = pl.multiple_of(c * FFN_ROWS, FFN_ROWS)
            return emit(c, tail(vm, end), tail(gm, end), heads)

        heads = lax.fori_loop(0, n - 1, chunk, heads)
        before = lambda ref: jnp.where(i > 0, f32(ref)[-SUBLANES:], 0.0)
        emit(0, before(vp), before(gp), heads)

        @pl.when(i == 0)
        def _():
            dw_ref[...] = jnp.zeros_like(dw_ref)
            db_ref[...] = jnp.zeros_like(db_ref)

        for half in range(2):
            for k in range(FFN_CONV_WIDTH):
                dw_ref[half, pl.ds(k, 1), :] += jnp.sum(acc[4 * half + k], axis=0, keepdims=True)
            db_ref[half] += jnp.sum(acc[4 * half + 3], axis=0, keepdims=True)

    last_halo = T // FFN_CHUNK - 1
    return pl.pallas_call(
        body, name=name, grid=(nv, n_t),
        in_specs=[pl.BlockSpec((tt, tc), lambda j, i: (i, j)),
                  pl.BlockSpec((FFN_CHUNK, tc), lambda j, i: (jnp.minimum((i + 1) * hb, last_halo), j))]
                 + _ffn_specs(tt, tc, nv, "ji", T),
        out_specs=[pl.BlockSpec((2, tt, tc), lambda j, i: (0, i, j)),
                   pl.BlockSpec((2, FFN_CONV_WIDTH, tc), lambda j, i: (0, 0, j)),
                   pl.BlockSpec((2, 1, tc), lambda j, i: (0, 0, j))],
        out_shape=[jax.ShapeDtypeStruct((2, T, DFF), BF16), jax.ShapeDtypeStruct((2, FFN_CONV_WIDTH, DFF), F32),
                   jax.ShapeDtypeStruct((2, 1, DFF), F32)],
        scratch_shapes=[pltpu.VMEM((8, FFN_ROWS, tc), F32)],
        compiler_params=_cparams(("arbitrary", "arbitrary")),
    )(dact, dact, up0, up0, up0, up0, up0, up0, w, w, b, b)


def adamw(contribs, w, m, v, *, name):
    L, R, C = w.shape
    assert len(contribs) == L and all(c.shape == (N_DEV, R, C) for c in contribs)
    tr = R
    if R * C > 256 * 1024 and R % 8 == 0:
        tr = 8
        while R % (tr * 2) == 0 and tr * 2 * C <= 256 * 1024:
            tr *= 2
    c1 = 1.0 - ADAM_B1 ** ADAM_STEP
    c2 = 1.0 - ADAM_B2 ** ADAM_STEP

    def body(*refs):
        c_refs = refs[:L]
        w_ref, m_ref, v_ref, g_ref, d_ref, nm_ref, nv_ref = refs[L:]
        layer = pl.program_id(0)
        for lp in range(L):
            @pl.when(layer == lp)
            def _(c_ref=c_refs[lp]):
                g = c_ref[0].astype(F32)
                for s in range(1, N_DEV):
                    g = g + c_ref[s].astype(F32)
                g_ref[...] = g

        g = g_ref[...]
        nm = ADAM_B1 * m_ref[...] + (1.0 - ADAM_B1) * g
        nv = ADAM_B2 * v_ref[...] + (1.0 - ADAM_B2) * (g * g)
        nm_ref[...] = nm
        nv_ref[...] = nv
        d_ref[...] = -ADAM_LR * ((nm / c1) / (jnp.sqrt(nv / c2) + ADAM_EPS) + ADAM_WD * w_ref[...])

    def contrib_spec(lp):
        return pl.BlockSpec((N_DEV, tr, C), lambda l, i: (0, jnp.where(l == lp, i, 0), 0))

    blk = pl.BlockSpec((None, tr, C), lambda l, i: (l, i, 0))
    out = jax.ShapeDtypeStruct((L, R, C), F32)
    return pl.pallas_call(
        body, name=name, grid=(L, R // tr),
        in_specs=[contrib_spec(lp) for lp in range(L)] + [blk, blk, blk],
        out_specs=[blk, blk, blk, blk], out_shape=[out, out, out, out],
        compiler_params=_cparams(("arbitrary", "arbitrary")),
    )(*contribs, w, m, v)


def _my_position():
    x, y, c = lax.axis_index("x"), lax.axis_index("y"), lax.axis_index("c")
    return x, y, c, 4 * x + 2 * y + c


def _peers(x, y, c):
    out = []
    for r in range(1, N_DEV):
        px = 1 - x if r & 4 else x
        py = 1 - y if r & 2 else y
        pc = 1 - c if r & 1 else c
        out.append(((px, py, pc), 4 * px + 2 * py + pc))
    return out


def _run_exchange(plan, n, send_sems, recv_sems, local_sems):
    x, y, c, me = _my_position()
    copies = []
    for t in range(n):
        src, dst = plan(t, None, me)
        own = pltpu.make_async_copy(src, dst, local_sems.at[t])
        own.start()
        copies.append(own)
    remote = []
    for r, (peer, peer_index) in enumerate(_peers(x, y, c)):
        for t in range(n):
            src, dst = plan(t, peer_index, me)
            cp = pltpu.make_async_remote_copy(src_ref=src, dst_ref=dst, send_sem=send_sems.at[t, r],
                                              recv_sem=recv_sems.at[t, r], device_id=peer,
                                              device_id_type=pl.DeviceIdType.MESH)
            cp.start()
            remote.append(cp)
    for cp in remote:
        cp.wait_send()
    for cp in remote:
        cp.wait_recv()
    for cp in copies:
        cp.wait()


def all_gather(blocks, *, name):
    n = len(blocks)

    def body(*refs):
        srcs, outs = refs[:n], refs[n:2 * n]
        send_sems, recv_sems, local_sems = refs[2 * n:]
        _run_exchange(lambda t, peer_index, me: (srcs[t], outs[t].at[me]), n, send_sems, recv_sems, local_sems)

    hbm = pl.BlockSpec(memory_space=pl.ANY)
    return pl.pallas_call(
        body, name=name, in_specs=[hbm] * n, out_specs=[hbm] * n,
        out_shape=[jax.ShapeDtypeStruct((N_DEV, *b.shape), b.dtype) for b in blocks],
        scratch_shapes=[pltpu.SemaphoreType.DMA((n, N_DEV - 1)), pltpu.SemaphoreType.DMA((n, N_DEV - 1)),
                        pltpu.SemaphoreType.DMA((n,))],
    )(*blocks)


_RELATIONS = {"scatter": (1, 2, 3, 4, 5, 6, 7), "gather": (1, 2, 4, 6), "forward": (2, 4, 6)}

_HBM = pl.BlockSpec(memory_space=pltpu.HBM)
_SEM = pl.BlockSpec(memory_space=pltpu.SEMAPHORE)


def _push_copies(kind, src_refs, land_refs, send_sems, recv_sems):
    x, y, c, me = _my_position()
    peers = _peers(x, y, c)
    relations = _RELATIONS[kind]
    copies = []
    for k, r in enumerate(relations):
        peer, peer_index = peers[r - 1]
        for t, land in enumerate(land_refs):
            if kind == "forward":
                src, dst, peer = land.at[peer_index], land.at[peer_index], peers[0][0]
            else:
                src = src_refs[t] if kind == "gather" else src_refs[t].at[peer_index]
                dst = land.at[me]
            copies.append(pltpu.make_async_remote_copy(
                src_ref=src, dst_ref=dst, send_sem=send_sems.at[t * len(relations) + k],
                recv_sem=recv_sems.at[t * len(relations) + k], device_id=peer,
                device_id_type=pl.DeviceIdType.MESH))
    return copies


def push_start(srcs, lands, *, kind, name):
    ns, n = len(srcs), len(lands)

    def body(*refs):
        send_sems, recv_sems = refs[ns + n:ns + n + 2]
        token = refs[-1]
        for cp in _push_copies(kind, refs[:ns], refs[ns:ns + n], send_sems, recv_sems):
            cp.start()
        token[...] = jnp.zeros_like(token)

    sems = pltpu.SemaphoreType.DMA((n * len(_RELATIONS[kind]),))
    arrays = [*srcs, *lands]
    out = pl.pallas_call(
        body, name=name,
        out_shape=(sems, sems, *[pltpu.HBM(a.shape, a.dtype) for a in arrays],
                   jax.ShapeDtypeStruct((8, LANES), F32)),
        in_specs=[_HBM] * (ns + n),
        out_specs=(_SEM, _SEM, *[_HBM] * (ns + n), pl.BlockSpec(memory_space=pltpu.VMEM)),
        input_output_aliases={i: 2 + i for i in range(ns + n)},
        compiler_params=pltpu.CompilerParams(has_side_effects=pltpu.SideEffectType.DATAFLOW_SIDE_EFFECTING),
    )(*[pltpu.with_memory_space_constraint(a, pltpu.HBM) for a in arrays])
    return out[0], out[1], list(out[2:2 + ns]), list(out[2 + ns:2 + ns + n]), out[-1]


def push_wait(send_sems, recv_sems, srcs, lands, after, *, kind, name):
    ns, n = len(srcs), len(lands)

    def body(*refs):
        for cp in _push_copies(kind, refs[:ns], refs[ns:ns + n], refs[ns + n], refs[ns + n + 1]):
            cp.wait_send()
            cp.wait_recv()

    arrays = [*srcs, *lands]
    out = pl.pallas_call(
        body, name=name,
        out_shape=tuple(pltpu.HBM(a.shape, a.dtype) for a in arrays),
        in_specs=[_HBM] * (ns + n) + [_SEM, _SEM, pl.BlockSpec(memory_space=pl.ANY)],
        out_specs=tuple([_HBM] * (ns + n)),
        input_output_aliases={i: i for i in range(ns + n)},
        compiler_params=pltpu.CompilerParams(has_side_effects=pltpu.SideEffectType.DATAFLOW_SIDE_EFFECTING),
    )(*arrays, send_sems, recv_sems, after)
    return list(out[ns:])


def _own_slot(block, me):
    zone = lax.empty((N_DEV, *block.shape), block.dtype)
    return lax.dynamic_update_slice(zone, block[None], (me,) + (0,) * block.ndim)


WEIGHT_NAMES = ('mix_norm_g', 'mem_norm_g', 'w_in', 'gate_b', 'conv_w', 'conv_b', 'conv_ln_g', 'conv_ln_b',
                'w_conv_out', 'rel_bias', 'w_attn_out', 'w_mem_kv', 'w_mem_out', 'w_o', 'ffn_norm_g', 'w_up',
                'ffn_conv_w', 'ffn_conv_b', 'w_down', 'final_norm_g')
BIG = (('w_in', False), ('w_conv_out', False), ('w_attn_out', False), ('w_mem_out', False), ('w_up', False),
       ('w_mem_kv', True), ('w_o', True), ('w_down', True))
BY_ROWS = dict(BIG)
GATHER_GROUPS = (('w_in',), ('w_mem_kv', 'w_conv_out', 'w_attn_out', 'w_mem_out', 'w_o'), ('w_up', 'w_down'))


MIN_SLOT_COLS = 512


def _as_matrix(gathered, by_rows):
    n, r, c = gathered.shape
    if by_rows:
        return gathered.reshape(1, n * r, c)
    if c >= MIN_SLOT_COLS:
        return gathered
    return jnp.transpose(gathered, (1, 0, 2)).reshape(1, r, n * c)


def _as_shards(grad, by_rows):
    s, r, c = grad.shape
    if by_rows:
        return grad.reshape(N_DEV, r // N_DEV, c)
    if s == N_DEV:
        return grad
    return jnp.transpose(grad.reshape(r, N_DEV, c // N_DEV), (1, 0, 2))


class _LayerWeights:
    def __init__(self, pending):
        self.pending = pending
        self.ready = {}

    def advance(self, after, name=None):
        for item in self.pending:
            if item['trip'] == 1 and (name is None or name in item['names']):
                lands = push_wait(*item['started'][:4], after, kind="gather", name=item['tag'] + "_wait1")
                item['started'] = push_start([], lands, kind="forward", name=item['tag'] + "_start2")
                item['trip'] = 2

    def get(self, name, after):
        self.advance(after, name)
        for item in self.pending:
            if name in item['names'] and item['trip'] == 2:
                got = push_wait(*item['started'][:4], after, kind="forward", name=item['tag'] + "_wait2")
                self.ready.update({n: _as_matrix(a, BY_ROWS[n]) for n, a in zip(item['names'], got)})
                item['trip'] = None
        return self.ready[name]

    def __getitem__(self, name):
        return self.ready[name]


def _forward_layer(h, mem2, p, l, dims, prefetch):
    W = p['gathered'][l]
    row = lambda name: p[name][l:l + 1]
    sv = {'h': h}
    sv['xn'] = rms_fwd(h, row('mix_norm_g'), name="mix_norm")
    w_in = W.get('w_in', sv['xn'])
    sv['proj'] = mm_nn(sv['xn'], w_in, out_dtype=BF16, after=prefetch(w_in), name="w_in")
    sv['s'], sv['c'] = conv_fwd(sv['proj'], p['conv_w_full'][l], row('conv_b'), row('conv_ln_g'),
                                row('conv_ln_b'), name="conv_module")
    sv['bias'] = attn_bias(p['rel_pad'][l], name="attn_bias")
    sv['att'] = attn_fwd(sv['proj'], sv['bias'], qcol=dims['qcol'], dattn=dims['DA'], name="chunk_attn")
    sv['mn'] = rms_fwd(mem2, row('mem_norm_g'), name="mem_norm")
    sv['kv'] = mm_nn(sv['mn'], W.get('w_mem_kv', sv['att']), out_dtype=BF16, name="w_mem_kv")
    sv['mo'] = memattn_fwd(sv['proj'], sv['kv'], qcol=dims['mcol'], name="mem_attn")
    sv['ys'] = (mm_nn(sv['s'], W['w_conv_out'], out_dtype=BF16, name="w_conv_out"),
                mm_nn(sv['att'], W['w_attn_out'], out_dtype=BF16, name="w_attn_out"),
                mm_nn(sv['mo'], W['w_mem_out'], out_dtype=BF16, name="w_mem_out"))
    sv['merged'] = merge_fwd(sv['proj'], row('gate_b'), sv['ys'], gcol=dims['gcol'], name="merge")
    sv['h1'] = mm_nn(sv['merged'], W['w_o'], out_dtype=F32, residual=h, name="w_o")
    sv['hn'] = rms_fwd(sv['h1'], row('ffn_norm_g'), name="ffn_norm")
    sv['up0'] = mm_nn(sv['hn'], W.get('w_up', sv['hn']), out_dtype=BF16, name="w_up")
    sv['act'] = ffn_act_fwd(sv['up0'], p['ffn_conv_w_full'][l], row('ffn_conv_b'), name="ffn_act")
    if l + 1 < len(p['gathered']):
        p['gathered'][l + 1].advance(sv['act'])
    h2 = mm_nn(sv['act'], W['w_down'], out_dtype=F32, residual=sv['h1'], tk=W_DOWN_TK, name="w_down")
    return h2, sv


SCATTER_GROUPS = (('w_down', 'w_up'), ('w_o', 'w_conv_out', 'w_attn_out', 'w_mem_out', 'w_mem_kv'), ('w_in',))


def _backward_layer(dh, dhb, mem2, sv, p, l, dims, scatter, after=None):
    W = p['gathered'][l]
    row = lambda name: p[name][l:l + 1]
    g, small = {}, {}
    dact = mm_nt(dhb, W['w_down'], out_dtype=BF16, after=after, name="d_act")
    g['w_down'] = mm_tn(sv['act'], dhb, slots=1, out_dtype=BF16, tm=W_DOWN_TM, name="g_w_down")
    dup0, dtaps, dbias = ffn_bwd(dact, sv['up0'], p['ffn_conv_w_full'][l], row('ffn_conv_b'), name="ffn_bwd")
    small['ffn_conv_w'] = jnp.transpose(dtaps, (1, 0, 2)).reshape(FFN_CONV_WIDTH, -1)
    small['ffn_conv_b'] = jnp.transpose(dbias, (1, 0, 2)).reshape(1, -1)
    dhn = mm_nt(dup0, W['w_up'], lead="planes", out_dtype=BF16, name="d_hn")
    g['w_up'] = mm_tn(sv['hn'], dup0, lead="planes", slots=W['w_up'].shape[0], out_dtype=BF16, name="g_w_up")
    dh1, dh1b, small['ffn_norm_g'] = rms_bwd(sv['h1'], row('ffn_norm_g'), dhn, dh, name="ffn_norm_bwd")
    dmerged = mm_nt(dh1b, W['w_o'], out_dtype=BF16, after=scatter(SCATTER_GROUPS[0], g), name="d_merged")
    g['w_o'] = mm_tn(sv['merged'], dh1b, slots=1, out_dtype=BF16, name="g_w_o")
    dproj, dy, small['gate_b'] = merge_bwd(dmerged, sv['proj'], row('gate_b'), sv['ys'], gcol=dims['gcol'],
                                           name="merge_bwd")
    ds = mm_nt(dy, W['w_conv_out'], lead=0, out_dtype=BF16, name="d_conv_out")
    g['w_conv_out'] = mm_tn(sv['s'], dy, lead=0, slots=W['w_conv_out'].shape[0], out_dtype=BF16,
                            name="g_w_conv_out")
    dc, small['conv_ln_g'], small['conv_ln_b'] = conv_bwd_ln(ds, sv['c'], row('conv_ln_g'), row('conv_ln_b'),
                                                             name="conv_ln_bwd")
    dproj, small['conv_w'], small['conv_b'] = conv_bwd_taps(dc, sv['proj'], p['conv_w_full'][l], dproj,
                                                            name="conv_taps_bwd")
    datt = mm_nt(dy, W['w_attn_out'], lead=1, out_dtype=BF16, name="d_attn_out")
    g['w_attn_out'] = mm_tn(sv['att'], dy, lead=1, slots=W['w_attn_out'].shape[0], out_dtype=BF16,
                            name="g_w_attn_out")
    dproj, dkw, dvw, dbias = attn_bwd(sv['proj'], sv['bias'], datt, dproj, qcol=dims['qcol'], dattn=dims['DA'],
                                      name="chunk_attn_bwd")
    dproj = window_sum(dkw, dproj, col=dims['qcol'] + dims['DA'], name="dk_windows")
    dproj = window_sum(dvw, dproj, col=dims['qcol'] + 2 * dims['DA'], name="dv_windows")
    small['rel_bias'] = attn_bias_bwd(dbias, name="attn_bias_bwd")[:, :2 * MAX_REL + 1]
    dmo = mm_nt(dy, W['w_mem_out'], lead=2, out_dtype=BF16, name="d_mem_out")
    g['w_mem_out'] = mm_tn(sv['mo'], dy, lead=2, slots=W['w_mem_out'].shape[0], out_dtype=BF16,
                           name="g_w_mem_out")
    dproj, dkv = memattn_bwd(sv['proj'], sv['kv'], dmo, dproj, qcol=dims['mcol'], name="mem_attn_bwd")
    dkvb = dkv.astype(BF16)
    g['w_mem_kv'] = mm_tn(sv['mn'], dkvb, slots=1, out_dtype=BF16, name="g_w_mem_kv")
    dmn = mm_nt(dkvb, W['w_mem_kv'], out_dtype=BF16, name="d_mem_norm")
    _, _, small['mem_norm_g'] = rms_bwd(mem2, row('mem_norm_g'), dmn, jnp.zeros(mem2.shape, F32),
                                        name="mem_norm_bwd")
    dxn = mm_nt(dproj, W['w_in'], out_dtype=BF16, after=scatter(SCATTER_GROUPS[1], g), name="d_xn")
    g['w_in'] = mm_tn(sv['xn'], dproj, slots=N_DEV, out_dtype=BF16, name="g_w_in")
    dh0, dh0b, small['mix_norm_g'] = rms_bwd(sv['h'], row('mix_norm_g'), dxn, dh1, name="mix_norm_bwd")
    return dh0, dh0b, g['w_in'], small


def kernel(x, mem, mix_norm_g, mem_norm_g, w_in, gate_b, conv_w, conv_b, conv_ln_g, conv_ln_b, w_conv_out, rel_bias, w_attn_out, w_mem_kv, w_mem_out, w_o, ffn_norm_g, w_up, ffn_conv_w, ffn_conv_b, w_down, final_norm_g, loss_target, m_mix_norm_g, m_mem_norm_g, m_w_in, m_gate_b, m_conv_w, m_conv_b, m_conv_ln_g, m_conv_ln_b, m_w_conv_out, m_rel_bias, m_w_attn_out, m_w_mem_kv, m_w_mem_out, m_w_o, m_ffn_norm_g, m_w_up, m_ffn_conv_w, m_ffn_conv_b, m_w_down, m_final_norm_g, v_mix_norm_g, v_mem_norm_g, v_w_in, v_gate_b, v_conv_w, v_conv_b, v_conv_ln_g, v_conv_ln_b, v_w_conv_out, v_rel_bias, v_w_attn_out, v_w_mem_kv, v_w_mem_out, v_w_o, v_ffn_norm_g, v_w_up, v_ffn_conv_w, v_ffn_conv_b, v_w_down, v_final_norm_g):
    env = locals()
    w = {n: env[n] for n in WEIGHT_NAMES}
    mom = {n: env['m_' + n] for n in WEIGHT_NAMES}
    var = {n: env['v_' + n] for n in WEIGHT_NAMES}
    T, D = x.shape[-2:]
    L = w_in.shape[0]
    DC = conv_b.shape[-1]
    DA = N_DEV * w_attn_out.shape[-1] // 2
    dims = {'DA': DA, 'qcol': 2 * DC, 'mcol': 2 * DC + 3 * DA, 'gcol': 2 * DC + 3 * DA + D // 2}
    x2, mem2, target = x.reshape(T, D), mem.reshape(-1, D), loss_target.reshape(T, D)
    me = 4 * lax.axis_index("x") + 2 * lax.axis_index("y") + lax.axis_index("c")

    p = dict(w)

    def start_gather(l, names, first, tag):
        blocks = [w[n][l].astype(BF16) for n in names]
        first, blocks = lax.optimization_barrier((first, blocks))
        tag = "gather_%d%s" % (l, tag)
        started = push_start(blocks, [_own_slot(b, me) for b in blocks], kind="gather", name=tag + "_start1")
        return {'names': names, 'started': started, 'trip': 1, 'tag': tag}

    taps, ffn_taps = all_gather([conv_w, ffn_conv_w], name="gather_taps")
    p['conv_w_full'] = jnp.moveaxis(taps, 0, 2).reshape(L, conv_w.shape[1], -1)
    p['ffn_conv_w_full'] = jnp.moveaxis(ffn_taps, 0, 2).reshape(L, ffn_conv_w.shape[1], -1)
    p['rel_pad'] = jnp.pad(rel_bias, ((0, 0), (0, 0), (0, REL_PAD - rel_bias.shape[-1])))
    pending, first = [], taps
    for names, tag in zip(GATHER_GROUPS, "abc"):
        pending.append(start_gather(0, names, first, tag))
        first = pending[-1]['started'][4]
    p['gathered'] = [_LayerWeights(pending)] + [None] * (L - 1)

    h = x2
    saved = []
    for l in range(L):
        def prefetch(first, l=l):
            if l + 1 == L:
                return None
            nxt = start_gather(l + 1, [n for n, _ in BIG], first, "")
            p['gathered'][l + 1] = _LayerWeights([nxt])
            return nxt['started'][4]

        h, sv = _forward_layer(h, mem2, p, l, dims, prefetch)
        saved.append(sv)
    loss_part, dh, dhb, d_final = loss_head(h, final_norm_g.reshape(1, D), target, name="loss_head")
    loss = lax.psum(loss_part[0, 0], ("x", "y", "c"))

    small, landed, inflight = [None] * L, [{} for _ in range(L)], []

    def scatter_start(l, names, g):
        tag = "abc"[SCATTER_GROUPS.index(names)]
        grads = [_as_shards(g[n], BY_ROWS[n]) for n in names]
        zones = [_own_slot(lax.dynamic_index_in_dim(a, me, 0, keepdims=False), me) for a in grads]
        started = push_start(grads, zones, kind="scatter", name="scatter_start_%d%s" % (l, tag))
        inflight.append((l, names, started, "scatter_wait_%d%s" % (l, tag)))
        return started[4]

    def scatter_finish(after, groups):
        for item in [i for i in inflight if (i[0], i[1]) in groups]:
            l, names, started, wait_name = item
            landed[l].update(zip(names, push_wait(*started[:4], after, kind="scatter", name=wait_name)))
            inflight.remove(item)

    token = None
    for l in reversed(range(L)):
        dh, dhb, g_w_in, small[l] = _backward_layer(dh, dhb, mem2, saved[l], p, l, dims,
                                                    functools.partial(scatter_start, l), after=token)
        scatter_finish(dh, [(l + 1, names) for names in SCATTER_GROUPS])
        if l > 0:
            token = scatter_start(l, SCATTER_GROUPS[2], {'w_in': g_w_in})

    big_names = [n for n, _ in BIG]
    small_names = [n for n in WEIGHT_NAMES if n not in big_names and n != 'final_norm_g']
    stacked = [jnp.stack([small[l][n] for l in range(L)]) for n in small_names] + [d_final]
    got = dict(zip(small_names + ['final_norm_g'], all_gather(stacked, name="gather_small_grads")))
    g_w_in, _ = lax.optimization_barrier((g_w_in, got['final_norm_g']))
    token = scatter_start(0, SCATTER_GROUPS[2], {'w_in': g_w_in})
    scatter_finish(token, [(0, names) for names in SCATTER_GROUPS[:2]])
    for n in ('conv_w', 'ffn_conv_w'):
        cs = w[n].shape[-1]
        got[n] = lax.dynamic_slice_in_dim(got[n], me * cs, cs, axis=3)
    contribs = {n: [got[n].reshape(N_DEV, -1, w[n].shape[-1])] for n in small_names + ['final_norm_g']}

    outs = {}
    for n in [n for n in WEIGHT_NAMES if n != 'w_in'] + ['w_in']:
        if n == 'w_in':
            done = lax.optimization_barrier(tuple(o[0] for o in outs.values()))
            scatter_finish(done[0], [(0, SCATTER_GROUPS[2])])
        if n in big_names:
            contribs[n] = [landed[l][n] for l in range(L)]
        shp = (len(contribs[n]),) + contribs[n][0].shape[1:]
        res = adamw(contribs[n], w[n].reshape(shp), mom[n].reshape(shp), var[n].reshape(shp), name="adamw_" + n)
        outs[n] = [r.reshape(w[n].shape) for r in res]
    return (loss, dh.reshape(x.shape),
            *[outs[n][0] for n in WEIGHT_NAMES], *[outs[n][1] for n in WEIGHT_NAMES],
            *[outs[n][2] for n in WEIGHT_NAMES], *[outs[n][3] for n in WEIGHT_NAMES])
```

```python
import functools

import numpy as np
import jax
import jax.numpy as jnp
from jax import lax
from jax.experimental import pallas as pl
from jax.experimental.pallas import tpu as pltpu

F32 = jnp.float32
BF16 = jnp.bfloat16

CHUNK = 64
LEFT_CHUNKS = 8
MAX_REL = 256
N_MEM_HEADS = 4
ATTN_HEAD_DIM = 64
CONV_WIDTH = 31
FFN_CONV_WIDTH = 3
EPS = 1e-6
NEG_INF = -1e30
ADAM_LR = 0.001
ADAM_B1 = 0.9
ADAM_B2 = 0.999
ADAM_EPS = 1e-08
ADAM_WD = 0.01
ADAM_STEP = 10
N_DEV = 8

VMEM_LIMIT_BYTES = 56 * 1024 * 1024
LANES = 128
SUBLANES = 8


def _cparams(sem=None):
    return pltpu.CompilerParams(dimension_semantics=sem, vmem_limit_bytes=VMEM_LIMIT_BYTES)


def _tile(n, want):
    if n <= want:
        return n
    t = (want // LANES) * LANES
    while t >= LANES:
        if n % t == 0:
            return t if 4 * t >= want or n > 2 * want else n
        t -= LANES
    return n


MM_TM, MM_TN, MM_TK = 1024, 1536, 2048
W_DOWN_TK, W_DOWN_TM = 2816, 1408


def _matmul(name, a, b, *, dims, grid, nk, a_spec, b_spec, out_spec, out_shape, acc_shape, residual=None,
            after=None, slots_per_step=1):
    n_in = 2 + (residual is not None) + (after is not None)

    def body(*refs):
        a_ref, b_ref = refs[:2]
        o_ref = refs[n_in]

        def finish(r):
            if residual is not None:
                r = r + refs[2][...]
            o_ref[...] = r.astype(o_ref.dtype)

        if slots_per_step == 1:
            part = lax.dot_general(a_ref[...], b_ref[...], (dims, ((), ())), preferred_element_type=F32)
        else:
            tk = b_ref.shape[-1]
            part = functools.reduce(jnp.add, [
                lax.dot_general(a_ref[:, pl.ds(g * tk, tk)], b_ref[g], (dims, ((), ())), preferred_element_type=F32)
                for g in range(slots_per_step)])
        if nk == 1:
            finish(part)
            return
        acc = refs[-1]
        k = pl.program_id(2)

        @pl.when(k == 0)
        def _():
            acc[...] = part

        @pl.when((k > 0) & (k < nk - 1))
        def _():
            acc[...] += part

        @pl.when(k == nk - 1)
        def _():
            finish(acc[...] + part)

    in_specs, args = [a_spec, b_spec], [a, b]
    if residual is not None:
        in_specs.append(out_spec)
        args.append(residual)
    if after is not None:
        in_specs.append(pl.BlockSpec(memory_space=pl.ANY))
        args.append(after)
    return pl.pallas_call(
        body, name=name, grid=grid, in_specs=in_specs, out_specs=out_spec, out_shape=out_shape,
        scratch_shapes=[pltpu.VMEM(acc_shape, F32)] if nk > 1 else [],
        compiler_params=_cparams(("parallel", "parallel", "arbitrary")),
    )(*args)


def _lead_spec(arr, lead, block, index):
    if lead is None:
        assert arr.ndim == 2
        return pl.BlockSpec(block, index)
    assert arr.ndim == 3
    if lead == "planes":
        per_plane = arr.shape[2] // block[1]
        assert arr.shape[2] % block[1] == 0

        def planes_index(i, j, k):
            r, c = index(i, j, k)
            return c // per_plane, r, c % per_plane

        return pl.BlockSpec((None, *block), planes_index)
    return pl.BlockSpec((None, *block), lambda i, j, k: (lead, *index(i, j, k)))


def _matrix_shape(arr, lead):
    return (arr.shape[1], arr.shape[0] * arr.shape[2]) if lead == "planes" else arr.shape[-2:]


def mm_nn(a, w, *, out_dtype, residual=None, after=None, tm=MM_TM, tn=MM_TN, tk=MM_TK, name):
    M, K = a.shape
    S, K2, Ns = w.shape
    assert K == K2
    tm, tn, tk = _tile(M, tm), _tile(Ns, tn), _tile(K, tk)
    nb = Ns // tn
    return _matmul(
        name, a, w, dims=((1,), (0,)), grid=(M // tm, S * nb, K // tk), nk=K // tk,
        a_spec=pl.BlockSpec((tm, tk), lambda i, j, k: (i, k)),
        b_spec=pl.BlockSpec((None, tk, tn), lambda i, j, k: (j // nb, k, j % nb)),
        out_spec=pl.BlockSpec((tm, tn), lambda i, j, k: (i, j)),
        out_shape=jax.ShapeDtypeStruct((M, S * Ns), out_dtype), acc_shape=(tm, tn), residual=residual,
        after=after)


def mm_nt(a, w, *, out_dtype, lead=None, after=None, tm=MM_TM, tn=MM_TN, tk=MM_TK, name):
    M, N = _matrix_shape(a, lead)
    S, K, Ns = w.shape
    assert N == S * Ns
    tm, tn, tk = _tile(M, tm), _tile(K, tn), _tile(Ns, tk)
    nb = Ns // tk
    if nb == 1 and S % 2 == 0 and S > 2:
        return _matmul(
            name, a, w, dims=((1,), (1,)), grid=(M // tm, K // tn, S // 2), nk=S // 2,
            a_spec=_lead_spec(a, lead, (tm, 2 * tk), lambda i, j, k: (i, k)),
            b_spec=pl.BlockSpec((2, tn, tk), lambda i, j, k: (k, j, 0)),
            out_spec=pl.BlockSpec((tm, tn), lambda i, j, k: (i, j)),
            out_shape=jax.ShapeDtypeStruct((M, K), out_dtype), acc_shape=(tm, tn), after=after, slots_per_step=2)
    return _matmul(
        name, a, w, dims=((1,), (1,)), grid=(M // tm, K // tn, S * nb), nk=S * nb,
        a_spec=_lead_spec(a, lead, (tm, tk), lambda i, j, k: (i, k)),
        b_spec=pl.BlockSpec((None, tn, tk), lambda i, j, k: (k // nb, j, k % nb)),
        out_spec=pl.BlockSpec((tm, tn), lambda i, j, k: (i, j)),
        out_shape=jax.ShapeDtypeStruct((M, K), out_dtype), acc_shape=(tm, tn), after=after)


def mm_tn(a, b, *, slots, out_dtype, lead=None, tm=MM_TM, tn=MM_TN, tk=MM_TK, name):
    T, K = a.shape
    T2, N = _matrix_shape(b, lead)
    assert T == T2 and N % slots == 0
    Ns = N // slots
    tm, tn, tk = _tile(K, tm), _tile(Ns, tn), _tile(T, tk)
    nb = Ns // tn
    return _matmul(
        name, a, b, dims=((0,), (0,)), grid=(K // tm, slots * nb, T // tk), nk=T // tk,
        a_spec=pl.BlockSpec((tk, tm), lambda i, j, k: (k, i)),
        b_spec=_lead_spec(b, lead, (tk, tn), lambda i, j, k: (k, j)),
        out_spec=pl.BlockSpec((None, tm, tn), lambda i, j, k: (j // nb, i, j % nb)),
        out_shape=jax.ShapeDtypeStruct((slots, K, Ns), out_dtype), acc_shape=(tm, tn))


def rms_fwd(h, g, *, tr=256, name):
    T, D = h.shape
    tr = min(tr, T)

    def body(h_ref, g_ref, o_ref):
        x = h_ref[...]
        r = lax.rsqrt(jnp.mean(x * x, axis=-1, keepdims=True) + EPS)
        o_ref[...] = (x * r * g_ref[...]).astype(o_ref.dtype)

    return pl.pallas_call(
        body, name=name, grid=(T // tr,),
        in_specs=[pl.BlockSpec((tr, D), lambda i: (i, 0)), pl.BlockSpec((1, D), lambda i: (0, 0))],
        out_specs=pl.BlockSpec((tr, D), lambda i: (i, 0)),
        out_shape=jax.ShapeDtypeStruct((T, D), BF16),
        compiler_params=_cparams(("parallel",)),
    )(h, g)


def rms_bwd(h, g, dxn, dres, *, tr=256, name):
    T, D = h.shape
    tr = min(tr, T)

    def body(h_ref, g_ref, d_ref, r_ref, dh_ref, dhb_ref, dg_ref):
        i = pl.program_id(0)
        x = h_ref[...]
        r = lax.rsqrt(jnp.mean(x * x, axis=-1, keepdims=True) + EPS)
        xh = x * r
        d = d_ref[...].astype(F32)

        @pl.when(i == 0)
        def _():
            dg_ref[...] = jnp.zeros_like(dg_ref)

        dg_ref[...] += jnp.sum(d * xh, axis=0, keepdims=True)
        dxh = d * g_ref[...]
        dh = r * (dxh - xh * jnp.mean(dxh * xh, axis=-1, keepdims=True)) + r_ref[...]
        dh_ref[...] = dh
        dhb_ref[...] = dh.astype(BF16)

    row = pl.BlockSpec((tr, D), lambda i: (i, 0))
    vec = pl.BlockSpec((1, D), lambda i: (0, 0))
    return pl.pallas_call(
        body, name=name, grid=(T // tr,),
        in_specs=[row, vec, row, row],
        out_specs=[row, row, vec],
        out_shape=[jax.ShapeDtypeStruct((T, D), F32), jax.ShapeDtypeStruct((T, D), BF16),
                   jax.ShapeDtypeStruct((1, D), F32)],
        compiler_params=_cparams(("arbitrary",)),
    )(h, g, dxn, dres)


def loss_head(h, g, target, *, tr=256, name):
    T, D = h.shape
    tr = min(tr, T)

    def body(h_ref, g_ref, t_ref, loss_ref, dh_ref, dhb_ref, dg_ref):
        i = pl.program_id(0)
        x = h_ref[...]
        r = lax.rsqrt(jnp.mean(x * x, axis=-1, keepdims=True) + EPS)
        xh = x * r
        gg = g_ref[...]
        err = xh * gg - t_ref[...]

        @pl.when(i == 0)
        def _():
            dg_ref[...] = jnp.zeros_like(dg_ref)
            loss_ref[...] = jnp.zeros_like(loss_ref)

        loss_ref[...] += 0.5 * jnp.sum(jnp.mean(err * err, axis=-1, keepdims=True))
        dy = err * (1.0 / D)
        dg_ref[...] += jnp.sum(dy * xh, axis=0, keepdims=True)
        dxh = dy * gg
        dh = r * (dxh - xh * jnp.mean(dxh * xh, axis=-1, keepdims=True))
        dh_ref[...] = dh
        dhb_ref[...] = dh.astype(BF16)

    row = pl.BlockSpec((tr, D), lambda i: (i, 0))
    vec = pl.BlockSpec((1, D), lambda i: (0, 0))
    return pl.pallas_call(
        body, name=name, grid=(T // tr,),
        in_specs=[row, vec, row],
        out_specs=[pl.BlockSpec((8, LANES), lambda i: (0, 0)), row, row, vec],
        out_shape=[jax.ShapeDtypeStruct((8, LANES), F32), jax.ShapeDtypeStruct((T, D), F32),
                   jax.ShapeDtypeStruct((T, D), BF16), jax.ShapeDtypeStruct((1, D), F32)],
        compiler_params=_cparams(("arbitrary",)),
    )(h, g, target)


CONV_HALO = 32


def _glu_ext(prev_ref, main_ref, hs_ref, i, dc, tt):
    up = prev_ref[...].astype(F32)
    hp = up[:, :dc] * jax.nn.sigmoid(up[:, dc:])
    hs_ref[pl.ds(0, CONV_HALO), :] = jnp.where(i > 0, hp, 0.0)
    um = main_ref[...].astype(F32)
    sig = jax.nn.sigmoid(um[:, dc:])
    hs_ref[pl.ds(CONV_HALO, tt), :] = um[:, :dc] * sig


CONV_ROWS = 32


def _phase_copies(src_ref, ph_ref, rows):
    for b in range(1, SUBLANES):
        ph_ref[b - 1, pl.ds(0, rows), :] = src_ref[pl.ds(b, rows), :]


def _tap_rows(src_ref, ph_ref, offset, r0, n):
    a, b = divmod(offset, SUBLANES)
    ref = src_ref if b == 0 else ph_ref.at[b - 1]
    return ref[pl.ds(r0 + a * SUBLANES, n), :]


def _fold_rows(x):
    out = x[0:SUBLANES]
    for r in range(SUBLANES, x.shape[0], SUBLANES):
        out = out + x[r:r + SUBLANES]
    return out


def conv_fwd(proj, conv_w, conv_b, ln_g, ln_b, *, tt=256, name):
    T = proj.shape[0]
    W, DC = conv_w.shape
    tt = min(tt, T)
    hb = tt // CONV_HALO
    ph_rows = tt + CONV_HALO - SUBLANES

    def body(prev_ref, main_ref, w_ref, b_ref, g_ref, bb_ref, s_ref, c_ref, hs_ref, ph_ref):
        i = pl.program_id(0)
        _glu_ext(prev_ref, main_ref, hs_ref, i, DC, tt)
        _phase_copies(hs_ref, ph_ref, ph_rows)

        def chunk(cc, carry):
            r0 = pl.multiple_of(cc * CONV_ROWS, CONV_ROWS)
            c = jnp.zeros((CONV_ROWS, DC), F32) + b_ref[...]
            for k in range(W):
                c = c + w_ref[pl.ds(k, 1), :] * _tap_rows(hs_ref, ph_ref, CONV_HALO - (W - 1) + k, r0, CONV_ROWS)
            rows = pl.ds(r0, CONV_ROWS)
            c_ref[rows, :] = c
            mu = jnp.mean(c, axis=-1, keepdims=True)
            xc = c - mu
            var = jnp.mean(xc * xc, axis=-1, keepdims=True)
            y = xc * lax.rsqrt(var + EPS) * g_ref[...] + bb_ref[...]
            s_ref[rows, :] = (y * jax.nn.sigmoid(y)).astype(s_ref.dtype)
            return carry

        lax.fori_loop(0, tt // CONV_ROWS, chunk, 0)

    vec = pl.BlockSpec((1, DC), lambda i: (0, 0))
    return pl.pallas_call(
        body, name=name, grid=(T // tt,),
        in_specs=[pl.BlockSpec((CONV_HALO, 2 * DC), lambda i: (jnp.maximum(i * hb - 1, 0), 0)),
                  pl.BlockSpec((tt, 2 * DC), lambda i: (i, 0)),
                  pl.BlockSpec((W, DC), lambda i: (0, 0)), vec, vec, vec],
        out_specs=[pl.BlockSpec((tt, DC), lambda i: (i, 0)), pl.BlockSpec((tt, DC), lambda i: (i, 0))],
        out_shape=[jax.ShapeDtypeStruct((T, DC), BF16), jax.ShapeDtypeStruct((T, DC), F32)],
        scratch_shapes=[pltpu.VMEM((tt + CONV_HALO, DC), F32), pltpu.VMEM((SUBLANES - 1, ph_rows, DC), F32)],
        compiler_params=_cparams(("parallel",)),
    )(proj, proj, conv_w, conv_b, ln_g, ln_b)


def conv_bwd_ln(ds, c, ln_g, ln_b, *, tt=256, name):
    T, DC = c.shape
    tt = min(tt, T)

    def body(ds_ref, c_ref, g_ref, bb_ref, dc_ref, dg_ref, db_ref):
        i = pl.program_id(0)
        c = c_ref[...]
        mu = jnp.mean(c, axis=-1, keepdims=True)
        xc = c - mu
        var = jnp.mean(xc * xc, axis=-1, keepdims=True)
        rstd = lax.rsqrt(var + EPS)
        xh = xc * rstd
        y = xh * g_ref[...] + bb_ref[...]
        sg = jax.nn.sigmoid(y)
        dy = ds_ref[...].astype(F32) * (sg * (1.0 + y * (1.0 - sg)))

        @pl.when(i == 0)
        def _():
            dg_ref[...] = jnp.zeros_like(dg_ref)
            db_ref[...] = jnp.zeros_like(db_ref)

        db_ref[...] += jnp.sum(dy, axis=0, keepdims=True)
        dg_ref[...] += jnp.sum(dy * xh, axis=0, keepdims=True)
        dxh = dy * g_ref[...]
        dc_ref[...] = rstd * (dxh - jnp.mean(dxh, axis=-1, keepdims=True)
                              - xh * jnp.mean(dxh * xh, axis=-1, keepdims=True))

    row = pl.BlockSpec((tt, DC), lambda i: (i, 0))
    vec = pl.BlockSpec((1, DC), lambda i: (0, 0))
    return pl.pallas_call(
        body, name=name, grid=(T // tt,),
        in_specs=[row, row, vec, vec], out_specs=[row, vec, vec],
        out_shape=[jax.ShapeDtypeStruct((T, DC), F32), jax.ShapeDtypeStruct((1, DC), F32),
                   jax.ShapeDtypeStruct((1, DC), F32)],
        compiler_params=_cparams(("arbitrary",)),
    )(ds, c, ln_g, ln_b)


def conv_bwd_taps(dc, proj, conv_w, dproj, *, tt=256, name):
    T, DC = dc.shape
    W = conv_w.shape[0]
    tt = min(tt, T)
    hb = tt // CONV_HALO
    n_t = T // tt
    last_halo = T // CONV_HALO - 1
    ph_rows = tt + CONV_HALO - SUBLANES

    def body(dc_ref, dcn_ref, prev_ref, main_ref, w_ref, _, dp_ref, dw_ref, db_ref, hs_ref, ds_ref, hph_ref,
             dph_ref, acc_ref):
        i = pl.program_id(0)
        _glu_ext(prev_ref, main_ref, hs_ref, i, DC, tt)
        ds_ref[pl.ds(0, tt), :] = dc_ref[...]
        ds_ref[pl.ds(tt, CONV_HALO), :] = jnp.where(i < n_t - 1, dcn_ref[...], 0.0)
        _phase_copies(hs_ref, hph_ref, ph_rows)
        _phase_copies(ds_ref, dph_ref, ph_rows)
        acc_ref[...] = jnp.zeros_like(acc_ref)

        def chunk(cc, carry):
            r0 = pl.multiple_of(cc * CONV_ROWS, CONV_ROWS)
            rows = pl.ds(r0, CONV_ROWS)
            d = ds_ref[rows, :]
            dh = jnp.zeros((CONV_ROWS, DC), F32)
            for k in range(W):
                dh = dh + w_ref[pl.ds(k, 1), :] * _tap_rows(ds_ref, dph_ref, W - 1 - k, r0, CONV_ROWS)
                acc_ref[k] += _fold_rows(d * _tap_rows(hs_ref, hph_ref, CONV_HALO - (W - 1) + k, r0, CONV_ROWS))
            acc_ref[W] += _fold_rows(d)
            um = main_ref[rows, :].astype(F32)
            a, sig = um[:, :DC], jax.nn.sigmoid(um[:, DC:])
            dp_ref[rows, :] = jnp.concatenate([dh * sig, dh * a * sig * (1.0 - sig)], axis=1).astype(dp_ref.dtype)
            return carry

        lax.fori_loop(0, tt // CONV_ROWS, chunk, 0)

        @pl.when(i == 0)
        def _():
            dw_ref[...] = jnp.zeros_like(dw_ref)
            db_ref[...] = jnp.zeros_like(db_ref)

        for k in range(W):
            dw_ref[pl.ds(k, 1), :] += jnp.sum(acc_ref[k], axis=0, keepdims=True)
        db_ref[...] += jnp.sum(acc_ref[W], axis=0, keepdims=True)

    return pl.pallas_call(
        body, name=name, grid=(n_t,),
        in_specs=[pl.BlockSpec((tt, DC), lambda i: (i, 0)),
                  pl.BlockSpec((CONV_HALO, DC), lambda i: (jnp.minimum((i + 1) * hb, last_halo), 0)),
                  pl.BlockSpec((CONV_HALO, 2 * DC), lambda i: (jnp.maximum(i * hb - 1, 0), 0)),
                  pl.BlockSpec((tt, 2 * DC), lambda i: (i, 0)),
                  pl.BlockSpec((W, DC), lambda i: (0, 0)),
                  pl.BlockSpec(memory_space=pl.ANY)],
        out_specs=[pl.BlockSpec((tt, 2 * DC), lambda i: (i, 0)),
                   pl.BlockSpec((W, DC), lambda i: (0, 0)), pl.BlockSpec((1, DC), lambda i: (0, 0))],
        out_shape=[jax.ShapeDtypeStruct(dproj.shape, dproj.dtype), jax.ShapeDtypeStruct((W, DC), F32),
                   jax.ShapeDtypeStruct((1, DC), F32)],
        scratch_shapes=[pltpu.VMEM((tt + CONV_HALO, DC), F32), pltpu.VMEM((tt + CONV_HALO, DC), F32),
                        pltpu.VMEM((SUBLANES - 1, ph_rows, DC), F32), pltpu.VMEM((SUBLANES - 1, ph_rows, DC), F32),
                        pltpu.VMEM((W + 1, SUBLANES, DC), F32)],
        input_output_aliases={5: 0},
        compiler_params=_cparams(("arbitrary",)),
    )(dc, dc, proj, proj, conv_w, dproj)


ATT_TQ = 128
ATT_NB = 1 + (LEFT_CHUNKS * CHUNK) // ATT_TQ
ATT_WIN = ATT_NB * ATT_TQ
ATT_PERIOD = 1024
REL_PAD = 640


def _rel_onehot():
    m = lax.broadcasted_iota(jnp.int32, (REL_PAD, ATT_PERIOD), 1)
    r = lax.broadcasted_iota(jnp.int32, (REL_PAD, ATT_PERIOD), 0)
    qk = jnp.where(m < ATT_PERIOD - ATT_TQ, -m, ATT_PERIOD - m)
    idx = jnp.clip(LEFT_CHUNKS * CHUNK + qk, -MAX_REL, MAX_REL) + MAX_REL
    return (idx == r).astype(F32)


def _band_valid():
    q = lax.broadcasted_iota(jnp.int32, (ATT_TQ, ATT_WIN), 0)
    k = lax.broadcasted_iota(jnp.int32, (ATT_TQ, ATT_WIN), 1)
    j = k // CHUNK - q // CHUNK
    return (j >= 0) & (j <= LEFT_CHUNKS)


def attn_bias(rel_pad, *, name):
    H = rel_pad.shape[0]

    def body(rb_ref, o_ref):
        g = jnp.dot(rb_ref[...], _rel_onehot(), preferred_element_type=F32, precision=lax.Precision.HIGHEST)
        valid = _band_valid()
        for h in range(H):
            row = jnp.broadcast_to(g[h:h + 1, :], (ATT_TQ, ATT_PERIOD))
            t = pltpu.roll(row, 0, 1, stride=1, stride_axis=0)
            o_ref[h] = jnp.where(valid, t[:, :ATT_WIN], NEG_INF)

    return pl.pallas_call(
        body, name=name,
        in_specs=[pl.BlockSpec(memory_space=pltpu.VMEM)], out_specs=pl.BlockSpec(memory_space=pltpu.VMEM),
        out_shape=jax.ShapeDtypeStruct((H, ATT_TQ, ATT_WIN), F32),
        compiler_params=_cparams(),
    )(rel_pad)


def attn_bias_bwd(dbias, *, name):
    H = dbias.shape[0]
    band = (LEFT_CHUNKS + 1) * CHUNK

    def body(d_ref, o_ref, acc_ref):
        r = lax.broadcasted_iota(jnp.int32, (CHUNK, CHUNK), 0)
        c = lax.broadcasted_iota(jnp.int32, (CHUNK, CHUNK), 1)
        rev = (r + c == CHUNK - 1).astype(F32)
        pad = jnp.zeros((CHUNK, ATT_PERIOD - ATT_WIN), F32)
        for h in range(H):
            tab = d_ref[h, pl.ds(0, CHUNK), :]
            for cq in range(1, ATT_TQ // CHUNK):
                tab = tab + pltpu.roll(d_ref[h, pl.ds(cq * CHUNK, CHUNK), :], ATT_WIN - cq * CHUNK, 1)
            t = jnp.dot(rev, tab, preferred_element_type=F32, precision=lax.Precision.HIGHEST)
            u = pltpu.roll(jnp.concatenate([t, pad], axis=1), 0, 1, stride=1, stride_axis=0)
            acc_ref[pl.ds(h, 1), :] = jnp.sum(u, axis=0, keepdims=True)
        j = lax.broadcasted_iota(jnp.int32, (REL_PAD, ATT_PERIOD), 1)
        rr = lax.broadcasted_iota(jnp.int32, (REL_PAD, ATT_PERIOD), 0)
        idx = jnp.clip(LEFT_CHUNKS * CHUNK + CHUNK - 1 - j, -MAX_REL, MAX_REL) + MAX_REL
        onehot = ((idx == rr) & (j < band + CHUNK - 1)).astype(F32)
        o_ref[...] = lax.dot_general(acc_ref[...], onehot, (((1,), (1,)), ((), ())),
                                     preferred_element_type=F32, precision=lax.Precision.HIGHEST)

    return pl.pallas_call(
        body, name=name,
        in_specs=[pl.BlockSpec(memory_space=pltpu.VMEM)], out_specs=pl.BlockSpec(memory_space=pltpu.VMEM),
        out_shape=jax.ShapeDtypeStruct((H, REL_PAD), F32),
        scratch_shapes=[pltpu.VMEM((H, ATT_PERIOD), F32)],
        compiler_params=_cparams(),
    )(dbias)


def _attn_specs(T, qcol, dattn):
    dn = dattn // LANES

    def kv(col0, back):
        return pl.BlockSpec((ATT_TQ, LANES), lambda hp, i: (jnp.maximum(i - back, 0), col0 // LANES + hp))

    q = pl.BlockSpec((ATT_TQ, LANES), lambda hp, i: (i, qcol // LANES + hp))
    ks = [kv(qcol + dattn, ATT_NB - 1 - b) for b in range(ATT_NB)]
    vs = [kv(qcol + 2 * dattn, ATT_NB - 1 - b) for b in range(ATT_NB)]
    return q, ks, vs


def _attn_scores(q, kw, bias, i):
    s = lax.dot_general(q, kw, (((1,), (1,)), ((), ())), preferred_element_type=F32)
    s = s * (ATTN_HEAD_DIM ** -0.5) + bias
    col = lax.broadcasted_iota(jnp.int32, s.shape, 1)
    s = jnp.where(col // ATT_TQ + i >= ATT_NB - 1, s, NEG_INF)
    s = s - jnp.max(s, axis=-1, keepdims=True)
    p = jnp.exp(s)
    return p * (1.0 / jnp.sum(p, axis=-1, keepdims=True))


def attn_fwd(proj, bias, *, qcol, dattn, name):
    T = proj.shape[0]
    HP = dattn // LANES
    q_spec, k_specs, v_specs = _attn_specs(T, qcol, dattn)

    def body(*refs):
        q_ref = refs[0]
        k_refs = refs[1:1 + ATT_NB]
        v_refs = refs[1 + ATT_NB:1 + 2 * ATT_NB]
        b_ref, o_ref = refs[1 + 2 * ATT_NB:]
        i = pl.program_id(1)
        lane = lax.broadcasted_iota(jnp.int32, (ATT_TQ, LANES), 1)
        kw = jnp.concatenate([r[...] for r in k_refs], axis=0)
        vw = jnp.concatenate([r[...] for r in v_refs], axis=0)
        q = q_ref[...]
        out = jnp.zeros((ATT_TQ, LANES), F32)
        for hh in range(2):
            mine = (lane // ATTN_HEAD_DIM) == hh
            p = _attn_scores(jnp.where(mine, q, jnp.zeros_like(q)), kw, b_ref[hh], i)
            o = jnp.dot(p.astype(BF16), vw, preferred_element_type=F32)
            out = jnp.where(mine, o, out)
        o_ref[...] = out.astype(o_ref.dtype)

    return pl.pallas_call(
        body, name=name, grid=(HP, T // ATT_TQ),
        in_specs=[q_spec, *k_specs, *v_specs,
                  pl.BlockSpec((2, ATT_TQ, ATT_WIN), lambda hp, i: (hp, 0, 0))],
        out_specs=pl.BlockSpec((ATT_TQ, LANES), lambda hp, i: (i, hp)),
        out_shape=jax.ShapeDtypeStruct((T, dattn), BF16),
        compiler_params=_cparams(("parallel", "parallel")),
    )(*([proj] * (1 + 2 * ATT_NB)), bias)


def attn_bwd(proj, bias, dout, dproj, *, qcol, dattn, name):
    T = proj.shape[0]
    HP = dattn // LANES
    nq = T // ATT_TQ
    q_spec, k_specs, v_specs = _attn_specs(T, qcol, dattn)

    def body(*refs):
        q_ref = refs[0]
        k_refs = refs[1:1 + ATT_NB]
        v_refs = refs[1 + ATT_NB:1 + 2 * ATT_NB]
        b_ref, do_ref, _, dq_ref, dk_ref, dv_ref, db_ref = refs[1 + 2 * ATT_NB:]
        i = pl.program_id(1)
        lane = lax.broadcasted_iota(jnp.int32, (ATT_TQ, LANES), 1)
        kw = jnp.concatenate([r[...] for r in k_refs], axis=0)
        vw = jnp.concatenate([r[...] for r in v_refs], axis=0)
        q = q_ref[...]
        do = do_ref[...]
        dq = jnp.zeros((ATT_TQ, LANES), F32)
        dk = jnp.zeros((ATT_WIN, LANES), F32)
        dv = jnp.zeros((ATT_WIN, LANES), F32)

        @pl.when(i == 0)
        def _():
            db_ref[...] = jnp.zeros_like(db_ref)

        for hh in range(2):
            mine = (lane // ATTN_HEAD_DIM) == hh
            qh = jnp.where(mine, q, jnp.zeros_like(q))
            doh = jnp.where(mine, do, jnp.zeros_like(do))
            p = _attn_scores(qh, kw, b_ref[hh], i)
            dp = lax.dot_general(doh, vw, (((1,), (1,)), ((), ())), preferred_element_type=F32)
            dv = dv + lax.dot_general(p.astype(BF16), doh, (((0,), (0,)), ((), ())), preferred_element_type=F32)
            ds = p * (dp - jnp.sum(dp * p, axis=-1, keepdims=True))
            db_ref[hh] += ds
            dsb = (ds * (ATTN_HEAD_DIM ** -0.5)).astype(BF16)
            dq = jnp.where(mine, jnp.dot(dsb, kw, preferred_element_type=F32), dq)
            dk = dk + lax.dot_general(dsb, qh, (((0,), (0,)), ((), ())), preferred_element_type=F32)
        dq_ref[...] = dq.astype(dq_ref.dtype)
        dk_ref[...] = dk
        dv_ref[...] = dv

    win = pl.BlockSpec((None, ATT_WIN, LANES), lambda hp, i: (i, 0, hp))
    return pl.pallas_call(
        body, name=name, grid=(HP, nq),
        in_specs=[q_spec, *k_specs, *v_specs,
                  pl.BlockSpec((2, ATT_TQ, ATT_WIN), lambda hp, i: (hp, 0, 0)),
                  pl.BlockSpec((ATT_TQ, LANES), lambda hp, i: (i, hp)),
                  pl.BlockSpec(memory_space=pl.ANY)],
        out_specs=[pl.BlockSpec((ATT_TQ, LANES), lambda hp, i: (i, qcol // LANES + hp)), win, win,
                   pl.BlockSpec((2, ATT_TQ, ATT_WIN), lambda hp, i: (hp, 0, 0))],
        out_shape=[jax.ShapeDtypeStruct(dproj.shape, dproj.dtype),
                   jax.ShapeDtypeStruct((nq, ATT_WIN, dattn), F32), jax.ShapeDtypeStruct((nq, ATT_WIN, dattn), F32),
                   jax.ShapeDtypeStruct((2 * HP, ATT_TQ, ATT_WIN), F32)],
        input_output_aliases={3 + 2 * ATT_NB: 0},
        compiler_params=_cparams(("arbitrary", "arbitrary")),
    )(*([proj] * (1 + 2 * ATT_NB)), bias, dout, dproj)


def window_sum(win, dproj, *, col, name):
    nq, _, C = win.shape
    assert col % C == 0

    def body(*refs):
        w_refs = refs[:ATT_NB]
        o_ref = refs[ATT_NB + 1]
        j = pl.program_id(0)
        acc = w_refs[0][...]
        for b in range(1, ATT_NB):
            acc = acc + jnp.where(j + b < nq, w_refs[b][...], 0.0)
        o_ref[...] = acc.astype(o_ref.dtype)

    def part(b):
        return pl.BlockSpec((None, ATT_TQ, C), lambda j: (jnp.minimum(j + b, nq - 1), ATT_NB - 1 - b, 0))

    return pl.pallas_call(
        body, name=name, grid=(nq,),
        in_specs=[part(b) for b in range(ATT_NB)] + [pl.BlockSpec(memory_space=pl.ANY)],
        out_specs=pl.BlockSpec((ATT_TQ, C), lambda j: (j, col // C)),
        out_shape=jax.ShapeDtypeStruct(dproj.shape, dproj.dtype),
        input_output_aliases={ATT_NB: 0},
        compiler_params=_cparams(("arbitrary",)),
    )(*([win] * ATT_NB), dproj)


def _mem_probs(q, km, scale):
    s = lax.dot_general(q, km, (((1,), (1,)), ((), ())), preferred_element_type=F32) * scale
    s = s - jnp.max(s, axis=-1, keepdims=True)
    p = jnp.exp(s)
    return p * (1.0 / jnp.sum(p, axis=-1, keepdims=True))


def memattn_fwd(proj, kv, *, qcol, tq=512, name):
    T = proj.shape[0]
    M, dm2 = kv.shape
    DM = dm2 // 2
    hd = DM // N_MEM_HEADS
    tq = min(tq, T)

    def body(q_ref, kv_ref, o_ref):
        for h in range(N_MEM_HEADS):
            sl = pl.ds(h * hd, hd)
            p = _mem_probs(q_ref[:, sl], kv_ref[:, sl], hd ** -0.5)
            o = jnp.dot(p.astype(BF16), kv_ref[:, pl.ds(DM + h * hd, hd)], preferred_element_type=F32)
            o_ref[:, sl] = o.astype(o_ref.dtype)

    return pl.pallas_call(
        body, name=name, grid=(T // tq,),
        in_specs=[pl.BlockSpec((tq, DM), lambda i: (i, qcol // DM)),
                  pl.BlockSpec((M, 2 * DM), lambda i: (0, 0))],
        out_specs=pl.BlockSpec((tq, DM), lambda i: (i, 0)),
        out_shape=jax.ShapeDtypeStruct((T, DM), BF16),
        compiler_params=_cparams(("parallel",)),
    )(proj, kv)


def memattn_bwd(proj, kv, dout, dproj, *, qcol, tq=512, name):
    T = proj.shape[0]
    M, dm2 = kv.shape
    DM = dm2 // 2
    hd = DM // N_MEM_HEADS
    tq = min(tq, T)

    def body(q_ref, kv_ref, do_ref, _, dq_ref, dkv_ref):
        i = pl.program_id(0)

        @pl.when(i == 0)
        def _():
            dkv_ref[...] = jnp.zeros_like(dkv_ref)

        for h in range(N_MEM_HEADS):
            sl = pl.ds(h * hd, hd)
            vsl = pl.ds(DM + h * hd, hd)
            q = q_ref[:, sl]
            do = do_ref[:, sl]
            p = _mem_probs(q, kv_ref[:, sl], hd ** -0.5)
            dp = lax.dot_general(do, kv_ref[:, vsl], (((1,), (1,)), ((), ())), preferred_element_type=F32)
            dkv_ref[:, vsl] += lax.dot_general(p.astype(BF16), do, (((0,), (0,)), ((), ())),
                                               preferred_element_type=F32)
            ds = p * (dp - jnp.sum(dp * p, axis=-1, keepdims=True))
            dsb = (ds * (hd ** -0.5)).astype(BF16)
            dq_ref[:, sl] = jnp.dot(dsb, kv_ref[:, sl], preferred_element_type=F32).astype(dq_ref.dtype)
            dkv_ref[:, sl] += lax.dot_general(dsb, q, (((0,), (0,)), ((), ())), preferred_element_type=F32)

    return pl.pallas_call(
        body, name=name, grid=(T // tq,),
        in_specs=[pl.BlockSpec((tq, DM), lambda i: (i, qcol // DM)),
                  pl.BlockSpec((M, 2 * DM), lambda i: (0, 0)),
                  pl.BlockSpec((tq, DM), lambda i: (i, 0)),
                  pl.BlockSpec(memory_space=pl.ANY)],
        out_specs=[pl.BlockSpec((tq, DM), lambda i: (i, qcol // DM)),
                   pl.BlockSpec((M, 2 * DM), lambda i: (0, 0))],
        out_shape=[jax.ShapeDtypeStruct(dproj.shape, dproj.dtype), jax.ShapeDtypeStruct((M, 2 * DM), F32)],
        input_output_aliases={3: 0},
        compiler_params=_cparams(("arbitrary",)),
    )(proj, kv, dout, dproj)


def merge_fwd(proj, gate_b, ys, *, gcol, tr=512, tc=512, name):
    T = proj.shape[0]
    D = ys[0].shape[1]
    tr, tc = min(tr, T), _tile(D, tc)
    nd = D // tc

    def body(g0, g1, g2, b0, b1, b2, y0, y1, y2, o_ref):
        acc = jnp.zeros((tr, tc), F32)
        for g, b, y in ((g0, b0, y0), (g1, b1, y1), (g2, b2, y2)):
            acc = acc + jax.nn.sigmoid(g[...].astype(F32) + b[...]) * y[...].astype(F32)
        o_ref[...] = acc.astype(o_ref.dtype)

    def gate(b):
        return pl.BlockSpec((tr, tc), lambda i, j: (i, gcol // tc + b * nd + j))

    def bias(b):
        return pl.BlockSpec((1, tc), lambda i, j: (0, b * nd + j))

    blk = pl.BlockSpec((tr, tc), lambda i, j: (i, j))
    return pl.pallas_call(
        body, name=name, grid=(T // tr, nd),
        in_specs=[gate(0), gate(1), gate(2), bias(0), bias(1), bias(2), blk, blk, blk],
        out_specs=blk,
        out_shape=jax.ShapeDtypeStruct((T, D), BF16),
        compiler_params=_cparams(("parallel", "parallel")),
    )(proj, proj, proj, gate_b, gate_b, gate_b, *ys)


def merge_bwd(dmerged, proj, gate_b, ys, *, gcol, tr=512, tc=512, name):
    T, D_IN = proj.shape
    D = ys[0].shape[1]
    tr, tc = min(tr, T), _tile(D, tc)
    nd = D // tc

    def body(dm_ref, g_ref, b_ref, y0, y1, y2, dp_ref, dy_ref, db_ref):
        br = pl.program_id(0)
        i = pl.program_id(2)

        y = jnp.where(br == 0, y0[...], jnp.where(br == 1, y1[...], y2[...])).astype(F32)
        dm = dm_ref[...].astype(F32)
        sg = jax.nn.sigmoid(g_ref[...].astype(F32) + b_ref[...])
        dy_ref[...] = (dm * sg).astype(dy_ref.dtype)
        dg = dm * y * sg * (1.0 - sg)
        dp_ref[...] = dg.astype(dp_ref.dtype)

        @pl.when(i == 0)
        def _():
            db_ref[...] = jnp.zeros_like(db_ref)

        db_ref[...] += jnp.sum(dg, axis=0, keepdims=True)

    def ysp(b):
        return pl.BlockSpec((tr, tc), lambda br, j, i: (jnp.where(br == b, i, 0), jnp.where(br == b, j, 0)))

    return pl.pallas_call(
        body, name=name, grid=(3, nd, T // tr),
        in_specs=[pl.BlockSpec((tr, tc), lambda br, j, i: (i, j)),
                  pl.BlockSpec((tr, tc), lambda br, j, i: (i, gcol // tc + br * nd + j)),
                  pl.BlockSpec((1, tc), lambda br, j, i: (0, br * nd + j)),
                  ysp(0), ysp(1), ysp(2)],
        out_specs=[pl.BlockSpec((tr, tc), lambda br, j, i: (i, gcol // tc + br * nd + j)),
                   pl.BlockSpec((None, tr, tc), lambda br, j, i: (br, i, j)),
                   pl.BlockSpec((1, tc), lambda br, j, i: (0, br * nd + j))],
        out_shape=[jax.ShapeDtypeStruct((T, D_IN), BF16), jax.ShapeDtypeStruct((3, T, D), BF16),
                   jax.ShapeDtypeStruct((1, 3 * D), F32)],
        compiler_params=_cparams(("arbitrary", "arbitrary", "arbitrary")),
    )(dmerged, proj, gate_b, *ys)


FFN_CHUNK = 16
FFN_ROWS = 32


def _shift_down(tail, x, s):
    return pltpu.roll(jnp.concatenate([tail, x], axis=0), s, 0)[SUBLANES:]


def _shift_up(x, head, s):
    n = x.shape[0]
    return pltpu.roll(jnp.concatenate([x, head], axis=0), n + SUBLANES - s, 0)[:n]


def _ffn_taps(w_ref, b_ref):
    return [w_ref[pl.ds(k, 1), :] for k in range(FFN_CONV_WIDTH)] + [b_ref[...]]


def _ffn_conv(tail, x, taps):
    w0, w1, w2, b = taps
    x1, x2 = _shift_down(tail, x, 1), _shift_down(tail, x, 2)
    return w0 * x2 + w1 * x1 + w2 * x + b, x1, x2


def _ffn_rows(c):
    return pl.ds(c * FFN_ROWS if isinstance(c, int) else pl.multiple_of(c * FFN_ROWS, FFN_ROWS), FFN_ROWS)


def _ffn_specs(tt, tc, nv, order, with_next):
    hb = tt // FFN_CHUNK

    def mk(shape, f):
        return pl.BlockSpec(shape, (lambda i, j: f(i, j)) if order == "ij" else (lambda j, i: f(i, j)))

    def halo_after(T):
        return lambda i, j, off: (jnp.minimum((i + 1) * hb, T // FFN_CHUNK - 1), off + j)

    out = []
    for off in (0, nv):
        out += [mk((FFN_CHUNK, tc), lambda i, j, off=off: (jnp.maximum(i * hb - 1, 0), off + j)),
                mk((tt, tc), lambda i, j, off=off: (i, off + j))]
        if with_next:
            out.append(mk((FFN_CHUNK, tc), lambda i, j, off=off: halo_after(with_next)(i, j, off)))
    out += [mk((FFN_CONV_WIDTH, tc), lambda i, j, off=off: (0, off + j)) for off in (0, nv)]
    out += [mk((1, tc), lambda i, j, off=off: (0, off + j)) for off in (0, nv)]
    return out


def ffn_act_fwd(up0, w, b, *, tt=1024, tc=256, name):
    T, dff2 = up0.shape
    DFF = dff2 // 2
    tt, tc = min(tt, T), _tile(DFF, tc)
    nv = DFF // tc

    def body(vp, vm, gp, gm, wv, wg, bv, bg, o_ref):
        i = pl.program_id(0)
        taps_v, taps_g = _ffn_taps(wv, bv), _ffn_taps(wg, bg)

        def chunk(c, tails):
            rows = _ffn_rows(c)
            xv, xg = vm[rows, :].astype(F32), gm[rows, :].astype(F32)
            val = _ffn_conv(tails[0], xv, taps_v)[0]
            gt = _ffn_conv(tails[1], xg, taps_g)[0]
            o_ref[rows, :] = (gt * jax.nn.sigmoid(gt) * val).astype(o_ref.dtype)
            return xv[-SUBLANES:], xg[-SUBLANES:]

        before = lambda ref: jnp.where(i > 0, ref[...].astype(F32)[-SUBLANES:], 0.0)
        lax.fori_loop(0, tt // FFN_ROWS, chunk, (before(vp), before(gp)))

    return pl.pallas_call(
        body, name=name, grid=(T // tt, nv),
        in_specs=_ffn_specs(tt, tc, nv, "ij", None),
        out_specs=pl.BlockSpec((tt, tc), lambda i, j: (i, j)),
        out_shape=jax.ShapeDtypeStruct((T, DFF), BF16),
        compiler_params=_cparams(("parallel", "parallel")),
    )(up0, up0, up0, up0, w, w, b, b)


def ffn_bwd(dact, up0, w, b, *, tt=1024, tc=256, name):
    T, dff2 = up0.shape
    DFF = dff2 // 2
    tt, tc = min(tt, T), _tile(DFF, tc)
    nv = DFF // tc
    hb = tt // FFN_CHUNK
    n_t = T // tt
    n = tt // FFN_ROWS

    def body(da, dan, vp, vm, vn, gp, gm, gn, wv, wg, bv, bg, o_ref, dw_ref, db_ref, acc):
        i = pl.program_id(1)
        taps_v, taps_g = _ffn_taps(wv, bv), _ffn_taps(wg, bg)
        f32 = lambda ref, rows=slice(None): ref[rows, :].astype(F32)

        def act_grads(d_act, tail_v, xv, tail_g, xg):
            val, v1, v2 = _ffn_conv(tail_v, xv, taps_v)
            gt, g1, g2 = _ffn_conv(tail_g, xg, taps_g)
            sg = jax.nn.sigmoid(gt)
            return d_act * gt * sg, d_act * val * (sg * (1.0 + gt * (1.0 - sg))), (xv, v1, v2), (xg, g1, g2)

        tail = lambda ref, end: f32(ref, pl.ds(end - FFN_CHUNK, FFN_CHUNK))[-SUBLANES:]
        dnv, dng, _, _ = act_grads(f32(dan), tail(vm, tt), f32(vn), tail(gm, tt), f32(gn))
        heads = (jnp.where(i < n_t - 1, dnv[:SUBLANES], 0.0), jnp.where(i < n_t - 1, dng[:SUBLANES], 0.0))
        acc[...] = jnp.zeros_like(acc)

        def emit(c, tail_v, tail_g, heads):
            rows = _ffn_rows(c)
            dv, dg, xs_v, xs_g = act_grads(f32(da, rows), tail_v, f32(vm, rows), tail_g, f32(gm, rows))
            for half, (d, head, xs, taps) in enumerate(((dv, heads[0], xs_v, taps_v), (dg, heads[1], xs_g, taps_g))):
                w0, w1, w2, _ = taps
                o_ref[half, rows, :] = (w2 * d + w1 * _shift_up(d, head, 1) + w0 * _shift_up(d, head, 2)
                                        ).astype(o_ref.dtype)
                for k in range(FFN_CONV_WIDTH):
                    acc[4 * half + k] += d * xs[FFN_CONV_WIDTH - 1 - k]
                acc[4 * half + 3] += d
            return dv[:SUBLANES], dg[:SUBLANES]

        def chunk(k, heads):
            c = n - 1 - k
            end = pl.multiple_of(c * FFN_ROWS, FFN_ROWS)
            return emit(c, tail(vm, end), tail(gm, end), heads)

        heads = lax.fori_loop(0, n - 1, chunk, heads)
        before = lambda ref: jnp.where(i > 0, f32(ref)[-SUBLANES:], 0.0)
        emit(0, before(vp), before(gp), heads)

        @pl.when(i == 0)
        def _():
            dw_ref[...] = jnp.zeros_like(dw_ref)
            db_ref[...] = jnp.zeros_like(db_ref)

        for half in range(2):
            for k in range(FFN_CONV_WIDTH):
                dw_ref[half, pl.ds(k, 1), :] += jnp.sum(acc[4 * half + k], axis=0, keepdims=True)
            db_ref[half] += jnp.sum(acc[4 * half + 3], axis=0, keepdims=True)

    last_halo = T // FFN_CHUNK - 1
    return pl.pallas_call(
        body, name=name, grid=(nv, n_t),
        in_specs=[pl.BlockSpec((tt, tc), lambda j, i: (i, j)),
                  pl.BlockSpec((FFN_CHUNK, tc), lambda j, i: (jnp.minimum((i + 1) * hb, last_halo), j))]
                 + _ffn_specs(tt, tc, nv, "ji", T),
        out_specs=[pl.BlockSpec((2, tt, tc), lambda j, i: (0, i, j)),
                   pl.BlockSpec((2, FFN_CONV_WIDTH, tc), lambda j, i: (0, 0, j)),
                   pl.BlockSpec((2, 1, tc), lambda j, i: (0, 0, j))],
        out_shape=[jax.ShapeDtypeStruct((2, T, DFF), BF16), jax.ShapeDtypeStruct((2, FFN_CONV_WIDTH, DFF), F32),
                   jax.ShapeDtypeStruct((2, 1, DFF), F32)],
        scratch_shapes=[pltpu.VMEM((8, FFN_ROWS, tc), F32)],
        compiler_params=_cparams(("arbitrary", "arbitrary")),
    )(dact, dact, up0, up0, up0, up0, up0, up0, w, w, b, b)


def adamw(contribs, w, m, v, *, name):
    L, R, C = w.shape
    assert len(contribs) == L and all(c.shape == (N_DEV, R, C) for c in contribs)
    tr = R
    if R * C > 256 * 1024 and R % 8 == 0:
        tr = 8
        while R % (tr * 2) == 0 and tr * 2 * C <= 256 * 1024:
            tr *= 2
    c1 = 1.0 - ADAM_B1 ** ADAM_STEP
    c2 = 1.0 - ADAM_B2 ** ADAM_STEP

    def body(*refs):
        c_refs = refs[:L]
        w_ref, m_ref, v_ref, g_ref, d_ref, nm_ref, nv_ref = refs[L:]
        layer = pl.program_id(0)
        for lp in range(L):
            @pl.when(layer == lp)
            def _(c_ref=c_refs[lp]):
                g = c_ref[0].astype(F32)
                for s in range(1, N_DEV):
                    g = g + c_ref[s].astype(F32)
                g_ref[...] = g

        g = g_ref[...]
        nm = ADAM_B1 * m_ref[...] + (1.0 - ADAM_B1) * g
        nv = ADAM_B2 * v_ref[...] + (1.0 - ADAM_B2) * (g * g)
        nm_ref[...] = nm
        nv_ref[...] = nv
        d_ref[...] = -ADAM_LR * ((nm / c1) / (jnp.sqrt(nv / c2) + ADAM_EPS) + ADAM_WD * w_ref[...])

    def contrib_spec(lp):
        return pl.BlockSpec((N_DEV, tr, C), lambda l, i: (0, jnp.where(l == lp, i, 0), 0))

    blk = pl.BlockSpec((None, tr, C), lambda l, i: (l, i, 0))
    out = jax.ShapeDtypeStruct((L, R, C), F32)
    return pl.pallas_call(
        body, name=name, grid=(L, R // tr),
        in_specs=[contrib_spec(lp) for lp in range(L)] + [blk, blk, blk],
        out_specs=[blk, blk, blk, blk], out_shape=[out, out, out, out],
        compiler_params=_cparams(("arbitrary", "arbitrary")),
    )(*contribs, w, m, v)


def _my_position():
    x, y, c = lax.axis_index("x"), lax.axis_index("y"), lax.axis_index("c")
    return x, y, c, 4 * x + 2 * y + c


def _peers(x, y, c):
    out = []
    for r in range(1, N_DEV):
        px = 1 - x if r & 4 else x
        py = 1 - y if r & 2 else y
        pc = 1 - c if r & 1 else c
        out.append(((px, py, pc), 4 * px + 2 * py + pc))
    return out


def _run_exchange(plan, n, send_sems, recv_sems, local_sems):
    x, y, c, me = _my_position()
    copies = []
    for t in range(n):
        src, dst = plan(t, None, me)
        own = pltpu.make_async_copy(src, dst, local_sems.at[t])
        own.start()
        copies.append(own)
    remote = []
    for r, (peer, peer_index) in enumerate(_peers(x, y, c)):
        for t in range(n):
            src, dst = plan(t, peer_index, me)
            cp = pltpu.make_async_remote_copy(src_ref=src, dst_ref=dst, send_sem=send_sems.at[t, r],
                                              recv_sem=recv_sems.at[t, r], device_id=peer,
                                              device_id_type=pl.DeviceIdType.MESH)
            cp.start()
            remote.append(cp)
    for cp in remote:
        cp.wait_send()
    for cp in remote:
        cp.wait_recv()
    for cp in copies:
        cp.wait()


def all_gather(blocks, *, name):
    n = len(blocks)

    def body(*refs):
        srcs, outs = refs[:n], refs[n:2 * n]
        send_sems, recv_sems, local_sems = refs[2 * n:]
        _run_exchange(lambda t, peer_index, me: (srcs[t], outs[t].at[me]), n, send_sems, recv_sems, local_sems)

    hbm = pl.BlockSpec(memory_space=pl.ANY)
    return pl.pallas_call(
        body, name=name, in_specs=[hbm] * n, out_specs=[hbm] * n,
        out_shape=[jax.ShapeDtypeStruct((N_DEV, *b.shape), b.dtype) for b in blocks],
        scratch_shapes=[pltpu.SemaphoreType.DMA((n, N_DEV - 1)), pltpu.SemaphoreType.DMA((n, N_DEV - 1)),
                        pltpu.SemaphoreType.DMA((n,))],
    )(*blocks)


_RELATIONS = {"scatter": (1, 2, 3, 4, 5, 6, 7), "gather": (1, 2, 4, 6), "forward": (2, 4, 6)}

_HBM = pl.BlockSpec(memory_space=pltpu.HBM)
_SEM = pl.BlockSpec(memory_space=pltpu.SEMAPHORE)


def _push_copies(kind, src_refs, land_refs, send_sems, recv_sems):
    x, y, c, me = _my_position()
    peers = _peers(x, y, c)
    relations = _RELATIONS[kind]
    copies = []
    for k, r in enumerate(relations):
        peer, peer_index = peers[r - 1]
        for t, land in enumerate(land_refs):
            if kind == "forward":
                src, dst, peer = land.at[peer_index], land.at[peer_index], peers[0][0]
            else:
                src = src_refs[t] if kind == "gather" else src_refs[t].at[peer_index]
                dst = land.at[me]
            copies.append(pltpu.make_async_remote_copy(
                src_ref=src, dst_ref=dst, send_sem=send_sems.at[t * len(relations) + k],
                recv_sem=recv_sems.at[t * len(relations) + k], device_id=peer,
                device_id_type=pl.DeviceIdType.MESH))
    return copies


def push_start(srcs, lands, *, kind, name):
    ns, n = len(srcs), len(lands)

    def body(*refs):
        send_sems, recv_sems = refs[ns + n:ns + n + 2]
        token = refs[-1]
        for cp in _push_copies(kind, refs[:ns], refs[ns:ns + n], send_sems, recv_sems):
            cp.start()
        token[...] = jnp.zeros_like(token)

    sems = pltpu.SemaphoreType.DMA((n * len(_RELATIONS[kind]),))
    arrays = [*srcs, *lands]
    out = pl.pallas_call(
        body, name=name,
        out_shape=(sems, sems, *[pltpu.HBM(a.shape, a.dtype) for a in arrays],
                   jax.ShapeDtypeStruct((8, LANES), F32)),
        in_specs=[_HBM] * (ns + n),
        out_specs=(_SEM, _SEM, *[_HBM] * (ns + n), pl.BlockSpec(memory_space=pltpu.VMEM)),
        input_output_aliases={i: 2 + i for i in range(ns + n)},
        compiler_params=pltpu.CompilerParams(has_side_effects=pltpu.SideEffectType.DATAFLOW_SIDE_EFFECTING),
    )(*[pltpu.with_memory_space_constraint(a, pltpu.HBM) for a in arrays])
    return out[0], out[1], list(out[2:2 + ns]), list(out[2 + ns:2 + ns + n]), out[-1]


def push_wait(send_sems, recv_sems, srcs, lands, after, *, kind, name):
    ns, n = len(srcs), len(lands)

    def body(*refs):
        for cp in _push_copies(kind, refs[:ns], refs[ns:ns + n], refs[ns + n], refs[ns + n + 1]):
            cp.wait_send()
            cp.wait_recv()

    arrays = [*srcs, *lands]
    out = pl.pallas_call(
        body, name=name,
        out_shape=tuple(pltpu.HBM(a.shape, a.dtype) for a in arrays),
        in_specs=[_HBM] * (ns + n) + [_SEM, _SEM, pl.BlockSpec(memory_space=pl.ANY)],
        out_specs=tuple([_HBM] * (ns + n)),
        input_output_aliases={i: i for i in range(ns + n)},
        compiler_params=pltpu.CompilerParams(has_side_effects=pltpu.SideEffectType.DATAFLOW_SIDE_EFFECTING),
    )(*arrays, send_sems, recv_sems, after)
    return list(out[ns:])


def _own_slot(block, me):
    zone = lax.empty((N_DEV, *block.shape), block.dtype)
    return lax.dynamic_update_slice(zone, block[None], (me,) + (0,) * block.ndim)


WEIGHT_NAMES = ('mix_norm_g', 'mem_norm_g', 'w_in', 'gate_b', 'conv_w', 'conv_b', 'conv_ln_g', 'conv_ln_b',
                'w_conv_out', 'rel_bias', 'w_attn_out', 'w_mem_kv', 'w_mem_out', 'w_o', 'ffn_norm_g', 'w_up',
                'ffn_conv_w', 'ffn_conv_b', 'w_down', 'final_norm_g')
BIG = (('w_in', False), ('w_conv_out', False), ('w_attn_out', False), ('w_mem_out', False), ('w_up', False),
       ('w_mem_kv', True), ('w_o', True), ('w_down', True))
BY_ROWS = dict(BIG)
GATHER_GROUPS = (('w_in',), ('w_mem_kv', 'w_conv_out', 'w_attn_out', 'w_mem_out', 'w_o'), ('w_up', 'w_down'))


MIN_SLOT_COLS = 512


def _as_matrix(gathered, by_rows):
    n, r, c = gathered.shape
    if by_rows:
        return gathered.reshape(1, n * r, c)
    if c >= MIN_SLOT_COLS:
        return gathered
    return jnp.transpose(gathered, (1, 0, 2)).reshape(1, r, n * c)


def _as_shards(grad, by_rows):
    s, r, c = grad.shape
    if by_rows:
        return grad.reshape(N_DEV, r // N_DEV, c)
    if s == N_DEV:
        return grad
    return jnp.transpose(grad.reshape(r, N_DEV, c // N_DEV), (1, 0, 2))


class _LayerWeights:
    def __init__(self, pending):
        self.pending = pending
        self.ready = {}

    def advance(self, after, name=None):
        for item in self.pending:
            if item['trip'] == 1 and (name is None or name in item['names']):
                lands = push_wait(*item['started'][:4], after, kind="gather", name=item['tag'] + "_wait1")
                item['started'] = push_start([], lands, kind="forward", name=item['tag'] + "_start2")
                item['trip'] = 2

    def get(self, name, after):
        self.advance(after, name)
        for item in self.pending:
            if name in item['names'] and item['trip'] == 2:
                got = push_wait(*item['started'][:4], after, kind="forward", name=item['tag'] + "_wait2")
                self.ready.update({n: _as_matrix(a, BY_ROWS[n]) for n, a in zip(item['names'], got)})
                item['trip'] = None
        return self.ready[name]

    def __getitem__(self, name):
        return self.ready[name]


def _forward_layer(h, mem2, p, l, dims, prefetch):
    W = p['gathered'][l]
    row = lambda name: p[name][l:l + 1]
    sv = {'h': h}
    sv['xn'] = rms_fwd(h, row('mix_norm_g'), name="mix_norm")
    w_in = W.get('w_in', sv['xn'])
    sv['proj'] = mm_nn(sv['xn'], w_in, out_dtype=BF16, after=prefetch(w_in), name="w_in")
    sv['s'], sv['c'] = conv_fwd(sv['proj'], p['conv_w_full'][l], row('conv_b'), row('conv_ln_g'),
                                row('conv_ln_b'), name="conv_module")
    sv['bias'] = attn_bias(p['rel_pad'][l], name="attn_bias")
    sv['att'] = attn_fwd(sv['proj'], sv['bias'], qcol=dims['qcol'], dattn=dims['DA'], name="chunk_attn")
    sv['mn'] = rms_fwd(mem2, row('mem_norm_g'), name="mem_norm")
    sv['kv'] = mm_nn(sv['mn'], W.get('w_mem_kv', sv['att']), out_dtype=BF16, name="w_mem_kv")
    sv['mo'] = memattn_fwd(sv['proj'], sv['kv'], qcol=dims['mcol'], name="mem_attn")
    sv['ys'] = (mm_nn(sv['s'], W['w_conv_out'], out_dtype=BF16, name="w_conv_out"),
                mm_nn(sv['att'], W['w_attn_out'], out_dtype=BF16, name="w_attn_out"),
                mm_nn(sv['mo'], W['w_mem_out'], out_dtype=BF16, name="w_mem_out"))
    sv['merged'] = merge_fwd(sv['proj'], row('gate_b'), sv['ys'], gcol=dims['gcol'], name="merge")
    sv['h1'] = mm_nn(sv['merged'], W['w_o'], out_dtype=F32, residual=h, name="w_o")
    sv['hn'] = rms_fwd(sv['h1'], row('ffn_norm_g'), name="ffn_norm")
    sv['up0'] = mm_nn(sv['hn'], W.get('w_up', sv['hn']), out_dtype=BF16, name="w_up")
    sv['act'] = ffn_act_fwd(sv['up0'], p['ffn_conv_w_full'][l], row('ffn_conv_b'), name="ffn_act")
    if l + 1 < len(p['gathered']):
        p['gathered'][l + 1].advance(sv['act'])
    h2 = mm_nn(sv['act'], W['w_down'], out_dtype=F32, residual=sv['h1'], tk=W_DOWN_TK, name="w_down")
    return h2, sv


SCATTER_GROUPS = (('w_down', 'w_up'), ('w_o', 'w_conv_out', 'w_attn_out', 'w_mem_out', 'w_mem_kv'), ('w_in',))


def _backward_layer(dh, dhb, mem2, sv, p, l, dims, scatter, after=None):
    W = p['gathered'][l]
    row = lambda name: p[name][l:l + 1]
    g, small = {}, {}
    dact = mm_nt(dhb, W['w_down'], out_dtype=BF16, after=after, name="d_act")
    g['w_down'] = mm_tn(sv['act'], dhb, slots=1, out_dtype=BF16, tm=W_DOWN_TM, name="g_w_down")
    dup0, dtaps, dbias = ffn_bwd(dact, sv['up0'], p['ffn_conv_w_full'][l], row('ffn_conv_b'), name="ffn_bwd")
    small['ffn_conv_w'] = jnp.transpose(dtaps, (1, 0, 2)).reshape(FFN_CONV_WIDTH, -1)
    small['ffn_conv_b'] = jnp.transpose(dbias, (1, 0, 2)).reshape(1, -1)
    dhn = mm_nt(dup0, W['w_up'], lead="planes", out_dtype=BF16, name="d_hn")
    g['w_up'] = mm_tn(sv['hn'], dup0, lead="planes", slots=W['w_up'].shape[0], out_dtype=BF16, name="g_w_up")
    dh1, dh1b, small['ffn_norm_g'] = rms_bwd(sv['h1'], row('ffn_norm_g'), dhn, dh, name="ffn_norm_bwd")
    dmerged = mm_nt(dh1b, W['w_o'], out_dtype=BF16, after=scatter(SCATTER_GROUPS[0], g), name="d_merged")
    g['w_o'] = mm_tn(sv['merged'], dh1b, slots=1, out_dtype=BF16, name="g_w_o")
    dproj, dy, small['gate_b'] = merge_bwd(dmerged, sv['proj'], row('gate_b'), sv['ys'], gcol=dims['gcol'],
                                           name="merge_bwd")
    ds = mm_nt(dy, W['w_conv_out'], lead=0, out_dtype=BF16, name="d_conv_out")
    g['w_conv_out'] = mm_tn(sv['s'], dy, lead=0, slots=W['w_conv_out'].shape[0], out_dtype=BF16,
                            name="g_w_conv_out")
    dc, small['conv_ln_g'], small['conv_ln_b'] = conv_bwd_ln(ds, sv['c'], row('conv_ln_g'), row('conv_ln_b'),
                                                             name="conv_ln_bwd")
    dproj, small['conv_w'], small['conv_b'] = conv_bwd_taps(dc, sv['proj'], p['conv_w_full'][l], dproj,
                                                            name="conv_taps_bwd")
    datt = mm_nt(dy, W['w_attn_out'], lead=1, out_dtype=BF16, name="d_attn_out")
    g['w_attn_out'] = mm_tn(sv['att'], dy, lead=1, slots=W['w_attn_out'].shape[0], out_dtype=BF16,
                            name="g_w_attn_out")
    dproj, dkw, dvw, dbias = attn_bwd(sv['proj'], sv['bias'], datt, dproj, qcol=dims['qcol'], dattn=dims['DA'],
                                      name="chunk_attn_bwd")
    dproj = window_sum(dkw, dproj, col=dims['qcol'] + dims['DA'], name="dk_windows")
    dproj = window_sum(dvw, dproj, col=dims['qcol'] + 2 * dims['DA'], name="dv_windows")
    small['rel_bias'] = attn_bias_bwd(dbias, name="attn_bias_bwd")[:, :2 * MAX_REL + 1]
    dmo = mm_nt(dy, W['w_mem_out'], lead=2, out_dtype=BF16, name="d_mem_out")
    g['w_mem_out'] = mm_tn(sv['mo'], dy, lead=2, slots=W['w_mem_out'].shape[0], out_dtype=BF16,
                           name="g_w_mem_out")
    dproj, dkv = memattn_bwd(sv['proj'], sv['kv'], dmo, dproj, qcol=dims['mcol'], name="mem_attn_bwd")
    dkvb = dkv.astype(BF16)
    g['w_mem_kv'] = mm_tn(sv['mn'], dkvb, slots=1, out_dtype=BF16, name="g_w_mem_kv")
    dmn = mm_nt(dkvb, W['w_mem_kv'], out_dtype=BF16, name="d_mem_norm")
    _, _, small['mem_norm_g'] = rms_bwd(mem2, row('mem_norm_g'), dmn, jnp.zeros(mem2.shape, F32),
                                        name="mem_norm_bwd")
    dxn = mm_nt(dproj, W['w_in'], out_dtype=BF16, after=scatter(SCATTER_GROUPS[1], g), name="d_xn")
    g['w_in'] = mm_tn(sv['xn'], dproj, slots=N_DEV, out_dtype=BF16, name="g_w_in")
    dh0, dh0b, small['mix_norm_g'] = rms_bwd(sv['h'], row('mix_norm_g'), dxn, dh1, name="mix_norm_bwd")
    return dh0, dh0b, g['w_in'], small


def kernel(x, mem, mix_norm_g, mem_norm_g, w_in, gate_b, conv_w, conv_b, conv_ln_g, conv_ln_b, w_conv_out, rel_bias, w_attn_out, w_mem_kv, w_mem_out, w_o, ffn_norm_g, w_up, ffn_conv_w, ffn_conv_b, w_down, final_norm_g, loss_target, m_mix_norm_g, m_mem_norm_g, m_w_in, m_gate_b, m_conv_w, m_conv_b, m_conv_ln_g, m_conv_ln_b, m_w_conv_out, m_rel_bias, m_w_attn_out, m_w_mem_kv, m_w_mem_out, m_w_o, m_ffn_norm_g, m_w_up, m_ffn_conv_w, m_ffn_conv_b, m_w_down, m_final_norm_g, v_mix_norm_g, v_mem_norm_g, v_w_in, v_gate_b, v_conv_w, v_conv_b, v_conv_ln_g, v_conv_ln_b, v_w_conv_out, v_rel_bias, v_w_attn_out, v_w_mem_kv, v_w_mem_out, v_w_o, v_ffn_norm_g, v_w_up, v_ffn_conv_w, v_ffn_conv_b, v_w_down, v_final_norm_g):
    env = locals()
    w = {n: env[n] for n in WEIGHT_NAMES}
    mom = {n: env['m_' + n] for n in WEIGHT_NAMES}
    var = {n: env['v_' + n] for n in WEIGHT_NAMES}
    T, D = x.shape[-2:]
    L = w_in.shape[0]
    DC = conv_b.shape[-1]
    DA = N_DEV * w_attn_out.shape[-1] // 2
    dims = {'DA': DA, 'qcol': 2 * DC, 'mcol': 2 * DC + 3 * DA, 'gcol': 2 * DC + 3 * DA + D // 2}
    x2, mem2, target = x.reshape(T, D), mem.reshape(-1, D), loss_target.reshape(T, D)
    me = 4 * lax.axis_index("x") + 2 * lax.axis_index("y") + lax.axis_index("c")

    p = dict(w)

    def start_gather(l, names, first, tag):
        blocks = [w[n][l].astype(BF16) for n in names]
        first, blocks = lax.optimization_barrier((first, blocks))
        tag = "gather_%d%s" % (l, tag)
        started = push_start(blocks, [_own_slot(b, me) for b in blocks], kind="gather", name=tag + "_start1")
        return {'names': names, 'started': started, 'trip': 1, 'tag': tag}

    taps, ffn_taps = all_gather([conv_w, ffn_conv_w], name="gather_taps")
    p['conv_w_full'] = jnp.moveaxis(taps, 0, 2).reshape(L, conv_w.shape[1], -1)
    p['ffn_conv_w_full'] = jnp.moveaxis(ffn_taps, 0, 2).reshape(L, ffn_conv_w.shape[1], -1)
    p['rel_pad'] = jnp.pad(rel_bias, ((0, 0), (0, 0), (0, REL_PAD - rel_bias.shape[-1])))
    pending, first = [], taps
    for names, tag in zip(GATHER_GROUPS, "abc"):
        pending.append(start_gather(0, names, first, tag))
        first = pending[-1]['started'][4]
    p['gathered'] = [_LayerWeights(pending)] + [None] * (L - 1)

    h = x2
    saved = []
    for l in range(L):
        def prefetch(first, l=l):
            if l + 1 == L:
                return None
            nxt = start_gather(l + 1, [n for n, _ in BIG], first, "")
            p['gathered'][l + 1] = _LayerWeights([nxt])
            return nxt['started'][4]

        h, sv = _forward_layer(h, mem2, p, l, dims, prefetch)
        saved.append(sv)
    loss_part, dh, dhb, d_final = loss_head(h, final_norm_g.reshape(1, D), target, name="loss_head")
    loss = lax.psum(loss_part[0, 0], ("x", "y", "c"))

    small, landed, inflight = [None] * L, [{} for _ in range(L)], []

    def scatter_start(l, names, g):
        tag = "abc"[SCATTER_GROUPS.index(names)]
        grads = [_as_shards(g[n], BY_ROWS[n]) for n in names]
        zones = [_own_slot(lax.dynamic_index_in_dim(a, me, 0, keepdims=False), me) for a in grads]
        started = push_start(grads, zones, kind="scatter", name="scatter_start_%d%s" % (l, tag))
        inflight.append((l, names, started, "scatter_wait_%d%s" % (l, tag)))
        return started[4]

    def scatter_finish(after, groups):
        for item in [i for i in inflight if (i[0], i[1]) in groups]:
            l, names, started, wait_name = item
            landed[l].update(zip(names, push_wait(*started[:4], after, kind="scatter", name=wait_name)))
            inflight.remove(item)

    token = None
    for l in reversed(range(L)):
        dh, dhb, g_w_in, small[l] = _backward_layer(dh, dhb, mem2, saved[l], p, l, dims,
                                                    functools.partial(scatter_start, l), after=token)
        scatter_finish(dh, [(l + 1, names) for names in SCATTER_GROUPS])
        if l > 0:
            token = scatter_start(l, SCATTER_GROUPS[2], {'w_in': g_w_in})

    big_names = [n for n, _ in BIG]
    small_names = [n for n in WEIGHT_NAMES if n not in big_names and n != 'final_norm_g']
    stacked = [jnp.stack([small[l][n] for l in range(L)]) for n in small_names] + [d_final]
    got = dict(zip(small_names + ['final_norm_g'], all_gather(stacked, name="gather_small_grads")))
    g_w_in, _ = lax.optimization_barrier((g_w_in, got['final_norm_g']))
    token = scatter_start(0, SCATTER_GROUPS[2], {'w_in': g_w_in})
    scatter_finish(token, [(0, names) for names in SCATTER_GROUPS[:2]])
    for n in ('conv_w', 'ffn_conv_w'):
        cs = w[n].shape[-1]
        got[n] = lax.dynamic_slice_in_dim(got[n], me * cs, cs, axis=3)
    contribs = {n: [got[n].reshape(N_DEV, -1, w[n].shape[-1])] for n in small_names + ['final_norm_g']}

    outs = {}
    for n in [n for n in WEIGHT_NAMES if n != 'w_in'] + ['w_in']:
        if n == 'w_in':
            done = lax.optimization_barrier(tuple(o[0] for o in outs.values()))
            scatter_finish(done[0], [(0, SCATTER_GROUPS[2])])
        if n in big_names:
            contribs[n] = [landed[l][n] for l in range(L)]
        shp = (len(contribs[n]),) + contribs[n][0].shape[1:]
        res = adamw(contribs[n], w[n].reshape(shp), mom[n].reshape(shp), var[n].reshape(shp), name="adamw_" + n)
        outs[n] = [r.reshape(w[n].shape) for r in res]
    return (loss, dh.reshape(x.shape),
            *[outs[n][0] for n in WEIGHT_NAMES], *[outs[n][1] for n in WEIGHT_NAMES],
            *[outs[n][2] for n in WEIGHT_NAMES], *[outs[n][3] for n in WEIGHT_NAMES])
```
